```python
import math
import jax, jax.numpy as jnp
from jax import lax
import numpy as np

D_MODEL = 1024
BATCH = 2
SEQ = 8192
DEPTH = 1
DEC_BATCH = 32
DEC_SEQ = 8
PAST_LEN = 8192
PAGE_SIZE = 128

N_HEADS = 8
N_KV = 2
HEAD_DIM = 64
HPG = N_HEADS // N_KV
CMP_LEN = 32
CMP_STRIDE = 16
SEL_BLOCK = 64
N_SEL = 16
WINDOW = 512
Q_BLOCK = 128
SSM_WIDTH = 512
SSM_GROUP = 16
N_SSM_GROUPS = SSM_WIDTH // SSM_GROUP
SSM_STATE = 64
N_KEYS = 128
N_EXPERTS = N_KEYS * N_KEYS
PEER_HEADS = 8
PEER_DK = 128
PEER_TOPK = 16
PEER_BLOCK = 128

NSA_WIDTH = N_HEADS * HEAD_DIM
KV_WIDTH = N_KV * HEAD_DIM
MIX_WIDTH = NSA_WIDTH + SSM_WIDTH
GATE_WIDTH = 3 * N_HEADS
IN_SIZES = [NSA_WIDTH] + [KV_WIDTH] * 6 + [GATE_WIDTH, SSM_WIDTH, SSM_WIDTH]
IN_COLS = sum(IN_SIZES)
IN_SPLITS = [sum(IN_SIZES[:i + 1]) for i in range(len(IN_SIZES) - 1)]
EPS = 1e-6
NEG = -1e30
FORCE = 1e4

kernel_name = 'nsa_s5_peer_hybrid_step'


def rmsnorm(x, g):
    xf = x.astype(jnp.float32)
    y = xf * lax.rsqrt(jnp.mean(xf * xf, axis=-1, keepdims=True) + EPS) * g.astype(jnp.float32)
    return y.astype(x.dtype)


def project(x, norm_g, w_in):
    B, T = x.shape[:2]
    h = rmsnorm(x, norm_g) @ w_in
    q, kc, vc, ks, vs, kw, vw, gt, u, z = jnp.split(h, IN_SPLITS, axis=-1)
    q = q.reshape(B, T, N_KV, HPG, HEAD_DIM)
    kv = [t.reshape(B, T, N_KV, HEAD_DIM) for t in (kc, vc, ks, vs, kw, vw)]
    return q, kv, gt, u, z


def compress(kv, w1, w2, pe):
    B, L = kv.shape[:2]
    n_cmp = (L - CMP_LEN) // CMP_STRIDE + 1
    R = CMP_LEN // CMP_STRIDE
    n_chunk = n_cmp + R - 1
    chunks = kv[:, :n_chunk * CMP_STRIDE].reshape(B, n_chunk, CMP_STRIDE, N_KV, HEAD_DIM)
    w1r = w1.reshape(R, CMP_STRIDE, HEAD_DIM, HEAD_DIM)
    per = pe.reshape(R, CMP_STRIDE, HEAD_DIM)
    hid = sum(jnp.einsum('bnsgd,sde->bnge', chunks[:, r:r + n_cmp] + per[r][:, None, :], w1r[r],
                         preferred_element_type=jnp.float32) for r in range(R))
    return jnp.einsum('bnge,ef->bngf', jax.nn.gelu(hid), w2.astype(jnp.float32))


def to_blocks(k):
    B, L = k.shape[:2]
    n_blk = -(-L // SEL_BLOCK)
    k = jnp.pad(k, ((0, 0), (0, n_blk * SEL_BLOCK - L), (0, 0), (0, 0)))
    return k.reshape(B, n_blk, SEL_BLOCK, N_KV, HEAD_DIM).transpose(0, 3, 1, 2, 4)


def nsa_attend(q, gates, t_pos, k_cmp, v_cmp, k_blk, v_blk, k_win, v_win, win_pos):
    f32 = jnp.float32
    B, Tq = q.shape[:2]
    scale = HEAD_DIM ** -0.5
    n_cmp = k_cmp.shape[1]
    cmp_start = jnp.arange(n_cmp, dtype=jnp.int32) * CMP_STRIDE
    cmp_ok = (cmp_start + CMP_LEN - 1)[None, :] <= t_pos[:, None]
    s = jnp.einsum('btghd,bngd->btghn', q, k_cmp, preferred_element_type=f32) * scale
    s = jnp.where(cmp_ok[None, :, None, None, :], s, NEG)
    p_cmp = jax.nn.softmax(s, axis=-1) * jnp.any(cmp_ok, axis=-1)[None, :, None, None, None]
    o_cmp = jnp.einsum('btghn,bngd->btghd', p_cmp, v_cmp.astype(f32))
    n_blk = k_blk.shape[2]
    blk_start = jnp.arange(n_blk, dtype=jnp.int32) * SEL_BLOCK
    cover = ((cmp_start[:, None] < blk_start[None, :] + SEL_BLOCK)
             & (cmp_start[:, None] + CMP_LEN > blk_start[None, :])).astype(f32)
    imp = jnp.einsum('btghn,ns->btgs', p_cmp, cover)
    cur = t_pos // SEL_BLOCK
    bid = jnp.arange(n_blk, dtype=jnp.int32)
    forced = (bid[None, :] == 0) | (bid[None, :] == cur[:, None]) | (bid[None, :] == cur[:, None] - 1)
    imp = jnp.where(forced[None, :, None, :], FORCE, imp)
    imp = jnp.where((blk_start[None, :] <= t_pos[:, None])[None, :, None, :], imp, NEG)
    n_top = min(N_SEL, n_blk)
    _, top = lax.top_k(imp, n_top)
    bi = jnp.arange(B)[:, None, None, None]
    gi = jnp.arange(N_KV)[None, None, :, None]
    kg = k_blk[bi, gi, top].reshape(B, Tq, N_KV, n_top * SEL_BLOCK, HEAD_DIM)
    vg = v_blk[bi, gi, top].reshape(B, Tq, N_KV, n_top * SEL_BLOCK, HEAD_DIM)
    tok = (top[..., None] * SEL_BLOCK + jnp.arange(SEL_BLOCK, dtype=jnp.int32)).reshape(B, Tq, N_KV, n_top * SEL_BLOCK)
    sel_ok = tok <= t_pos[None, :, None, None]
    s = jnp.einsum('btghd,btgmd->btghm', q, kg, preferred_element_type=f32) * scale
    s = jnp.where(sel_ok[:, :, :, None, :], s, NEG)
    o_sel = jnp.einsum('btghm,btgmd->btghd', jax.nn.softmax(s, axis=-1), vg.astype(f32))
    dist = t_pos[:, None] - win_pos[None, :]
    win_ok = (dist >= 0) & (dist <= WINDOW) & (win_pos[None, :] >= 0)
    s = jnp.einsum('btghd,bwgd->btghw', q, k_win, preferred_element_type=f32) * scale
    s = jnp.where(win_ok[None, :, None, None, :], s, NEG)
    o_win = jnp.einsum('btghw,bwgd->btghd', jax.nn.softmax(s, axis=-1), v_win.astype(f32))
    g = jax.nn.sigmoid(gates.astype(f32)).reshape(B, Tq, N_KV, HPG, 3)
    o = g[..., 0:1] * o_cmp + g[..., 1:2] * o_sel + g[..., 2:3] * o_win
    return o.reshape(B, Tq, NSA_WIDTH)


def nsa_prompt(q, gates, kc, vc, ks, vs, kw, vw, w1, w2, pe):
    B, T = q.shape[:2]
    k_cmp = compress(kc, w1[0], w2[0], pe[0])
    v_cmp = compress(vc, w1[1], w2[1], pe[1])
    k_blk, v_blk = to_blocks(ks), to_blocks(vs)
    pad = ((0, 0), (WINDOW, 0), (0, 0), (0, 0))
    kwp, vwp = jnp.pad(kw, pad), jnp.pad(vw, pad)

    def body(i):
        s0 = i * Q_BLOCK
        qb = lax.dynamic_slice_in_dim(q, s0, Q_BLOCK, axis=1)
        gb = lax.dynamic_slice_in_dim(gates, s0, Q_BLOCK, axis=1)
        kb = lax.dynamic_slice_in_dim(kwp, s0, WINDOW + Q_BLOCK, axis=1)
        vb = lax.dynamic_slice_in_dim(vwp, s0, WINDOW + Q_BLOCK, axis=1)
        t_pos = s0 + jnp.arange(Q_BLOCK, dtype=jnp.int32)
        win_pos = s0 - WINDOW + jnp.arange(WINDOW + Q_BLOCK, dtype=jnp.int32)
        return nsa_attend(qb, gb, t_pos, k_cmp, v_cmp, k_blk, v_blk, kb, vb, win_pos)

    out = lax.map(body, jnp.arange(T // Q_BLOCK, dtype=jnp.int32))
    return out.transpose(1, 0, 2, 3).reshape(B, T, NSA_WIDTH)


def nsa_sample(q, gates, kc, vc, ks, vs, kw, vw, cache_kv_l, cache_win_l, page_table, w1, w2, pe):
    DB, T = q.shape[:2]
    past = cache_kv_l[page_table]
    past = past.reshape(DB, -1, 4, N_KV, HEAD_DIM)
    past_len = past.shape[1]
    new = jnp.stack([kc, vc, ks, vs], axis=2)
    full = jnp.concatenate([past, new.astype(past.dtype)], axis=1)
    k_cmp = compress(full[:, :, 0], w1[0], w2[0], pe[0])
    v_cmp = compress(full[:, :, 1], w1[1], w2[1], pe[1])
    k_blk, v_blk = to_blocks(full[:, :, 2]), to_blocks(full[:, :, 3])
    wb = cache_win_l.shape[1]
    win_all = jnp.concatenate([cache_win_l, jnp.stack([kw, vw], axis=2).astype(cache_win_l.dtype)], axis=1)
    win_pos = past_len - wb + jnp.arange(wb + T, dtype=jnp.int32)
    t_pos = past_len + jnp.arange(T, dtype=jnp.int32)
    out = nsa_attend(q, gates, t_pos, k_cmp, v_cmp, k_blk, v_blk, win_all[:, :, 0], win_all[:, :, 1], win_pos)
    return out, new, win_all[:, -wb:]


def _ssm_combine(left, right):
    a1, b1 = left
    a2, b2 = right
    return (a1 * a2, a2 * b1 + b2)


def s5_mixer(u, z, h0, lam_re, lam_im, log_dt, b_re, b_im, c_re, c_im, d_skip):
    f32 = jnp.float32
    B, T = u.shape[:2]
    uf = u.reshape(B, T, N_SSM_GROUPS, SSM_GROUP).astype(f32)
    lam = lax.complex(lam_re.astype(f32), lam_im.astype(f32))
    dt = jnp.exp(log_dt.astype(f32))[:, None]
    lam_bar = jnp.exp(lam * dt)
    b_bar = ((lam_bar - 1.0) / lam)[..., None] * lax.complex(b_re.astype(f32), b_im.astype(f32))
    bu = jnp.einsum('gph,btgh->btgp', b_bar, uf.astype(jnp.complex64))
    a = jnp.broadcast_to(lam_bar, bu.shape)
    a_cum, b_cum = lax.associative_scan(_ssm_combine, (a, bu), axis=1)
    h = a_cum * h0[:, None] + b_cum
    cm = lax.complex(c_re.astype(f32), c_im.astype(f32))
    y = jnp.einsum('ghp,btgp->btgh', cm, h).real + d_skip.astype(f32) * uf
    y = y.reshape(B, T, SSM_WIDTH)
    out = jax.nn.gelu(y) * jax.nn.sigmoid(z.astype(f32))
    return out, h[:, -1]


def peer(xn, w_q, sub_k1, sub_k2, u_tab, v_tab):
    f32 = jnp.float32
    n = xn.shape[0]
    pad = (-n) % PEER_BLOCK
    xp = jnp.pad(xn, ((0, pad), (0, 0))).reshape(-1, PEER_BLOCK, D_MODEL)

    def blk(xb):
        q = (xb @ w_q).astype(f32).reshape(PEER_BLOCK, PEER_HEADS, 2, PEER_DK // 2)
        s1 = jnp.einsum('thd,hkd->thk', q[:, :, 0], sub_k1.astype(f32))
        s2 = jnp.einsum('thd,hkd->thk', q[:, :, 1], sub_k2.astype(f32))
        v1, i1 = lax.top_k(s1, PEER_TOPK)
        v2, i2 = lax.top_k(s2, PEER_TOPK)
        cand = (v1[..., :, None] + v2[..., None, :]).reshape(PEER_BLOCK, PEER_HEADS, PEER_TOPK * PEER_TOPK)
        cid = (i1[..., :, None] * N_KEYS + i2[..., None, :]).reshape(PEER_BLOCK, PEER_HEADS, PEER_TOPK * PEER_TOPK)
        sv, si = lax.top_k(cand, PEER_TOPK)
        eid = jnp.take_along_axis(cid, si, axis=-1)
        gate = jax.nn.softmax(sv, axis=-1)
        act = jax.nn.gelu(jnp.einsum('thkd,td->thk', u_tab[eid], xb, preferred_element_type=f32))
        return jnp.einsum('thk,thkd->td', gate * act, v_tab[eid].astype(f32))

    return lax.map(blk, xp).reshape(-1, D_MODEL)[:n]


def channel_mix(x, norm_g, w_q, sub_k1, sub_k2, u_tab, v_tab):
    B, T, D = x.shape
    y = peer(rmsnorm(x, norm_g).reshape(B * T, D), w_q, sub_k1, sub_k2, u_tab, v_tab)
    return x + y.reshape(B, T, D).astype(x.dtype)


def setup_inputs(seed: int = 0) -> dict:
    key = jax.random.key(seed)
    ks = jax.random.split(key, 32)
    f32 = jnp.float32
    n_pages = PAST_LEN // PAGE_SIZE
    n_used = DEC_BATCH * n_pages
    n_pool = n_used + max(1, n_used // 4)
    win_buf = min(WINDOW, PAST_LEN)

    def nrm(k, shape, s):
        return jax.random.normal(k, shape, f32) * s

    page_table = jax.random.permutation(ks[0], n_pool)[:n_used].reshape(DEC_BATCH, n_pages).astype(jnp.int32)
    lam_im0 = jnp.pi * jnp.arange(SSM_STATE, dtype=f32)
    gshape = (DEPTH, N_SSM_GROUPS, SSM_STATE)
    return {
        'x_prompt': nrm(ks[1], (BATCH, SEQ, D_MODEL), 1.0),
        'x_sample': nrm(ks[2], (DEC_BATCH, DEC_SEQ, D_MODEL), 1.0),
        'cache_kv': nrm(ks[3], (DEPTH, n_pool, PAGE_SIZE, 4, N_KV, HEAD_DIM), 1.0),
        'cache_win': nrm(ks[4], (DEPTH, DEC_BATCH, win_buf, 2, N_KV, HEAD_DIM), 1.0),
        'state_ssm': nrm(ks[5], (DEPTH, DEC_BATCH, N_SSM_GROUPS, SSM_STATE, 2), 0.1),
        'page_table': page_table,
        'norm_mix': 1.0 + nrm(ks[6], (DEPTH, D_MODEL), 0.01),
        'w_in': nrm(ks[7], (DEPTH, D_MODEL, IN_COLS), D_MODEL ** -0.5),
        'w_cmp1': nrm(ks[8], (DEPTH, 2, CMP_LEN, HEAD_DIM, HEAD_DIM), (CMP_LEN * HEAD_DIM) ** -0.5),
        'w_cmp2': nrm(ks[9], (DEPTH, 2, HEAD_DIM, HEAD_DIM), HEAD_DIM ** -0.5),
        'pe_cmp': nrm(ks[10], (DEPTH, 2, CMP_LEN, HEAD_DIM), 0.02),
        'lam_re': -0.5 + nrm(ks[11], gshape, 0.01),
        'lam_im': lam_im0 + nrm(ks[12], gshape, 0.01),
        'log_dt': jax.random.uniform(ks[13], (DEPTH, N_SSM_GROUPS), f32, math.log(1e-3), math.log(1e-1)),
        'b_re': nrm(ks[14], (DEPTH, N_SSM_GROUPS, SSM_STATE, SSM_GROUP), (2 * SSM_GROUP) ** -0.5),
        'b_im': nrm(ks[15], (DEPTH, N_SSM_GROUPS, SSM_STATE, SSM_GROUP), (2 * SSM_GROUP) ** -0.5),
        'c_re': nrm(ks[16], (DEPTH, N_SSM_GROUPS, SSM_GROUP, SSM_STATE), 0.5),
        'c_im': nrm(ks[17], (DEPTH, N_SSM_GROUPS, SSM_GROUP, SSM_STATE), 0.5),
        'd_skip': nrm(ks[18], (DEPTH, N_SSM_GROUPS, SSM_GROUP), 1.0),
        'w_out': nrm(ks[19], (DEPTH, MIX_WIDTH, D_MODEL), MIX_WIDTH ** -0.5),
        'norm_ffn': 1.0 + nrm(ks[20], (DEPTH, D_MODEL), 0.01),
        'w_q_peer': nrm(ks[21], (DEPTH, D_MODEL, PEER_HEADS * PEER_DK), D_MODEL ** -0.5),
        'sub_k1': nrm(ks[22], (DEPTH, PEER_HEADS, N_KEYS, PEER_DK // 2), (PEER_DK // 2) ** -0.5),
        'sub_k2': nrm(ks[23], (DEPTH, PEER_HEADS, N_KEYS, PEER_DK // 2), (PEER_DK // 2) ** -0.5),
        'u_tab': nrm(ks[24], (DEPTH, N_EXPERTS, D_MODEL), D_MODEL ** -0.5),
        'v_tab': nrm(ks[25], (DEPTH, N_EXPERTS, D_MODEL), 0.1),
        'norm_final': 1.0 + nrm(ks[26], (D_MODEL,), 0.01),
    }


def reference(x_prompt, x_sample, cache_kv, cache_win, state_ssm, page_table, norm_mix, w_in,
              w_cmp1, w_cmp2, pe_cmp, lam_re, lam_im, log_dt, b_re, b_im, c_re, c_im, d_skip,
              w_out, norm_ffn, w_q_peer, sub_k1, sub_k2, u_tab, v_tab, norm_final):
    f32 = jnp.float32
    yp, ys = x_prompt, x_sample
    kv_p, kv_s, win_p, win_s, ssm_p, ssm_s = [], [], [], [], [], []
    for l in range(DEPTH):
        ssm_w = (lam_re[l], lam_im[l], log_dt[l], b_re[l], b_im[l], c_re[l], c_im[l], d_skip[l])
        peer_w = (norm_ffn[l], w_q_peer[l], sub_k1[l], sub_k2[l], u_tab[l], v_tab[l])
        B, T = yp.shape[:2]
        q, (kc, vc, ks, vs, kw, vw), gt, u, z = project(yp, norm_mix[l], w_in[l])
        a_out = nsa_prompt(q, gt, kc, vc, ks, vs, kw, vw, w_cmp1[l], w_cmp2[l], pe_cmp[l])
        h0 = jnp.zeros((B, N_SSM_GROUPS, SSM_STATE), jnp.complex64)
        s_out, h_last = s5_mixer(u, z, h0, *ssm_w)
        yp = yp + (jnp.concatenate([a_out, s_out], axis=-1).astype(yp.dtype) @ w_out[l])
        yp = channel_mix(yp, *peer_w)
        kv_p.append(jnp.stack([kc, vc, ks, vs], axis=2))
        win_p.append(jnp.stack([kw, vw], axis=2)[:, -min(WINDOW, T):])
        ssm_p.append(jnp.stack([h_last.real, h_last.imag], axis=-1).astype(yp.dtype))
        q, (kc, vc, ks, vs, kw, vw), gt, u, z = project(ys, norm_mix[l], w_in[l])
        a_out, kv_new, win_new = nsa_sample(q, gt, kc, vc, ks, vs, kw, vw, cache_kv[l], cache_win[l],
                                            page_table, w_cmp1[l], w_cmp2[l], pe_cmp[l])
        st = state_ssm[l]
        h0 = lax.complex(st[..., 0].astype(f32), st[..., 1].astype(f32))
        s_out, h_last = s5_mixer(u, z, h0, *ssm_w)
        ys = ys + (jnp.concatenate([a_out, s_out], axis=-1).astype(ys.dtype) @ w_out[l])
        ys = channel_mix(ys, *peer_w)
        kv_s.append(kv_new)
        win_s.append(win_new)
        ssm_s.append(jnp.stack([h_last.real, h_last.imag], axis=-1).astype(ys.dtype))
    y_prompt = rmsnorm(yp, norm_final)
    y_sample = rmsnorm(ys, norm_final)
    return (y_prompt, y_sample, jnp.stack(kv_p), jnp.stack(kv_s), jnp.stack(win_p), jnp.stack(win_s),
            jnp.stack(ssm_p), jnp.stack(ssm_s))
```

```python
import functools
import math

import jax
import jax.numpy as jnp
from jax import lax
from jax.experimental import pallas as pl
from jax.experimental.pallas import tpu as pltpu

D_MODEL = 1024
DEPTH = 1
PAGE_SIZE = 128
N_HEADS = 8
N_KV = 2
HEAD_DIM = 64
HPG = N_HEADS // N_KV
CMP_LEN = 32
CMP_STRIDE = 16
SEL_BLOCK = 64
N_SEL = 16
WINDOW = 512
SSM_WIDTH = 512
SSM_GROUP = 16
N_SSM_GROUPS = SSM_WIDTH // SSM_GROUP
SSM_STATE = 64
N_KEYS = 128
N_EXPERTS = N_KEYS * N_KEYS
PEER_HEADS = 8
PEER_DK = 128
PEER_TOPK = 16
NSA_WIDTH = N_HEADS * HEAD_DIM
KV_WIDTH = N_KV * HEAD_DIM
GATE_WIDTH = 3 * N_HEADS
EPS = 1e-6
NEG = -1e30
FORCE = 1e4

LANES = 128
VMEM_LIMIT = 56 * 1024 * 1024

N_STATE = N_SSM_GROUPS * SSM_STATE
QPAD = N_HEADS * LANES
F32 = jnp.float32
BF16 = jnp.bfloat16


def _cparams(*sem):
    return pltpu.CompilerParams(dimension_semantics=sem, vmem_limit_bytes=VMEM_LIMIT)


def _nt_dot(a, b):
    return lax.dot_general(a, b, (((1,), (1,)), ((), ())), preferred_element_type=F32)


def _rmsnorm(x, g):
    return x * lax.rsqrt(jnp.mean(x * x, axis=-1, keepdims=True) + EPS) * g


_PROJ_COLS = (("q", QPAD), ("kvsel", 4 * KV_WIDTH), ("kvwin", 2 * KV_WIDTH),
              ("gate", LANES), ("u", SSM_WIDTH), ("z", SSM_WIDTH))


def _proj_kernel(x_ref, g_ref, w_ref, q_ref, kvsel_ref, kvselb_ref, kvwin_ref, kvwinb_ref,
                 gate_ref, u_ref, z_ref):
    xn = _rmsnorm(x_ref[...], g_ref[...]).astype(BF16)
    off = 0
    outs = {}
    for name, width in _PROJ_COLS:
        outs[name] = jnp.dot(xn, w_ref[:, off:off + width], preferred_element_type=F32)
        off += width
    q_ref[...] = outs["q"].astype(q_ref.dtype)
    kvsel_ref[...] = outs["kvsel"]
    kvselb_ref[...] = outs["kvsel"][:, 2 * KV_WIDTH:].astype(BF16)
    kvwin_ref[...] = outs["kvwin"]
    kvwinb_ref[...] = outs["kvwin"].astype(BF16)
    gate_ref[...] = outs["gate"]
    u_ref[...] = outs["u"]
    z_ref[...] = outs["z"]


def _proj_weight(w_in):
    c0 = NSA_WIDTH
    wq = w_in[:, :c0].reshape(D_MODEL, N_KV, HPG, HEAD_DIM) * (HEAD_DIM ** -0.5)
    wq_pad = jnp.zeros((D_MODEL, N_KV, HPG, N_KV, HEAD_DIM), F32)
    for g in range(N_KV):
        wq_pad = wq_pad.at[:, g, :, g, :].set(wq[:, g])
    wq_pad = wq_pad.reshape(D_MODEL, QPAD)
    c1 = c0 + 4 * KV_WIDTH
    c2 = c1 + 2 * KV_WIDTH
    c3 = c2 + GATE_WIDTH
    wg = jnp.pad(w_in[:, c2:c3], ((0, 0), (0, LANES - GATE_WIDTH)))
    w = jnp.concatenate([wq_pad, w_in[:, c0:c1], w_in[:, c1:c2], wg,
                         w_in[:, c3:c3 + SSM_WIDTH], w_in[:, c3 + SSM_WIDTH:]], axis=1)
    return w.astype(BF16)


def _project(x2d, norm_g, w_all, tm, q_dtype):
    n = x2d.shape[0]
    wtot = w_all.shape[1]
    row = lambda w: pl.BlockSpec((tm, w), lambda i: (i, 0))
    shapes = [
        jax.ShapeDtypeStruct((n, QPAD), q_dtype),
        jax.ShapeDtypeStruct((n, 4 * KV_WIDTH), F32),
        jax.ShapeDtypeStruct((n, 2 * KV_WIDTH), BF16),
        jax.ShapeDtypeStruct((n, 2 * KV_WIDTH), F32),
        jax.ShapeDtypeStruct((n, 2 * KV_WIDTH), BF16),
        jax.ShapeDtypeStruct((n, LANES), F32),
        jax.ShapeDtypeStruct((n, SSM_WIDTH), F32),
        jax.ShapeDtypeStruct((n, SSM_WIDTH), F32),
    ]
    return pl.pallas_call(
        _proj_kernel,
        grid=(n // tm,),
        in_specs=[row(D_MODEL), pl.BlockSpec((1, D_MODEL), lambda i: (0, 0)),
                  pl.BlockSpec((D_MODEL, wtot), lambda i: (0, 0))],
        out_specs=[row(s.shape[1]) for s in shapes],
        out_shape=shapes,
        compiler_params=_cparams("parallel"),
        name="proj",
    )(x2d, norm_g.reshape(1, D_MODEL), w_all)


def _compress_kernel(tbl_ref, page_ref, w1_ref, pe_ref, w2_ref, out_ref, slabk_ref, slabv_ref, *, n_pages):
    del tbl_ref
    p = pl.program_id(1)
    seq = n_pages * PAGE_SIZE
    slabs = (slabk_ref, slabv_ref)

    @pl.when(p == 0)
    def _():
        for slab in slabs:
            slab[pl.ds(seq, CMP_LEN), :] = jnp.zeros((CMP_LEN, LANES), F32)

    rows = pl.ds(pl.multiple_of(p * PAGE_SIZE, PAGE_SIZE), PAGE_SIZE)
    for c, slab in enumerate(slabs):
        slab[rows, :] = page_ref[0, :, c * LANES:(c + 1) * LANES]

    @pl.when(p == n_pages - 1)
    def _():
        nb = seq // CMP_STRIDE
        for c, slab in enumerate(slabs):
            hid = jnp.zeros((nb, LANES), F32)
            for s in range(CMP_LEN):
                x = slab[pl.ds(s, nb, stride=CMP_STRIDE), :] + pe_ref[c, s:s + 1, :]
                hid = hid + jnp.dot(x.astype(BF16), w1_ref[c, s], preferred_element_type=F32)
            out = jnp.dot(jax.nn.gelu(hid).astype(BF16), w2_ref[c], preferred_element_type=F32)
            out_ref[0, :, c * LANES:(c + 1) * LANES] = out.astype(BF16)


def _compress_weights(w1, w2, pe):
    def bd(m):
        z = jnp.zeros((N_KV, HEAD_DIM, N_KV, HEAD_DIM), F32)
        for g in range(N_KV):
            z = z.at[g, :, g, :].set(m)
        return z.reshape(LANES, LANES)
    w1bd = jnp.stack([jnp.stack([bd(w1[c, s]) for s in range(CMP_LEN)]) for c in range(2)])
    w2bd = jnp.stack([bd(w2[c]) for c in range(2)])
    pe2 = jnp.concatenate([pe, pe], axis=-1)
    return w1bd.astype(BF16), pe2, w2bd.astype(BF16)


def _compress(pages, table, w1bd, pe2, w2bd):
    s, p = table.shape
    seq = p * PAGE_SIZE
    grid_spec = pltpu.PrefetchScalarGridSpec(
        num_scalar_prefetch=1,
        grid=(s, p),
        in_specs=[
            pl.BlockSpec((1, PAGE_SIZE, 2 * LANES), lambda b, j, t: (t[b, j], 0, 0)),
            pl.BlockSpec((2, CMP_LEN, LANES, LANES), lambda b, j, t: (0, 0, 0, 0)),
            pl.BlockSpec((2, CMP_LEN, LANES), lambda b, j, t: (0, 0, 0)),
            pl.BlockSpec((2, LANES, LANES), lambda b, j, t: (0, 0, 0)),
        ],
        out_specs=pl.BlockSpec((1, seq // CMP_STRIDE, 2 * LANES), lambda b, j, t: (b, 0, 0)),
        scratch_shapes=[pltpu.VMEM((seq + CMP_LEN, LANES), F32),
                        pltpu.VMEM((seq + CMP_LEN, LANES), F32)],
    )
    return pl.pallas_call(
        functools.partial(_compress_kernel, n_pages=p),
        grid_spec=grid_spec,
        out_shape=jax.ShapeDtypeStruct((s, seq // CMP_STRIDE, 2 * LANES), BF16),
        compiler_params=_cparams("parallel", "arbitrary"),
        name="compress",
    )(table, pages, w1bd, pe2, w2bd)


S5_LANE_BLOCK = 512


def _s5_kernel(u_ref, z_ref, h0_ref, bbd_ref, lam_ref, cbd_ref, d_ref, y_ref, hl_ref,
               bu_ref, hs_ref, h_ref):
    c = pl.program_id(1)
    steps = u_ref.shape[1]

    @pl.when(c == 0)
    def _():
        h_ref[...] = h0_ref[0]

    u = u_ref[0]
    bu_ref[...] = jnp.dot(u.astype(BF16), bbd_ref[...], preferred_element_type=F32)
    for blk in range(N_STATE // S5_LANE_BLOCK):
        re = slice(blk * S5_LANE_BLOCK, (blk + 1) * S5_LANE_BLOCK)
        im = slice(N_STATE + blk * S5_LANE_BLOCK, N_STATE + (blk + 1) * S5_LANE_BLOCK)
        lr = lam_ref[:, re]
        li = lam_ref[:, im]

        def step(t, carry, re=re, im=im, lr=lr, li=li):
            hr, hi = carry
            row = pl.ds(t, 1)
            nr = lr * hr - li * hi + bu_ref[row, re]
            ni = lr * hi + li * hr + bu_ref[row, im]
            hs_ref[row, re] = nr
            hs_ref[row, im] = ni
            return nr, ni

        hr, hi = lax.fori_loop(0, steps, step, (h_ref[:, re], h_ref[:, im]),
                               unroll=min(8, steps))
        h_ref[:, re] = hr
        h_ref[:, im] = hi
    y = jnp.dot(hs_ref[...].astype(BF16), cbd_ref[...], preferred_element_type=F32)
    y = y + d_ref[...] * u
    y_ref[0] = (jax.nn.gelu(y) * jax.nn.sigmoid(z_ref[0])).astype(y_ref.dtype)

    @pl.when(c == pl.num_programs(1) - 1)
    def _():
        hl_ref[0] = h_ref[...]


def _s5_weights(lam_re, lam_im, log_dt, b_re, b_im, c_re, c_im, d_skip):
    lam = lax.complex(lam_re.astype(F32), lam_im.astype(F32))
    dt = jnp.exp(log_dt.astype(F32))[:, None]
    lam_bar = jnp.exp(lam * dt)
    b_bar = ((lam_bar - 1.0) / lam)[..., None] * lax.complex(b_re.astype(F32), b_im.astype(F32))
    eye = jnp.eye(N_SSM_GROUPS, dtype=F32)
    def in_bd(b):
        return jnp.einsum("gph,gk->ghkp", b, eye).reshape(SSM_WIDTH, N_STATE)
    bbd = jnp.concatenate([in_bd(b_bar.real), in_bd(b_bar.imag)], axis=1)
    def out_bd(cm):
        return jnp.einsum("ghp,gk->gpkh", cm, eye).reshape(N_STATE, SSM_WIDTH)
    cbd = jnp.concatenate([out_bd(c_re.astype(F32)), -out_bd(c_im.astype(F32))], axis=0)
    lam_row = jnp.concatenate([lam_bar.real.reshape(1, N_STATE), lam_bar.imag.reshape(1, N_STATE)], axis=1)
    return bbd.astype(BF16), lam_row, cbd.astype(BF16), d_skip.astype(F32).reshape(1, SSM_WIDTH)


def _s5(u, z, h0, s5w, chunk, out_dtype):
    bbd, lam_row, cbd, d_row = s5w
    s, t, _ = u.shape
    const = lambda shape: pl.BlockSpec(shape, lambda b, c: (0,) * len(shape))
    return pl.pallas_call(
        _s5_kernel,
        grid=(s, t // chunk),
        in_specs=[
            pl.BlockSpec((1, chunk, SSM_WIDTH), lambda b, c: (b, c, 0)),
            pl.BlockSpec((1, chunk, SSM_WIDTH), lambda b, c: (b, c, 0)),
            pl.BlockSpec((1, 1, 2 * N_STATE), lambda b, c: (b, 0, 0)),
            const((SSM_WIDTH, 2 * N_STATE)), const((1, 2 * N_STATE)),
            const((2 * N_STATE, SSM_WIDTH)), const((1, SSM_WIDTH)),
        ],
        out_specs=[
            pl.BlockSpec((1, chunk, SSM_WIDTH), lambda b, c: (b, c, 0)),
            pl.BlockSpec((1, 1, 2 * N_STATE), lambda b, c: (b, 0, 0)),
        ],
        out_shape=[jax.ShapeDtypeStruct((s, t, SSM_WIDTH), out_dtype),
                   jax.ShapeDtypeStruct((s, 1, 2 * N_STATE), F32)],
        scratch_shapes=[pltpu.VMEM((chunk, 2 * N_STATE), F32),
                        pltpu.VMEM((chunk, 2 * N_STATE), F32),
                        pltpu.VMEM((1, 2 * N_STATE), F32)],
        compiler_params=_cparams("parallel", "arbitrary"),
        name="s5",
    )(u, z, h0, bbd, lam_row, cbd, d_row)


def _outproj_kernel(x_ref, a_ref, s_ref, wa_ref, ws_ref, y_ref):
    y = x_ref[...]
    y = y + jnp.dot(a_ref[...].astype(BF16), wa_ref[...], preferred_element_type=F32)
    y = y + jnp.dot(s_ref[...].astype(BF16), ws_ref[...], preferred_element_type=F32)
    y_ref[...] = y


def _outproj_weights(w_out):
    wa = w_out[:NSA_WIDTH].reshape(N_KV, HPG, HEAD_DIM, D_MODEL)
    wa_pad = jnp.zeros((N_KV, HPG, N_KV, HEAD_DIM, D_MODEL), F32)
    for g in range(N_KV):
        wa_pad = wa_pad.at[g, :, g].set(wa[g])
    return wa_pad.reshape(QPAD, D_MODEL).astype(BF16), w_out[NSA_WIDTH:].astype(BF16)


def _outproj(x2d, a_out, s_out, wa, ws, tm):
    n = x2d.shape[0]
    row = lambda w: pl.BlockSpec((tm, w), lambda i: (i, 0))
    return pl.pallas_call(
        _outproj_kernel,
        grid=(n // tm,),
        in_specs=[row(D_MODEL), row(QPAD), row(SSM_WIDTH),
                  pl.BlockSpec((QPAD, D_MODEL), lambda i: (0, 0)),
                  pl.BlockSpec((SSM_WIDTH, D_MODEL), lambda i: (0, 0))],
        out_specs=row(D_MODEL),
        out_shape=jax.ShapeDtypeStruct((n, D_MODEL), F32),
        compiler_params=_cparams("parallel"),
        name="outproj",
    )(x2d, a_out, s_out, wa, ws)


Q_TILE = 128
KEY_CHUNK = 512
N_BLK_PAD = 128
LOG2_SEL_BLOCK = int(math.log2(SEL_BLOCK))
WIN_KEYS = WINDOW + Q_TILE


def _cover_t(n_cmp_pad):
    n = jnp.arange(n_cmp_pad, dtype=jnp.int32)[None, :] * CMP_STRIDE
    s = jnp.arange(N_BLK_PAD, dtype=jnp.int32)[:, None] * SEL_BLOCK
    return ((n < s + SEL_BLOCK) & (n + CMP_LEN > s)).astype(BF16)


def _split3_nt(w, x):
    hi = x.astype(BF16)
    r1 = x - hi.astype(F32)
    mid = r1.astype(BF16)
    lo = (r1 - mid.astype(F32)).astype(BF16)
    return _nt_dot(w, hi) + _nt_dot(w, mid) + _nt_dot(w, lo)


def _topk_block_mask(imp_t, impt_ref, n_rows, n_keep):
    impt_ref[...] = imp_t
    s_idx = lax.broadcasted_iota(jnp.int32, imp_t.shape, 0)

    def body(sp, rank):
        row = impt_ref[pl.ds(sp, 1), :]
        beats = (row > imp_t) | ((row == imp_t) & (s_idx > sp))
        return rank + jnp.where(beats, 1.0, 0.0)

    rank = lax.fori_loop(0, n_rows, body, jnp.zeros(imp_t.shape, F32))
    return jnp.where(rank < n_keep, 0.0, NEG)


def _cmp_branch(q, s_ref, cmp_ref, ocmp_ref, t_pos, rows):
    ncp = cmp_ref.shape[1]
    kc = cmp_ref[0, :, 0:LANES]
    vc = cmp_ref[0, :, LANES:2 * LANES]
    s_ref[:, 0:ncp] = _nt_dot(q, kc)
    n_idx = lax.broadcasted_iota(jnp.int32, (rows, ncp), 1)
    ok = (n_idx * CMP_STRIDE + (CMP_LEN - 1) <= t_pos) & (n_idx < ncp - 1)
    imps = []
    for g in range(N_KV):
        psum = jnp.zeros((rows, ncp), F32)
        for h in range(HPG):
            hh = g * HPG + h
            s = jnp.where(ok, s_ref[hh * rows:(hh + 1) * rows, 0:ncp], NEG)
            m = jnp.max(s, axis=-1, keepdims=True)
            e = jnp.where(ok, jnp.exp(s - m), 0.0)
            l = jnp.sum(e, axis=-1, keepdims=True)
            p = e * jnp.where(l > 0.0, 1.0 / l, 0.0)
            ocmp_ref[hh] = jnp.dot(p.astype(BF16), vc, preferred_element_type=F32)
            psum = psum + p
        imps.append(psum)
    return imps


def _nsa_prompt_kernel(q_ref, gate_ref, kvs_ref, kvw_ref, cmp_ref, covt_ref, out_ref,
                       qs_ref, s_ref, impt_ref, msel_ref, m_ref, l_ref, acc_ref, ocmp_ref):
    i = pl.program_id(1)
    t0 = i * Q_TILE
    for h in range(N_HEADS):
        qs_ref[h * Q_TILE:(h + 1) * Q_TILE, :] = q_ref[0, :, h * LANES:(h + 1) * LANES]
    t_col = t0 + lax.broadcasted_iota(jnp.int32, (Q_TILE, 1), 0)

    psums = _cmp_branch(qs_ref[...], s_ref, cmp_ref, ocmp_ref, t_col, Q_TILE)
    s_idx = lax.broadcasted_iota(jnp.int32, (N_BLK_PAD, Q_TILE), 0)
    t_row = t0 + lax.broadcasted_iota(jnp.int32, (N_BLK_PAD, Q_TILE), 1)
    cur = jnp.right_shift(t_row, LOG2_SEL_BLOCK)
    forced = (s_idx == 0) | (s_idx == cur) | (s_idx == cur - 1)
    n_live = jnp.right_shift(t0, LOG2_SEL_BLOCK) + Q_TILE // SEL_BLOCK
    for g in range(N_KV):
        imp_t = _split3_nt(covt_ref[...], psums[g])
        imp_t = jnp.where(forced, FORCE, imp_t)
        imp_t = jnp.where(s_idx * SEL_BLOCK <= t_row, imp_t, NEG)
        mask_t = _topk_block_mask(imp_t, impt_ref, n_live, N_SEL)
        msel_ref[g] = mask_t.T.astype(BF16)

    m_ref[...] = jnp.full(m_ref.shape, NEG, F32)
    l_ref[...] = jnp.zeros(l_ref.shape, F32)
    acc_ref[...] = jnp.zeros(acc_ref.shape, F32)
    blk_row = lax.broadcasted_iota(jnp.int32, (N_BLK_PAD, KEY_CHUNK), 0)
    key_lane = lax.broadcasted_iota(jnp.int32, (Q_TILE, KEY_CHUNK), 1)

    def chunk(c, carry):
        k0 = pl.multiple_of(c * KEY_CHUNK, KEY_CHUNK)
        kch = kvs_ref[0, pl.ds(k0, KEY_CHUNK), 0:LANES]
        vch = kvs_ref[0, pl.ds(k0, KEY_CHUNK), LANES:2 * LANES]
        s_ref[:, 0:KEY_CHUNK] = _nt_dot(qs_ref[...], kch)
        expand = jnp.where(blk_row == jnp.right_shift(k0 + key_lane, LOG2_SEL_BLOCK), 1.0, 0.0).astype(BF16)
        causal = jnp.where(k0 + key_lane <= t_col, 0.0, NEG)
        for g in range(N_KV):
            bias = jnp.dot(msel_ref[g], expand, preferred_element_type=F32) + causal
            for h in range(HPG):
                hh = g * HPG + h
                s = s_ref[hh * Q_TILE:(hh + 1) * Q_TILE, 0:KEY_CHUNK] + bias
                m_old = m_ref[hh]
                m_new = jnp.maximum(m_old, jnp.max(s, axis=-1, keepdims=True))
                alpha = jnp.exp(m_old - m_new)
                p = jnp.exp(s - m_new[:, 0:1])
                l_ref[hh] = alpha * l_ref[hh] + jnp.sum(p, axis=-1, keepdims=True)
                acc_ref[hh] = alpha * acc_ref[hh] + jnp.dot(p.astype(BF16), vch, preferred_element_type=F32)
                m_ref[hh] = m_new
        return carry

    lax.fori_loop(0, t0 // KEY_CHUNK + 1, chunk, 0)

    start = pl.multiple_of(jnp.maximum(t0 - WINDOW, 0), Q_TILE)
    kw = kvw_ref[0, pl.ds(start, WIN_KEYS), 0:LANES]
    vw = kvw_ref[0, pl.ds(start, WIN_KEYS), LANES:2 * LANES]
    s_ref[:, 0:WIN_KEYS] = _nt_dot(qs_ref[...], kw)
    dist = t_col - (start + lax.broadcasted_iota(jnp.int32, (Q_TILE, WIN_KEYS), 1))
    bias_w = jnp.where((dist >= 0) & (dist <= WINDOW), 0.0, NEG)

    gates = jax.nn.sigmoid(gate_ref[0])
    for hh in range(N_HEADS):
        s = s_ref[hh * Q_TILE:(hh + 1) * Q_TILE, 0:WIN_KEYS] + bias_w
        p = jnp.exp(s - jnp.max(s, axis=-1, keepdims=True))
        o_win = jnp.dot(p.astype(BF16), vw, preferred_element_type=F32) / jnp.sum(p, axis=-1, keepdims=True)
        o_sel = acc_ref[hh] / l_ref[hh]
        o = (gates[:, 3 * hh:3 * hh + 1] * ocmp_ref[hh] + gates[:, 3 * hh + 1:3 * hh + 2] * o_sel
             + gates[:, 3 * hh + 2:3 * hh + 3] * o_win)
        out_ref[0, :, hh * LANES:(hh + 1) * LANES] = o.astype(out_ref.dtype)


def _nsa_prompt(q, gates, kvs, kvw, cmp):
    b, t, _ = q.shape
    ncp = cmp.shape[1]
    tile = lambda w: pl.BlockSpec((1, Q_TILE, w), lambda bb, i: (bb, i, 0))
    whole = lambda r, w: pl.BlockSpec((1, r, w), lambda bb, i: (bb, 0, 0))
    rows = N_HEADS * Q_TILE
    return pl.pallas_call(
        _nsa_prompt_kernel,
        grid=(b, t // Q_TILE),
        in_specs=[tile(QPAD), tile(LANES), whole(t, 2 * LANES), whole(t, 2 * LANES),
                  whole(ncp, 2 * LANES), pl.BlockSpec((N_BLK_PAD, ncp), lambda bb, i: (0, 0))],
        out_specs=tile(QPAD),
        out_shape=jax.ShapeDtypeStruct((b, t, QPAD), BF16),
        scratch_shapes=[
            pltpu.VMEM((rows, LANES), BF16),
            pltpu.VMEM((rows, max(WIN_KEYS, ncp)), F32),
            pltpu.VMEM((N_BLK_PAD, Q_TILE), F32),
            pltpu.VMEM((N_KV, Q_TILE, N_BLK_PAD), BF16),
            pltpu.VMEM((N_HEADS, Q_TILE, LANES), F32),
            pltpu.VMEM((N_HEADS, Q_TILE, LANES), F32),
            pltpu.VMEM((N_HEADS, Q_TILE, LANES), F32),
            pltpu.VMEM((N_HEADS, Q_TILE, LANES), F32),
        ],
        compiler_params=_cparams("parallel", "arbitrary"),
        name="nsa_prompt",
    )(q, gates, kvs, kvw, cmp, _cover_t(ncp))


def _flash_update(s, v, m_ref, l_ref, acc_ref):
    m_old = m_ref[...]
    m_new = jnp.maximum(m_old, jnp.max(s, axis=-1, keepdims=True))
    alpha = jnp.exp(m_old - m_new)
    p = jnp.exp(s - m_new[:, 0:1])
    l_ref[...] = alpha * l_ref[...] + jnp.sum(p, axis=-1, keepdims=True)
    acc_ref[...] = alpha * acc_ref[...] + jnp.dot(p.astype(BF16), v, preferred_element_type=F32)
    m_ref[...] = m_new


def _nsa_sample_kernel(tbl_ref, q_ref, gate_ref, page_ref, cmp_ref, win_ref, newkv_ref, newwin_ref,
                       covt_ref, out_ref, qs_ref, s_ref, impt_ref, msel_ref, m_ref, l_ref, acc_ref,
                       ocmp_ref, *, n_pages, tq):
    del tbl_ref
    p = pl.program_id(1)
    past_len = n_pages * PAGE_SIZE
    rows = N_HEADS * tq
    i_col = lax.broadcasted_iota(jnp.int32, (rows, 1), 0) & (tq - 1)
    blk_row = lax.broadcasted_iota(jnp.int32, (N_BLK_PAD, PAGE_SIZE), 0)
    key_lane = lax.broadcasted_iota(jnp.int32, (N_BLK_PAD, PAGE_SIZE), 1)

    @pl.when(p == 0)
    def _():
        for h in range(N_HEADS):
            qs_ref[h * tq:(h + 1) * tq, :] = q_ref[0, :, h * LANES:(h + 1) * LANES]
        t_col = past_len + lax.broadcasted_iota(jnp.int32, (tq, 1), 0)
        psums = _cmp_branch(qs_ref[...].astype(BF16), s_ref, cmp_ref, ocmp_ref, t_col, tq)
        s_idx = lax.broadcasted_iota(jnp.int32, (N_BLK_PAD, LANES), 0)
        t_row = past_len + lax.broadcasted_iota(jnp.int32, (N_BLK_PAD, LANES), 1)
        cur = jnp.right_shift(t_row, LOG2_SEL_BLOCK)
        forced = (s_idx == 0) | (s_idx == cur) | (s_idx == cur - 1)
        ncp = cmp_ref.shape[1]
        for g in range(N_KV):
            psum = jnp.concatenate([psums[g], jnp.zeros((LANES - tq, ncp), F32)], axis=0)
            imp_t = jnp.where(forced, FORCE, _split3_nt(covt_ref[...], psum))
            mask_t = _topk_block_mask(imp_t, impt_ref, N_BLK_PAD, N_SEL - 1)
            msel_ref[g] = mask_t.T.astype(BF16)
        m_ref[...] = jnp.full(m_ref.shape, NEG, F32)
        l_ref[...] = jnp.zeros(l_ref.shape, F32)
        acc_ref[...] = jnp.zeros(acc_ref.shape, F32)

    q = qs_ref[...].astype(BF16)
    kp = page_ref[0, :, 0:LANES].astype(BF16)
    vp = page_ref[0, :, LANES:2 * LANES].astype(BF16)
    expand = jnp.where(blk_row == jnp.right_shift(p * PAGE_SIZE + key_lane, LOG2_SEL_BLOCK), 1.0, 0.0).astype(BF16)
    bias_g = [jnp.dot(msel_ref[g], expand, preferred_element_type=F32)[0:tq, :] for g in range(N_KV)]
    bias = jnp.concatenate([bias_g[hh // HPG] for hh in range(N_HEADS)], axis=0)
    _flash_update(_nt_dot(q, kp) + bias, vp, m_ref, l_ref, acc_ref)

    @pl.when(p == n_pages - 1)
    def _():
        j_lane = lax.broadcasted_iota(jnp.int32, (rows, PAGE_SIZE), 1)
        new_bias = jnp.where(j_lane <= i_col, 0.0, NEG)
        _flash_update(_nt_dot(q, newkv_ref[0, :, 0:LANES]) + new_bias, newkv_ref[0, :, LANES:2 * LANES],
                      m_ref, l_ref, acc_ref)
        wb = win_ref.shape[1]
        kwc = win_ref[0, :, 0:LANES].astype(BF16)
        vwc = win_ref[0, :, LANES:2 * LANES].astype(BF16)
        dist = wb + i_col - lax.broadcasted_iota(jnp.int32, (rows, wb), 1)
        s1 = _nt_dot(q, kwc) + jnp.where((dist >= 0) & (dist <= WINDOW), 0.0, NEG)
        s2 = _nt_dot(q, newwin_ref[0, :, 0:LANES]) + new_bias
        m = jnp.maximum(jnp.max(s1, axis=-1, keepdims=True), jnp.max(s2, axis=-1, keepdims=True))
        p1 = jnp.exp(s1 - m)
        p2 = jnp.exp(s2 - m)
        l = jnp.sum(p1, axis=-1, keepdims=True) + jnp.sum(p2, axis=-1, keepdims=True)
        o_win = (jnp.dot(p1.astype(BF16), vwc, preferred_element_type=F32)
                 + jnp.dot(p2.astype(BF16), newwin_ref[0, :, LANES:2 * LANES], preferred_element_type=F32)) / l
        o_sel = acc_ref[...] / l_ref[...]
        gates = jax.nn.sigmoid(gate_ref[0])
        for hh in range(N_HEADS):
            r = slice(hh * tq, (hh + 1) * tq)
            o = (gates[:, 3 * hh:3 * hh + 1] * ocmp_ref[hh] + gates[:, 3 * hh + 1:3 * hh + 2] * o_sel[r]
                 + gates[:, 3 * hh + 2:3 * hh + 3] * o_win[r])
            out_ref[0, :, hh * LANES:(hh + 1) * LANES] = o


def _nsa_sample(q, gates, pages, table, cmp, win, newkv, newwin):
    s, tq, _ = q.shape
    n_pages = table.shape[1]
    ncp = cmp.shape[1]
    wb = win.shape[1]
    assert (n_pages * PAGE_SIZE) // SEL_BLOCK == N_BLK_PAD and tq <= SEL_BLOCK and tq & (tq - 1) == 0
    rows = N_HEADS * tq
    per_seq = lambda r, w: pl.BlockSpec((1, r, w), lambda b, j, t: (b, 0, 0))
    grid_spec = pltpu.PrefetchScalarGridSpec(
        num_scalar_prefetch=1,
        grid=(s, n_pages),
        in_specs=[per_seq(tq, QPAD), per_seq(tq, LANES),
                  pl.BlockSpec((1, PAGE_SIZE, 2 * LANES), lambda b, j, t: (t[b, j], 0, 1)),
                  per_seq(ncp, 2 * LANES), per_seq(wb, 2 * LANES),
                  per_seq(PAGE_SIZE, 2 * LANES), per_seq(PAGE_SIZE, 2 * LANES),
                  pl.BlockSpec((N_BLK_PAD, ncp), lambda b, j, t: (0, 0))],
        out_specs=per_seq(tq, QPAD),
        scratch_shapes=[
            pltpu.VMEM((rows, LANES), F32),
            pltpu.VMEM((rows, ncp), F32),
            pltpu.VMEM((N_BLK_PAD, LANES), F32),
            pltpu.VMEM((N_KV, LANES, N_BLK_PAD), BF16),
            pltpu.VMEM((rows, LANES), F32),
            pltpu.VMEM((rows, LANES), F32),
            pltpu.VMEM((rows, LANES), F32),
            pltpu.VMEM((N_HEADS, tq, LANES), F32),
        ],
    )
    return pl.pallas_call(
        functools.partial(_nsa_sample_kernel, n_pages=n_pages, tq=tq),
        grid_spec=grid_spec,
        out_shape=jax.ShapeDtypeStruct((s, tq, QPAD), F32),
        compiler_params=_cparams("parallel", "arbitrary"),
        name="nsa_sample",
    )(table, q, gates, pages, cmp, win, newkv, newwin, _cover_t(ncp))


LOWEST = -3.0e38
PEER_HALF = PEER_HEADS * PEER_DK // 2
_CAND_COUNTS = tuple(PEER_TOPK // (a + 1) for a in range(PEER_TOPK))
N_CAND = sum(_CAND_COUNTS)
N_CAND_PAD = -(-N_CAND // 8) * 8


def _top_rows(s, out_ref, n):
    kidx = lax.broadcasted_iota(jnp.int32, s.shape, 0).astype(F32)
    rem = s
    for a in range(n):
        mx = jnp.max(rem, axis=0, keepdims=True)
        first = jnp.min(jnp.where(rem == mx, kidx, float(s.shape[0])), axis=0, keepdims=True)
        out_ref[a:a + 1, :] = mx
        rem = jnp.where(kidx == first, LOWEST, rem)


def _peer_route_kernel(y_ref, g_ref, wq_ref, k1_ref, k2_ref, xn_ref, s1_ref, tau_ref, e1_ref,
                       s2_ref, e2_ref, v1_ref, v2_ref, cand_ref):
    xn = _rmsnorm(y_ref[...], g_ref[...]).astype(BF16)
    xn_ref[...] = xn
    q = jnp.dot(xn, wq_ref[...], preferred_element_type=F32).astype(BF16)
    s1_ref[...] = _nt_dot(k1_ref[...], q[:, :PEER_HALF])
    s2_ref[...] = _nt_dot(k2_ref[...], q[:, PEER_HALF:])
    t = y_ref.shape[0]
    cand_ref[N_CAND:N_CAND_PAD, :] = jnp.full((N_CAND_PAD - N_CAND, t), LOWEST, F32)
    for h in range(PEER_HEADS):
        rows = slice(h * N_KEYS, (h + 1) * N_KEYS)
        s1 = s1_ref[rows, :]
        s2 = s2_ref[rows, :]
        _top_rows(s1, v1_ref, PEER_TOPK)
        _top_rows(s2, v2_ref, PEER_TOPK)
        v1 = v1_ref[...]
        v2 = v2_ref[...]
        off = 0
        for a, nb in enumerate(_CAND_COUNTS):
            cand_ref[off:off + nb, :] = v1[a:a + 1, :] + v2[0:nb, :]
            off += nb
        cand = cand_ref[...]
        rem = cand
        tau = jnp.zeros((1, t), F32)
        done = jnp.zeros((1, t), F32)
        for _ in range(PEER_TOPK):
            mx = jnp.max(rem, axis=0, keepdims=True)
            cnt = jnp.sum(jnp.where(cand >= mx, 1.0, 0.0), axis=0, keepdims=True)
            newly = (cnt >= float(PEER_TOPK)) & (done < 0.5)
            tau = jnp.where(newly, mx, tau)
            done = jnp.where(newly, 1.0, done)
            rem = jnp.where(rem >= mx, LOWEST, rem)
        top = v1[0:1, :] + v2[0:1, :]
        z = jnp.sum(jnp.where(cand >= tau, jnp.exp(cand - top), 0.0), axis=0, keepdims=True)
        tau_ref[h:h + 1, :] = tau
        e1_ref[rows, :] = jnp.exp(s1 - v1[0:1, :]) / z
        e2_ref[rows, :] = jnp.exp(s2 - v2[0:1, :])


def _peer_weights(w_q, sub_k1, sub_k2):
    wq = w_q.reshape(D_MODEL, PEER_HEADS, 2, PEER_DK // 2).transpose(0, 2, 1, 3).reshape(D_MODEL, 2 * PEER_HALF)
    eye = jnp.eye(PEER_HEADS, dtype=F32)
    bd = lambda k: jnp.einsum("hkd,hj->hkjd", k, eye).reshape(PEER_HEADS * N_KEYS, PEER_HALF)
    return wq.astype(BF16), bd(sub_k1).astype(BF16), bd(sub_k2).astype(BF16)


def _peer_route(y2d, norm_g, wq, k1bd, k2bd, tm):
    n = y2d.shape[0]
    hk = PEER_HEADS * N_KEYS
    const = lambda a: pl.BlockSpec(a.shape, lambda i: (0, 0))
    col = lambda r: pl.BlockSpec((r, tm), lambda i: (0, i))
    tshape = jax.ShapeDtypeStruct((hk, n), F32)
    return pl.pallas_call(
        _peer_route_kernel,
        grid=(n // tm,),
        in_specs=[pl.BlockSpec((tm, D_MODEL), lambda i: (i, 0)),
                  pl.BlockSpec((1, D_MODEL), lambda i: (0, 0)), const(wq), const(k1bd), const(k2bd)],
        out_specs=[pl.BlockSpec((tm, D_MODEL), lambda i: (i, 0)), col(hk), col(PEER_HEADS),
                   col(hk), col(hk), col(hk)],
        out_shape=[jax.ShapeDtypeStruct((n, D_MODEL), BF16), tshape,
                   jax.ShapeDtypeStruct((PEER_HEADS, n), F32), tshape, tshape, tshape],
        scratch_shapes=[pltpu.VMEM((PEER_TOPK, tm), F32), pltpu.VMEM((PEER_TOPK, tm), F32),
                        pltpu.VMEM((N_CAND_PAD, tm), F32)],
        compiler_params=_cparams("parallel"),
        name="peer_route",
    )(y2d, norm_g.reshape(1, D_MODEL), wq, k1bd, k2bd)


EXPERT_BLOCK = 1024
I1_PER_BLOCK = EXPERT_BLOCK // N_KEYS


def _peer_dense_kernel(y_ref, xn_ref, u_ref, vt_ref, s1_ref, tau_ref, e1_ref, s2_ref, e2_ref,
                       gf_ref, out_ref, acc_ref, ga_ref):
    j = pl.program_id(1)
    t = xn_ref.shape[0]

    @pl.when(j == 0)
    def _():
        acc_ref[...] = jnp.zeros(acc_ref.shape, F32)

    act = jax.nn.gelu(_nt_dot(u_ref[...], xn_ref[...]))
    for tc in range(t // LANES):
        cols = slice(tc * LANES, (tc + 1) * LANES)
        for ii in range(I1_PER_BLOCK):
            gate = jnp.zeros((N_KEYS, LANES), F32)
            for h in range(PEER_HEADS):
                grp = pl.ds(pl.multiple_of(h * N_KEYS + j * I1_PER_BLOCK, I1_PER_BLOCK), I1_PER_BLOCK)
                krows = slice(h * N_KEYS, (h + 1) * N_KEYS)
                s1 = s1_ref[grp, cols][ii:ii + 1, :]
                e1 = e1_ref[grp, cols][ii:ii + 1, :]
                picked = (s1 + s2_ref[krows, cols]) >= tau_ref[h:h + 1, cols]
                gate = gate + jnp.where(picked, e1 * e2_ref[krows, cols], 0.0)
            erows = slice(ii * N_KEYS, (ii + 1) * N_KEYS)
            ga_ref[erows, cols] = (gate * act[erows, cols]).astype(BF16)
    acc_ref[...] += jnp.dot(vt_ref[...], ga_ref[...], preferred_element_type=F32)

    @pl.when(j == pl.num_programs(1) - 1)
    def _():
        y = y_ref[...] + acc_ref[...].T
        out_ref[...] = _rmsnorm(y, gf_ref[...])


def _peer_dense(y2d, xn, u_bf, vt_bf, s1, tau, e1, s2, e2, norm_f, tm):
    n = y2d.shape[0]
    hk = PEER_HEADS * N_KEYS
    tok = lambda w: pl.BlockSpec((tm, w), lambda i, j: (i, 0))
    col = lambda r: pl.BlockSpec((r, tm), lambda i, j: (0, i))
    return pl.pallas_call(
        _peer_dense_kernel,
        grid=(n // tm, N_EXPERTS // EXPERT_BLOCK),
        in_specs=[tok(D_MODEL), tok(D_MODEL),
                  pl.BlockSpec((EXPERT_BLOCK, D_MODEL), lambda i, j: (j, 0)),
                  pl.BlockSpec((D_MODEL, EXPERT_BLOCK), lambda i, j: (0, j)),
                  col(hk), col(PEER_HEADS), col(hk), col(hk), col(hk),
                  pl.BlockSpec((1, D_MODEL), lambda i, j: (0, 0))],
        out_specs=tok(D_MODEL),
        out_shape=jax.ShapeDtypeStruct((n, D_MODEL), F32),
        scratch_shapes=[pltpu.VMEM((D_MODEL, tm), F32), pltpu.VMEM((EXPERT_BLOCK, tm), BF16)],
        compiler_params=_cparams("parallel", "arbitrary"),
        name="peer_dense",
    )(y2d, xn, u_bf, vt_bf, s1, tau, e1, s2, e2, norm_f.reshape(1, D_MODEL))


PROJ_TILE = 512
PEER_TILE = 256
S5_CHUNK = 256


def _state_to_rows(st):
    s = st.shape[0]
    return jnp.concatenate([st[..., 0].reshape(s, 1, N_STATE), st[..., 1].reshape(s, 1, N_STATE)], axis=-1)


def _rows_to_state(h):
    s = h.shape[0]
    shape = (s, N_SSM_GROUPS, SSM_STATE)
    return jnp.stack([h[:, 0, :N_STATE].reshape(shape), h[:, 0, N_STATE:].reshape(shape)], axis=-1)


def _peer_block(y2d, norm_g, peer_w, norm_f, tm):
    wq, k1bd, k2bd, u_bf, vt_bf = peer_w
    xn, s1, tau, e1, s2, e2 = _peer_route(y2d, norm_g, wq, k1bd, k2bd, tm)
    return _peer_dense(y2d, xn, u_bf, vt_bf, s1, tau, e1, s2, e2, norm_f, tm)


def kernel(x_prompt, x_sample, cache_kv, cache_win, state_ssm, page_table, norm_mix, w_in, w_cmp1, w_cmp2, pe_cmp, lam_re, lam_im, log_dt, b_re, b_im, c_re, c_im, d_skip, w_out, norm_ffn, w_q_peer, sub_k1, sub_k2, u_tab, v_tab, norm_final):
    b, t, d = x_prompt.shape
    db, ts, _ = x_sample.shape
    assert w_in.shape[0] == DEPTH == 1 and d == D_MODEL
    l = 0
    n_pool = cache_kv.shape[1]
    wb = cache_win.shape[2]

    w_all = _proj_weight(w_in[l])
    cmp_w = _compress_weights(w_cmp1[l], w_cmp2[l], pe_cmp[l])
    s5_w = _s5_weights(lam_re[l], lam_im[l], log_dt[l], b_re[l], b_im[l], c_re[l], c_im[l], d_skip[l])
    wa, ws = _outproj_weights(w_out[l])
    peer_w = _peer_weights(w_q_peer[l], sub_k1[l], sub_k2[l]) + (
        u_tab[l].astype(BF16), v_tab[l].T.astype(BF16))

    xp = x_prompt.reshape(b * t, d)
    q, kvsel, kvsel_b, kvwin, kvwin_b, gate, u, z = _project(xp, norm_mix[l], w_all, PROJ_TILE, BF16)
    pages_p = kvsel.reshape(b * t // PAGE_SIZE, PAGE_SIZE, 4 * KV_WIDTH)
    table_p = jnp.arange(b * t // PAGE_SIZE, dtype=jnp.int32).reshape(b, t // PAGE_SIZE)
    cmp_p = _compress(pages_p, table_p, *cmp_w)
    a_p = _nsa_prompt(q.reshape(b, t, QPAD), gate.reshape(b, t, LANES), kvsel_b.reshape(b, t, 2 * KV_WIDTH),
                      kvwin_b.reshape(b, t, 2 * KV_WIDTH), cmp_p)
    s_p, h_p = _s5(u.reshape(b, t, SSM_WIDTH), z.reshape(b, t, SSM_WIDTH),
                   jnp.zeros((b, 1, 2 * N_STATE), F32), s5_w, S5_CHUNK, BF16)
    y1p = _outproj(xp, a_p.reshape(b * t, QPAD), s_p.reshape(b * t, SSM_WIDTH), wa, ws, PROJ_TILE)
    y_prompt = _peer_block(y1p, norm_ffn[l], peer_w, norm_final, PEER_TILE).reshape(b, t, d)
    kv_prompt = kvsel.reshape(b, t, 4, N_KV, HEAD_DIM)
    win_prompt = kvwin.reshape(b, t, 2, N_KV, HEAD_DIM)[:, t - min(WINDOW, t):]
    ssm_prompt = _rows_to_state(h_p)

    xs = x_sample.reshape(db * ts, d)
    qs, kvsel_s, kvsel_sb, kvwin_s, kvwin_sb, gate_s, u_s, z_s = _project(
        xs, norm_mix[l], w_all, db * ts, F32)
    pages_s = cache_kv[l].reshape(n_pool, PAGE_SIZE, 4 * KV_WIDTH)
    cmp_s = _compress(pages_s, page_table, *cmp_w)
    pad_rows = lambda a: jnp.pad(a.reshape(db, ts, 2 * KV_WIDTH), ((0, 0), (0, PAGE_SIZE - ts), (0, 0)))
    a_s = _nsa_sample(qs.reshape(db, ts, QPAD), gate_s.reshape(db, ts, LANES), pages_s, page_table, cmp_s,
                      cache_win[l].reshape(db, wb, 2 * KV_WIDTH), pad_rows(kvsel_sb), pad_rows(kvwin_sb))
    s_s, h_s = _s5(u_s.reshape(db, ts, SSM_WIDTH), z_s.reshape(db, ts, SSM_WIDTH),
                   _state_to_rows(state_ssm[l].astype(F32)), s5_w, ts, F32)
    y1s = _outproj(xs, a_s.reshape(db * ts, QPAD), s_s.reshape(db * ts, SSM_WIDTH), wa, ws, db * ts)
    y_sample = _peer_block(y1s, norm_ffn[l], peer_w, norm_final, PEER_TILE).reshape(db, ts, d)
    kv_sample = kvsel_s.reshape(db, ts, 4, N_KV, HEAD_DIM)
    win_new = kvwin_s.reshape(db, ts, 2, N_KV, HEAD_DIM).astype(cache_win.dtype)
    win_sample = jnp.concatenate([cache_win[l], win_new], axis=1)[:, ts:]
    ssm_sample = _rows_to_state(h_s)

    return (y_prompt, y_sample, kv_prompt[None], kv_sample[None], win_prompt[None], win_sample[None],
            ssm_prompt[None], ssm_sample[None])
```

```python
import functools
import math

import jax
import jax.numpy as jnp
from jax import lax
from jax.experimental import pallas as pl
from jax.experimental.pallas import tpu as pltpu

D_MODEL = 1024
DEPTH = 1
PAGE_SIZE = 128
N_HEADS = 8
N_KV = 2
HEAD_DIM = 64
HPG = N_HEADS // N_KV
CMP_LEN = 32
CMP_STRIDE = 16
SEL_BLOCK = 64
N_SEL = 16
WINDOW = 512
SSM_WIDTH = 512
SSM_GROUP = 16
N_SSM_GROUPS = SSM_WIDTH // SSM_GROUP
SSM_STATE = 64
N_KEYS = 128
N_EXPERTS = N_KEYS * N_KEYS
PEER_HEADS = 8
PEER_DK = 128
PEER_TOPK = 16
NSA_WIDTH = N_HEADS * HEAD_DIM
KV_WIDTH = N_KV * HEAD_DIM
GATE_WIDTH = 3 * N_HEADS
EPS = 1e-6
NEG = -1e30
FORCE = 1e4

LANES = 128
VMEM_LIMIT = 56 * 1024 * 1024

N_STATE = N_SSM_GROUPS * SSM_STATE
QPAD = N_HEADS * LANES
F32 = jnp.float32
BF16 = jnp.bfloat16

assert KV_WIDTH == LANES and PAGE_SIZE == LANES


def _cparams(*sem):
    return pltpu.CompilerParams(dimension_semantics=sem, vmem_limit_bytes=VMEM_LIMIT)


def _nt_dot(a, b):
    return lax.dot_general(a, b, (((1,), (1,)), ((), ())), preferred_element_type=F32)


def _dot(a, b):
    return jnp.dot(a, b, preferred_element_type=F32)


def _rmsnorm(x, g):
    return x * lax.rsqrt(jnp.mean(x * x, axis=-1, keepdims=True) + EPS) * g


_PROJ_COLS = (("q", QPAD), ("gate", LANES), ("u", SSM_WIDTH), ("z", SSM_WIDTH))
KVT_ROWS = 6 * KV_WIDTH


def _proj_kernel(x_ref, g_ref, w_ref, wt_ref, q_ref, gate_ref, u_ref, z_ref,
                 kvsel_ref, ksvs_ref, kvwin_ref, kwvw_ref):
    xn = _rmsnorm(x_ref[...], g_ref[...]).astype(BF16)
    off = 0
    for (name, width), ref in zip(_PROJ_COLS, (q_ref, gate_ref, u_ref, z_ref)):
        ref[...] = _dot(xn, w_ref[:, off:off + width]).astype(ref.dtype)
        off += width
    kvt = _nt_dot(wt_ref[...], xn)
    kvsel_ref[0] = kvt[0:4 * KV_WIDTH]
    ksvs_ref[0] = kvt[2 * KV_WIDTH:4 * KV_WIDTH].astype(BF16)
    kvwin_ref[0] = kvt[4 * KV_WIDTH:]
    kwvw_ref[0] = kvt[4 * KV_WIDTH:].astype(BF16)


def _proj_weights(w_in):
    c0 = NSA_WIDTH
    c1 = c0 + 6 * KV_WIDTH
    c2 = c1 + GATE_WIDTH
    wq = w_in[:, :c0].reshape(D_MODEL, N_KV, HPG, 1, HEAD_DIM) * (HEAD_DIM ** -0.5)
    slot = jnp.eye(N_KV, dtype=F32).reshape(1, N_KV, 1, N_KV, 1)
    wq_pad = (wq * slot).reshape(D_MODEL, QPAD)
    wg = jnp.pad(w_in[:, c1:c2], ((0, 0), (0, LANES - GATE_WIDTH)))
    w = jnp.concatenate([wq_pad, wg, w_in[:, c2:]], axis=1)
    return w.astype(BF16), w_in[:, c0:c1].T.astype(BF16)


def _project(x3d, norm_g, w_tok, w_feat, tm, q_dtype):
    s, t, _ = x3d.shape
    nt = t // tm
    n = s * t
    row = lambda w: pl.BlockSpec((tm, w), lambda b, i: (b * nt + i, 0))
    feat = lambda r: pl.BlockSpec((1, r, tm), lambda b, i: (b, 0, i))
    const = lambda a: pl.BlockSpec(a.shape, lambda b, i: (0, 0))
    shapes = [
        jax.ShapeDtypeStruct((n, QPAD), q_dtype),
        jax.ShapeDtypeStruct((n, LANES), F32),
        jax.ShapeDtypeStruct((n, SSM_WIDTH), F32),
        jax.ShapeDtypeStruct((n, SSM_WIDTH), F32),
        jax.ShapeDtypeStruct((s, 4 * KV_WIDTH, t), F32),
        jax.ShapeDtypeStruct((s, 2 * KV_WIDTH, t), BF16),
        jax.ShapeDtypeStruct((s, 2 * KV_WIDTH, t), F32),
        jax.ShapeDtypeStruct((s, 2 * KV_WIDTH, t), BF16),
    ]
    g2 = norm_g.reshape(1, D_MODEL)
    return pl.pallas_call(
        _proj_kernel,
        grid=(s, nt),
        in_specs=[row(D_MODEL), const(g2), const(w_tok), const(w_feat)],
        out_specs=[row(sh.shape[1]) for sh in shapes[:4]] + [feat(sh.shape[1]) for sh in shapes[4:]],
        out_shape=shapes,
        compiler_params=_cparams("parallel", "parallel"),
        name="proj",
    )(x3d.reshape(n, D_MODEL), g2, w_tok, w_feat)


CMP_PAGES = 8


def _compress_kernel(tbl_ref, *refs, n_pages):
    del tbl_ref
    page_refs = refs[:CMP_PAGES]
    w1_ref, pe_ref, w2_ref, out_ref, slabk_ref, slabv_ref = refs[CMP_PAGES:]
    j = pl.program_id(1)
    seq = n_pages * PAGE_SIZE
    slabs = (slabk_ref, slabv_ref)

    for k, page in enumerate(page_refs):
        rows = pl.ds(pl.multiple_of((j * CMP_PAGES + k) * PAGE_SIZE, PAGE_SIZE), PAGE_SIZE)
        for c, slab in enumerate(slabs):
            slab[rows, :] = page[c].T

    @pl.when(j == n_pages // CMP_PAGES - 1)
    def _():
        nb = seq // CMP_STRIDE
        for c, slab in enumerate(slabs):
            head = jnp.zeros((nb, LANES), F32)
            tail = jnp.zeros((nb, LANES), F32)
            for s in range(CMP_STRIDE):
                x = slab[pl.ds(s, nb, stride=CMP_STRIDE), :]
                head = head + _dot((x + pe_ref[c, s:s + 1, :]).astype(BF16), w1_ref[c, s])
                s2 = CMP_STRIDE + s
                tail = tail + _dot((x + pe_ref[c, s2:s2 + 1, :]).astype(BF16), w1_ref[c, s2])
            hid = head + pltpu.roll(tail, nb - 1, 0)
            out = _dot(jax.nn.gelu(hid).astype(BF16), w2_ref[c])
            out_ref[0, :, c * LANES:(c + 1) * LANES] = out.astype(BF16)


def _compress_weights(w1, w2, pe):
    eye = jnp.eye(N_KV, dtype=F32)
    bd = lambda m: jnp.einsum("...de,gk->...gdke", m, eye).reshape(m.shape[:-2] + (LANES, LANES))
    pe2 = jnp.concatenate([pe, pe], axis=-1)
    return bd(w1).astype(BF16), pe2, bd(w2).astype(BF16)


def _compress(pages, page_spec, table, w1bd, pe2, w2bd):
    s, p = table.shape
    assert CMP_LEN == 2 * CMP_STRIDE and p % CMP_PAGES == 0
    seq = p * PAGE_SIZE
    const = lambda a: pl.BlockSpec(a.shape, lambda b, j, t: (0,) * a.ndim)
    grid_spec = pltpu.PrefetchScalarGridSpec(
        num_scalar_prefetch=1,
        grid=(s, p // CMP_PAGES),
        in_specs=[page_spec(k) for k in range(CMP_PAGES)] + [const(w1bd), const(pe2), const(w2bd)],
        out_specs=pl.BlockSpec((1, seq // CMP_STRIDE, 2 * LANES), lambda b, j, t: (b, 0, 0)),
        scratch_shapes=[pltpu.VMEM((seq, LANES), F32), pltpu.VMEM((seq, LANES), F32)],
    )
    return pl.pallas_call(
        functools.partial(_compress_kernel, n_pages=p),
        grid_spec=grid_spec,
        out_shape=jax.ShapeDtypeStruct((s, seq // CMP_STRIDE, 2 * LANES), BF16),
        compiler_params=_cparams("parallel", "arbitrary"),
        name="compress",
    )(table, *([pages] * CMP_PAGES), w1bd, pe2, w2bd)


Q_TILE = 128
KEY_CHUNK = 512
N_BLK_PAD = 128
LOG2_SEL_BLOCK = int(math.log2(SEL_BLOCK))
WIN_KEYS = WINDOW + Q_TILE


def _cover_t(n_cmp_pad):
    n = jnp.arange(n_cmp_pad, dtype=jnp.int32)[None, :] * CMP_STRIDE
    s = jnp.arange(N_BLK_PAD, dtype=jnp.int32)[:, None] * SEL_BLOCK
    return ((n < s + SEL_BLOCK) & (n + CMP_LEN > s)).astype(BF16)


def _split3_nt(w, x):
    hi = x.astype(BF16)
    r1 = x - hi.astype(F32)
    mid = r1.astype(BF16)
    lo = (r1 - mid.astype(F32)).astype(BF16)
    return _nt_dot(w, hi) + _nt_dot(w, mid) + _nt_dot(w, lo)


def _topk_block_mask(imp_t, impt_ref, n_rows, n_keep):
    impt_ref[...] = imp_t
    s_idx = lax.broadcasted_iota(jnp.int32, imp_t.shape, 0)

    def body(sp, rank):
        row = impt_ref[pl.ds(sp, 1), :]
        beats = (row > imp_t) | ((row == imp_t) & (s_idx > sp))
        return rank + jnp.where(beats, 1.0, 0.0)

    rank = lax.fori_loop(0, n_rows, body, jnp.zeros(imp_t.shape, F32))
    return jnp.where(rank < n_keep, 0.0, NEG)


def _cmp_branch(q, s_ref, cmp_ref, ocmp_ref, t_pos, rows):
    ncp = cmp_ref.shape[1]
    kc = cmp_ref[0, :, 0:LANES]
    vc = cmp_ref[0, :, LANES:2 * LANES]
    s_ref[:, 0:ncp] = _nt_dot(q, kc)
    n_idx = lax.broadcasted_iota(jnp.int32, (rows, ncp), 1)
    ok = (n_idx * CMP_STRIDE + (CMP_LEN - 1) <= t_pos) & (n_idx < ncp - 1)
    imps = []
    for g in range(N_KV):
        psum = jnp.zeros((rows, ncp), F32)
        for h in range(HPG):
            hh = g * HPG + h
            s = jnp.where(ok, s_ref[hh * rows:(hh + 1) * rows, 0:ncp], NEG)
            m = jnp.max(s, axis=-1, keepdims=True)
            e = jnp.where(ok, jnp.exp(s - m), 0.0)
            l = jnp.sum(e, axis=-1, keepdims=True)
            p = e * jnp.where(l > 0.0, 1.0 / l, 0.0)
            ocmp_ref[hh] = _dot(p.astype(BF16), vc)
            psum = psum + p
        imps.append(psum)
    return imps


def _nsa_prompt_kernel(q_ref, gate_ref, kvs_ref, kvw_ref, cmp_ref, covt_ref, out_ref,
                       qs_ref, s_ref, impt_ref, msel_ref, m_ref, l_ref, acc_ref, ocmp_ref):
    i = pl.program_id(1)
    t0 = i * Q_TILE
    for h in range(N_HEADS):
        qs_ref[h * Q_TILE:(h + 1) * Q_TILE, :] = q_ref[0, :, h * LANES:(h + 1) * LANES]
    t_col = t0 + lax.broadcasted_iota(jnp.int32, (Q_TILE, 1), 0)

    psums = _cmp_branch(qs_ref[...], s_ref, cmp_ref, ocmp_ref, t_col, Q_TILE)
    s_idx = lax.broadcasted_iota(jnp.int32, (N_BLK_PAD, Q_TILE), 0)
    t_row = t0 + lax.broadcasted_iota(jnp.int32, (N_BLK_PAD, Q_TILE), 1)
    cur = jnp.right_shift(t_row, LOG2_SEL_BLOCK)
    forced = (s_idx == 0) | (s_idx == cur) | (s_idx == cur - 1)
    n_live = jnp.right_shift(t0, LOG2_SEL_BLOCK) + Q_TILE // SEL_BLOCK
    for g in range(N_KV):
        imp_t = _split3_nt(covt_ref[...], psums[g])
        imp_t = jnp.where(forced, FORCE, imp_t)
        imp_t = jnp.where(s_idx * SEL_BLOCK <= t_row, imp_t, NEG)
        mask_t = _topk_block_mask(imp_t, impt_ref, n_live, N_SEL)
        msel_ref[g] = mask_t.T.astype(BF16)

    m_ref[...] = jnp.full(m_ref.shape, NEG, F32)
    l_ref[...] = jnp.zeros(l_ref.shape, F32)
    acc_ref[...] = jnp.zeros(acc_ref.shape, F32)
    blk_row = lax.broadcasted_iota(jnp.int32, (N_BLK_PAD, KEY_CHUNK), 0)
    key_lane = lax.broadcasted_iota(jnp.int32, (Q_TILE, KEY_CHUNK), 1)

    def chunk(c, carry):
        k0 = pl.multiple_of(c * KEY_CHUNK, KEY_CHUNK)
        kt = kvs_ref[0, 0:LANES, pl.ds(k0, KEY_CHUNK)]
        vt = kvs_ref[0, LANES:2 * LANES, pl.ds(k0, KEY_CHUNK)]
        s_ref[:, 0:KEY_CHUNK] = _dot(qs_ref[...], kt)
        expand = jnp.where(blk_row == jnp.right_shift(k0 + key_lane, LOG2_SEL_BLOCK), 1.0, 0.0).astype(BF16)
        causal = jnp.where(k0 + key_lane <= t_col, 0.0, NEG)
        for g in range(N_KV):
            bias = _dot(msel_ref[g], expand) + causal
            for h in range(HPG):
                hh = g * HPG + h
                s = s_ref[hh * Q_TILE:(hh + 1) * Q_TILE, 0:KEY_CHUNK] + bias
                m_old = m_ref[hh]
                m_new = jnp.maximum(m_old, jnp.max(s, axis=-1, keepdims=True))
                alpha = jnp.exp(m_old - m_new)
                p = jnp.exp(s - m_new[:, 0:1])
                l_ref[hh] = alpha * l_ref[hh] + jnp.sum(p, axis=-1, keepdims=True)
                acc_ref[hh] = alpha * acc_ref[hh] + _nt_dot(p.astype(BF16), vt)
                m_ref[hh] = m_new
        return carry

    lax.fori_loop(0, t0 // KEY_CHUNK + 1, chunk, 0)

    start = pl.multiple_of(jnp.maximum(t0 - WINDOW, 0), Q_TILE)
    kwt = kvw_ref[0, 0:LANES, pl.ds(start, WIN_KEYS)]
    vwt = kvw_ref[0, LANES:2 * LANES, pl.ds(start, WIN_KEYS)]
    s_ref[:, 0:WIN_KEYS] = _dot(qs_ref[...], kwt)
    dist = t_col - (start + lax.broadcasted_iota(jnp.int32, (Q_TILE, WIN_KEYS), 1))
    bias_w = jnp.where((dist >= 0) & (dist <= WINDOW), 0.0, NEG)

    gates = jax.nn.sigmoid(gate_ref[0])
    for hh in range(N_HEADS):
        s = s_ref[hh * Q_TILE:(hh + 1) * Q_TILE, 0:WIN_KEYS] + bias_w
        p = jnp.exp(s - jnp.max(s, axis=-1, keepdims=True))
        o_win = _nt_dot(p.astype(BF16), vwt) / jnp.sum(p, axis=-1, keepdims=True)
        o_sel = acc_ref[hh] / l_ref[hh]
        o = (gates[:, 3 * hh:3 * hh + 1] * ocmp_ref[hh] + gates[:, 3 * hh + 1:3 * hh + 2] * o_sel
             + gates[:, 3 * hh + 2:3 * hh + 3] * o_win)
        out_ref[0, :, hh * LANES:(hh + 1) * LANES] = o.astype(out_ref.dtype)


def _nsa_prompt(q, gates, kvs, kvw, cmp):
    b, t, _ = q.shape
    ncp = cmp.shape[1]
    tile = lambda w: pl.BlockSpec((1, Q_TILE, w), lambda bb, i: (bb, i, 0))
    whole = lambda r, w: pl.BlockSpec((1, r, w), lambda bb, i: (bb, 0, 0))
    rows = N_HEADS * Q_TILE
    return pl.pallas_call(
        _nsa_prompt_kernel,
        grid=(b, t // Q_TILE),
        in_specs=[tile(QPAD), tile(LANES), whole(2 * LANES, t), whole(2 * LANES, t),
                  whole(ncp, 2 * LANES), pl.BlockSpec((N_BLK_PAD, ncp), lambda bb, i: (0, 0))],
        out_specs=tile(QPAD),
        out_shape=jax.ShapeDtypeStruct((b, t, QPAD), BF16),
        scratch_shapes=[
            pltpu.VMEM((rows, LANES), BF16),
            pltpu.VMEM((rows, max(WIN_KEYS, ncp)), F32),
            pltpu.VMEM((N_BLK_PAD, Q_TILE), F32),
            pltpu.VMEM((N_KV, Q_TILE, N_BLK_PAD), BF16),
            pltpu.VMEM((N_HEADS, Q_TILE, LANES), F32),
            pltpu.VMEM((N_HEADS, Q_TILE, LANES), F32),
            pltpu.VMEM((N_HEADS, Q_TILE, LANES), F32),
            pltpu.VMEM((N_HEADS, Q_TILE, LANES), F32),
        ],
        compiler_params=_cparams("parallel", "arbitrary"),
        name="nsa_prompt",
    )(q, gates, kvs, kvw, cmp, _cover_t(ncp))


SEL_PAGES = 8


def _flash_update(s, vt, m_ref, l_ref, acc_ref):
    m_old = m_ref[...]
    m_new = jnp.maximum(m_old, jnp.max(s, axis=-1, keepdims=True))
    alpha = jnp.exp(m_old - m_new)
    p = jnp.exp(s - m_new[:, 0:1])
    l_ref[...] = alpha * l_ref[...] + jnp.sum(p, axis=-1, keepdims=True)
    acc_ref[...] = alpha * acc_ref[...] + _nt_dot(p.astype(BF16), vt)
    m_ref[...] = m_new


def _nsa_sample_kernel(tbl_ref, *refs, n_pages, tq):
    del tbl_ref
    q_ref, gate_ref = refs[:2]
    page_refs = refs[2:2 + SEL_PAGES]
    (cmp_ref, win_ref, newkv_ref, newwin_ref, covt_ref, out_ref, qs_ref, s_ref, impt_ref, msel_ref,
     m_ref, l_ref, acc_ref, ocmp_ref) = refs[2 + SEL_PAGES:]
    j = pl.program_id(1)
    past_len = n_pages * PAGE_SIZE
    rows = N_HEADS * tq
    step_keys = SEL_PAGES * PAGE_SIZE
    i_col = lax.broadcasted_iota(jnp.int32, (rows, 1), 0) & (tq - 1)

    @pl.when(j == 0)
    def _():
        for h in range(N_HEADS):
            qs_ref[h * tq:(h + 1) * tq, :] = q_ref[0, :, h * LANES:(h + 1) * LANES]
        t_col = past_len + lax.broadcasted_iota(jnp.int32, (tq, 1), 0)
        psums = _cmp_branch(qs_ref[...].astype(BF16), s_ref, cmp_ref, ocmp_ref, t_col, tq)
        s_idx = lax.broadcasted_iota(jnp.int32, (N_BLK_PAD, LANES), 0)
        t_row = past_len + lax.broadcasted_iota(jnp.int32, (N_BLK_PAD, LANES), 1)
        cur = jnp.right_shift(t_row, LOG2_SEL_BLOCK)
        forced = (s_idx == 0) | (s_idx == cur) | (s_idx == cur - 1)
        ncp = cmp_ref.shape[1]
        for g in range(N_KV):
            psum = jnp.concatenate([psums[g], jnp.zeros((LANES - tq, ncp), F32)], axis=0)
            imp_t = jnp.where(forced, FORCE, _split3_nt(covt_ref[...], psum))
            mask_t = _topk_block_mask(imp_t, impt_ref, N_BLK_PAD, N_SEL - 1)
            msel_ref[g] = mask_t.T.astype(BF16)
        m_ref[...] = jnp.full(m_ref.shape, NEG, F32)
        l_ref[...] = jnp.zeros(l_ref.shape, F32)
        acc_ref[...] = jnp.zeros(acc_ref.shape, F32)

    q = qs_ref[...].astype(BF16)
    kt = jnp.concatenate([page[0].astype(BF16) for page in page_refs], axis=1)
    vt = jnp.concatenate([page[1].astype(BF16) for page in page_refs], axis=1)
    blk_row = lax.broadcasted_iota(jnp.int32, (N_BLK_PAD, step_keys), 0)
    key_pos = j * step_keys + lax.broadcasted_iota(jnp.int32, (N_BLK_PAD, step_keys), 1)
    expand = jnp.where(blk_row == jnp.right_shift(key_pos, LOG2_SEL_BLOCK), 1.0, 0.0).astype(BF16)
    bias_g = [_dot(msel_ref[g], expand)[0:tq, :] for g in range(N_KV)]
    bias = jnp.concatenate([bias_g[hh // HPG] for hh in range(N_HEADS)], axis=0)
    _flash_update(_dot(q, kt) + bias, vt, m_ref, l_ref, acc_ref)

    @pl.when(j == n_pages // SEL_PAGES - 1)
    def _():
        j_lane = lax.broadcasted_iota(jnp.int32, (rows, PAGE_SIZE), 1)
        new_bias = jnp.where(j_lane <= i_col, 0.0, NEG)
        _flash_update(_dot(q, newkv_ref[0, 0:LANES, :]) + new_bias, newkv_ref[0, LANES:2 * LANES, :],
                      m_ref, l_ref, acc_ref)
        wb = win_ref.shape[2]
        dist = wb + i_col - lax.broadcasted_iota(jnp.int32, (rows, wb), 1)
        s1 = _dot(q, win_ref[0, 0:LANES, :].astype(BF16)) + jnp.where((dist >= 0) & (dist <= WINDOW), 0.0, NEG)
        s2 = _dot(q, newwin_ref[0, 0:LANES, :]) + new_bias
        m = jnp.maximum(jnp.max(s1, axis=-1, keepdims=True), jnp.max(s2, axis=-1, keepdims=True))
        p1 = jnp.exp(s1 - m)
        p2 = jnp.exp(s2 - m)
        l = jnp.sum(p1, axis=-1, keepdims=True) + jnp.sum(p2, axis=-1, keepdims=True)
        o_win = (_nt_dot(p1.astype(BF16), win_ref[0, LANES:2 * LANES, :].astype(BF16))
                 + _nt_dot(p2.astype(BF16), newwin_ref[0, LANES:2 * LANES, :])) / l
        o_sel = acc_ref[...] / l_ref[...]
        gates = jax.nn.sigmoid(gate_ref[0])
        for hh in range(N_HEADS):
            r = slice(hh * tq, (hh + 1) * tq)
            o = (gates[:, 3 * hh:3 * hh + 1] * ocmp_ref[hh] + gates[:, 3 * hh + 1:3 * hh + 2] * o_sel[r]
                 + gates[:, 3 * hh + 2:3 * hh + 3] * o_win[r])
            out_ref[0, :, hh * LANES:(hh + 1) * LANES] = o


def _nsa_sample(q, gates, pages, table, cmp, win, newkv, newwin):
    s, tq, _ = q.shape
    n_pages = table.shape[1]
    ncp = cmp.shape[1]
    wb = win.shape[2]
    assert (n_pages * PAGE_SIZE) // SEL_BLOCK == N_BLK_PAD and tq <= SEL_BLOCK and tq & (tq - 1) == 0
    assert n_pages % SEL_PAGES == 0
    rows = N_HEADS * tq
    per_seq = lambda r, w: pl.BlockSpec((1, r, w), lambda b, j, t: (b, 0, 0))
    page_spec = lambda k: pl.BlockSpec((None, 2, LANES, PAGE_SIZE),
                                       lambda b, j, t: (t[b, j * SEL_PAGES + k], 1, 0, 0))
    grid_spec = pltpu.PrefetchScalarGridSpec(
        num_scalar_prefetch=1,
        grid=(s, n_pages // SEL_PAGES),
        in_specs=[per_seq(tq, QPAD), per_seq(tq, LANES)] + [page_spec(k) for k in range(SEL_PAGES)] + [
            per_seq(ncp, 2 * LANES), per_seq(2 * LANES, wb),
            per_seq(2 * LANES, PAGE_SIZE), per_seq(2 * LANES, PAGE_SIZE),
            pl.BlockSpec((N_BLK_PAD, ncp), lambda b, j, t: (0, 0))],
        out_specs=per_seq(tq, QPAD),
        scratch_shapes=[
            pltpu.VMEM((rows, LANES), F32),
            pltpu.VMEM((rows, ncp), F32),
            pltpu.VMEM((N_BLK_PAD, LANES), F32),
            pltpu.VMEM((N_KV, LANES, N_BLK_PAD), BF16),
            pltpu.VMEM((rows, LANES), F32),
            pltpu.VMEM((rows, LANES), F32),
            pltpu.VMEM((rows, LANES), F32),
            pltpu.VMEM((N_HEADS, tq, LANES), F32),
        ],
    )
    return pl.pallas_call(
        functools.partial(_nsa_sample_kernel, n_pages=n_pages, tq=tq),
        grid_spec=grid_spec,
        out_shape=jax.ShapeDtypeStruct((s, tq, QPAD), F32),
        compiler_params=_cparams("parallel", "arbitrary"),
        name="nsa_sample",
    )(table, q, gates, *([pages] * SEL_PAGES), cmp, win, newkv, newwin, _cover_t(ncp))


S5_LANE_BLOCK = 512


def _s5_kernel(u_ref, z_ref, h0_ref, bbd_ref, lam_ref, cbd_ref, d_ref, y_ref, hl_ref,
               bu_ref, hs_ref, h_ref):
    c = pl.program_id(1)
    steps = u_ref.shape[1]

    @pl.when(c == 0)
    def _():
        h_ref[...] = h0_ref[0]

    u = u_ref[0]
    bu_ref[...] = _dot(u.astype(BF16), bbd_ref[...])
    for blk in range(N_STATE // S5_LANE_BLOCK):
        re = slice(blk * S5_LANE_BLOCK, (blk + 1) * S5_LANE_BLOCK)
        im = slice(N_STATE + blk * S5_LANE_BLOCK, N_STATE + (blk + 1) * S5_LANE_BLOCK)
        lr = lam_ref[:, re]
        li = lam_ref[:, im]

        def step(t, carry, re=re, im=im, lr=lr, li=li):
            hr, hi = carry
            row = pl.ds(t, 1)
            nr = lr * hr - li * hi + bu_ref[row, re]
            ni = lr * hi + li * hr + bu_ref[row, im]
            hs_ref[row, re] = nr
            hs_ref[row, im] = ni
            return nr, ni

        hr, hi = lax.fori_loop(0, steps, step, (h_ref[:, re], h_ref[:, im]),
                               unroll=min(8, steps))
        h_ref[:, re] = hr
        h_ref[:, im] = hi
    y = _dot(hs_ref[...].astype(BF16), cbd_ref[...])
    y = y + d_ref[...] * u
    y_ref[0] = (jax.nn.gelu(y) * jax.nn.sigmoid(z_ref[0])).astype(y_ref.dtype)

    @pl.when(c == pl.num_programs(1) - 1)
    def _():
        hl_ref[0] = h_ref[...]


def _s5_weights(lam_re, lam_im, log_dt, b_re, b_im, c_re, c_im, d_skip):
    lam = lax.complex(lam_re.astype(F32), lam_im.astype(F32))
    dt = jnp.exp(log_dt.astype(F32))[:, None]
    lam_bar = jnp.exp(lam * dt)
    b_bar = ((lam_bar - 1.0) / lam)[..., None] * lax.complex(b_re.astype(F32), b_im.astype(F32))
    eye = jnp.eye(N_SSM_GROUPS, dtype=F32)
    def in_bd(b):
        return jnp.einsum("gph,gk->ghkp", b, eye).reshape(SSM_WIDTH, N_STATE)
    bbd = jnp.concatenate([in_bd(b_bar.real), in_bd(b_bar.imag)], axis=1)
    def out_bd(cm):
        return jnp.einsum("ghp,gk->gpkh", cm, eye).reshape(N_STATE, SSM_WIDTH)
    cbd = jnp.concatenate([out_bd(c_re.astype(F32)), -out_bd(c_im.astype(F32))], axis=0)
    lam_row = jnp.concatenate([lam_bar.real.reshape(1, N_STATE), lam_bar.imag.reshape(1, N_STATE)], axis=1)
    return bbd.astype(BF16), lam_row, cbd.astype(BF16), d_skip.astype(F32).reshape(1, SSM_WIDTH)


def _s5(u, z, h0, s5w, chunk, out_dtype):
    bbd, lam_row, cbd, d_row = s5w
    s, t, _ = u.shape
    const = lambda shape: pl.BlockSpec(shape, lambda b, c: (0,) * len(shape))
    return pl.pallas_call(
        _s5_kernel,
        grid=(s, t // chunk),
        in_specs=[
            pl.BlockSpec((1, chunk, SSM_WIDTH), lambda b, c: (b, c, 0)),
            pl.BlockSpec((1, chunk, SSM_WIDTH), lambda b, c: (b, c, 0)),
            pl.BlockSpec((1, 1, 2 * N_STATE), lambda b, c: (b, 0, 0)),
            const((SSM_WIDTH, 2 * N_STATE)), const((1, 2 * N_STATE)),
            const((2 * N_STATE, SSM_WIDTH)), const((1, SSM_WIDTH)),
        ],
        out_specs=[
            pl.BlockSpec((1, chunk, SSM_WIDTH), lambda b, c: (b, c, 0)),
            pl.BlockSpec((1, 1, 2 * N_STATE), lambda b, c: (b, 0, 0)),
        ],
        out_shape=[jax.ShapeDtypeStruct((s, t, SSM_WIDTH), out_dtype),
                   jax.ShapeDtypeStruct((s, 1, 2 * N_STATE), F32)],
        scratch_shapes=[pltpu.VMEM((chunk, 2 * N_STATE), F32),
                        pltpu.VMEM((chunk, 2 * N_STATE), F32),
                        pltpu.VMEM((1, 2 * N_STATE), F32)],
        compiler_params=_cparams("parallel", "arbitrary"),
        name="s5",
    )(u, z, h0, bbd, lam_row, cbd, d_row)


def _outproj_kernel(x_ref, a_ref, s_ref, wa_ref, ws_ref, y_ref):
    y = x_ref[...]
    y = y + _dot(a_ref[...].astype(BF16), wa_ref[...])
    y = y + _dot(s_ref[...].astype(BF16), ws_ref[...])
    y_ref[...] = y


def _outproj_weights(w_out):
    wa = w_out[:NSA_WIDTH].reshape(N_KV, HPG, 1, HEAD_DIM, D_MODEL)
    slot = jnp.eye(N_KV, dtype=F32).reshape(N_KV, 1, N_KV, 1, 1)
    return (wa * slot).reshape(QPAD, D_MODEL).astype(BF16), w_out[NSA_WIDTH:].astype(BF16)


def _outproj(x2d, a_out, s_out, wa, ws, tm):
    n = x2d.shape[0]
    row = lambda w: pl.BlockSpec((tm, w), lambda i: (i, 0))
    return pl.pallas_call(
        _outproj_kernel,
        grid=(n // tm,),
        in_specs=[row(D_MODEL), row(QPAD), row(SSM_WIDTH),
                  pl.BlockSpec((QPAD, D_MODEL), lambda i: (0, 0)),
                  pl.BlockSpec((SSM_WIDTH, D_MODEL), lambda i: (0, 0))],
        out_specs=row(D_MODEL),
        out_shape=jax.ShapeDtypeStruct((n, D_MODEL), F32),
        compiler_params=_cparams("parallel"),
        name="outproj",
    )(x2d, a_out, s_out, wa, ws)


LOWEST = -3.0e38
PEER_HALF = PEER_HEADS * PEER_DK // 2
_CAND_COUNTS = tuple(PEER_TOPK // (a + 1) for a in range(PEER_TOPK))
N_CAND = sum(_CAND_COUNTS)
N_CAND_PAD = -(-N_CAND // 8) * 8


def _top_rows(s, out_ref, n):
    kidx = lax.broadcasted_iota(jnp.int32, s.shape, 0).astype(F32)
    rem = s
    for a in range(n):
        mx = jnp.max(rem, axis=0, keepdims=True)
        first = jnp.min(jnp.where(rem == mx, kidx, float(s.shape[0])), axis=0, keepdims=True)
        out_ref[a:a + 1, :] = mx
        rem = jnp.where(kidx == first, LOWEST, rem)


def _peer_route_kernel(y_ref, g_ref, wq_ref, k1_ref, k2_ref, xn_ref, s1_ref, tau_ref, e1_ref,
                       s2_ref, e2_ref, v1_ref, v2_ref, cand_ref):
    xn = _rmsnorm(y_ref[...], g_ref[...]).astype(BF16)
    xn_ref[...] = xn
    q = _dot(xn, wq_ref[...]).astype(BF16)
    s1_ref[...] = _nt_dot(k1_ref[...], q[:, :PEER_HALF])
    s2_ref[...] = _nt_dot(k2_ref[...], q[:, PEER_HALF:])
    t = y_ref.shape[0]
    cand_ref[N_CAND:N_CAND_PAD, :] = jnp.full((N_CAND_PAD - N_CAND, t), LOWEST, F32)
    for h in range(PEER_HEADS):
        rows = slice(h * N_KEYS, (h + 1) * N_KEYS)
        s1 = s1_ref[rows, :]
        s2 = s2_ref[rows, :]
        _top_rows(s1, v1_ref, PEER_TOPK)
        _top_rows(s2, v2_ref, PEER_TOPK)
        v1 = v1_ref[...]
        v2 = v2_ref[...]
        off = 0
        for a, nb in enumerate(_CAND_COUNTS):
            cand_ref[off:off + nb, :] = v1[a:a + 1, :] + v2[0:nb, :]
            off += nb
        cand = cand_ref[...]
        rem = cand
        tau = jnp.zeros((1, t), F32)
        done = jnp.zeros((1, t), F32)
        for _ in range(PEER_TOPK):
            mx = jnp.max(rem, axis=0, keepdims=True)
            cnt = jnp.sum(jnp.where(cand >= mx, 1.0, 0.0), axis=0, keepdims=True)
            newly = (cnt >= float(PEER_TOPK)) & (done < 0.5)
            tau = jnp.where(newly, mx, tau)
            done = jnp.where(newly, 1.0, done)
            rem = jnp.where(rem >= mx, LOWEST, rem)
        top = v1[0:1, :] + v2[0:1, :]
        z = jnp.sum(jnp.where(cand >= tau, jnp.exp(cand - top), 0.0), axis=0, keepdims=True)
        tau_ref[h:h + 1, :] = tau
        e1_ref[rows, :] = jnp.exp(s1 - v1[0:1, :]) / z
        e2_ref[rows, :] = jnp.exp(s2 - v2[0:1, :])


def _peer_weights(w_q, sub_k1, sub_k2):
    wq = w_q.reshape(D_MODEL, PEER_HEADS, 2, PEER_DK // 2).transpose(0, 2, 1, 3).reshape(D_MODEL, 2 * PEER_HALF)
    eye = jnp.eye(PEER_HEADS, dtype=F32)
    bd = lambda k: jnp.einsum("hkd,hj->hkjd", k, eye).reshape(PEER_HEADS * N_KEYS, PEER_HALF)
    return wq.astype(BF16), bd(sub_k1).astype(BF16), bd(sub_k2).astype(BF16)


def _peer_route(y2d, norm_g, wq, k1bd, k2bd, tm):
    n = y2d.shape[0]
    hk = PEER_HEADS * N_KEYS
    const = lambda a: pl.BlockSpec(a.shape, lambda i: (0, 0))
    col = lambda r: pl.BlockSpec((r, tm), lambda i: (0, i))
    tshape = jax.ShapeDtypeStruct((hk, n), F32)
    return pl.pallas_call(
        _peer_route_kernel,
        grid=(n // tm,),
        in_specs=[pl.BlockSpec((tm, D_MODEL), lambda i: (i, 0)),
                  pl.BlockSpec((1, D_MODEL), lambda i: (0, 0)), const(wq), const(k1bd), const(k2bd)],
        out_specs=[pl.BlockSpec((tm, D_MODEL), lambda i: (i, 0)), col(hk), col(PEER_HEADS),
                   col(hk), col(hk), col(hk)],
        out_shape=[jax.ShapeDtypeStruct((n, D_MODEL), BF16), tshape,
                   jax.ShapeDtypeStruct((PEER_HEADS, n), F32), tshape, tshape, tshape],
        scratch_shapes=[pltpu.VMEM((PEER_TOPK, tm), F32), pltpu.VMEM((PEER_TOPK, tm), F32),
                        pltpu.VMEM((N_CAND_PAD, tm), F32)],
        compiler_params=_cparams("parallel"),
        name="peer_route",
    )(y2d, norm_g.reshape(1, D_MODEL), wq, k1bd, k2bd)


EXPERT_BLOCK = 1024
I1_PER_BLOCK = EXPERT_BLOCK // N_KEYS


def _peer_dense_kernel(y_ref, xn_ref, u_ref, vt_ref, s1_ref, tau_ref, e1_ref, s2_ref, e2_ref,
                       gf_ref, out_ref, acc_ref, ga_ref):
    j = pl.program_id(1)
    t = xn_ref.shape[0]

    @pl.when(j == 0)
    def _():
        acc_ref[...] = jnp.zeros(acc_ref.shape, F32)

    act = jax.nn.gelu(_nt_dot(u_ref[...], xn_ref[...]))
    for tc in range(t // LANES):
        cols = slice(tc * LANES, (tc + 1) * LANES)
        for ii in range(I1_PER_BLOCK):
            gate = jnp.zeros((N_KEYS, LANES), F32)
            for h in range(PEER_HEADS):
                grp = pl.ds(pl.multiple_of(h * N_KEYS + j * I1_PER_BLOCK, I1_PER_BLOCK), I1_PER_BLOCK)
                krows = slice(h * N_KEYS, (h + 1) * N_KEYS)
                s1 = s1_ref[grp, cols][ii:ii + 1, :]
                e1 = e1_ref[grp, cols][ii:ii + 1, :]
                picked = (s1 + s2_ref[krows, cols]) >= tau_ref[h:h + 1, cols]
                gate = gate + jnp.where(picked, e1 * e2_ref[krows, cols], 0.0)
            erows = slice(ii * N_KEYS, (ii + 1) * N_KEYS)
            ga_ref[erows, cols] = (gate * act[erows, cols]).astype(BF16)
    acc_ref[...] += _dot(vt_ref[...], ga_ref[...])

    @pl.when(j == pl.num_programs(1) - 1)
    def _():
        y = y_ref[...] + acc_ref[...].T
        out_ref[...] = _rmsnorm(y, gf_ref[...])


def _peer_dense(y2d, xn, u_bf, vt_bf, s1, tau, e1, s2, e2, norm_f, tm):
    n = y2d.shape[0]
    hk = PEER_HEADS * N_KEYS
    tok = lambda w: pl.BlockSpec((tm, w), lambda i, j: (i, 0))
    col = lambda r: pl.BlockSpec((r, tm), lambda i, j: (0, i))
    return pl.pallas_call(
        _peer_dense_kernel,
        grid=(n // tm, N_EXPERTS // EXPERT_BLOCK),
        in_specs=[tok(D_MODEL), tok(D_MODEL),
                  pl.BlockSpec((EXPERT_BLOCK, D_MODEL), lambda i, j: (j, 0)),
                  pl.BlockSpec((D_MODEL, EXPERT_BLOCK), lambda i, j: (0, j)),
                  col(hk), col(PEER_HEADS), col(hk), col(hk), col(hk),
                  pl.BlockSpec((1, D_MODEL), lambda i, j: (0, 0))],
        out_specs=tok(D_MODEL),
        out_shape=jax.ShapeDtypeStruct((n, D_MODEL), F32),
        scratch_shapes=[pltpu.VMEM((D_MODEL, tm), F32), pltpu.VMEM((EXPERT_BLOCK, tm), BF16)],
        compiler_params=_cparams("parallel", "arbitrary"),
        name="peer_dense",
    )(y2d, xn, u_bf, vt_bf, s1, tau, e1, s2, e2, norm_f.reshape(1, D_MODEL))


PROJ_TILE = 512
PEER_TILE = 256
S5_CHUNK = 256


def _state_to_rows(st):
    s = st.shape[0]
    return jnp.concatenate([st[..., 0].reshape(s, 1, N_STATE), st[..., 1].reshape(s, 1, N_STATE)], axis=-1)


def _rows_to_state(h):
    s = h.shape[0]
    shape = (s, N_SSM_GROUPS, SSM_STATE)
    return jnp.stack([h[:, 0, :N_STATE].reshape(shape), h[:, 0, N_STATE:].reshape(shape)], axis=-1)


def _feat_to_tokens(a, n_types):
    s, _, t = a.shape
    return a.reshape(s, n_types, N_KV, HEAD_DIM, t).transpose(0, 4, 1, 2, 3)


def _peer_block(y2d, norm_g, peer_w, norm_f, tm):
    wq, k1bd, k2bd, u_bf, vt_bf = peer_w
    xn, s1, tau, e1, s2, e2 = _peer_route(y2d, norm_g, wq, k1bd, k2bd, tm)
    return _peer_dense(y2d, xn, u_bf, vt_bf, s1, tau, e1, s2, e2, norm_f, tm)


def kernel(x_prompt, x_sample, cache_kv, cache_win, state_ssm, page_table, norm_mix, w_in, w_cmp1, w_cmp2, pe_cmp, lam_re, lam_im, log_dt, b_re, b_im, c_re, c_im, d_skip, w_out, norm_ffn, w_q_peer, sub_k1, sub_k2, u_tab, v_tab, norm_final):
    b, t, d = x_prompt.shape
    db, ts, _ = x_sample.shape
    assert w_in.shape[0] == DEPTH == 1 and d == D_MODEL
    l = 0
    n_pool = cache_kv.shape[1]
    wb = cache_win.shape[2]

    w_tok, w_feat = _proj_weights(w_in[l])
    cmp_w = _compress_weights(w_cmp1[l], w_cmp2[l], pe_cmp[l])
    s5_w = _s5_weights(lam_re[l], lam_im[l], log_dt[l], b_re[l], b_im[l], c_re[l], c_im[l], d_skip[l])
    wa, ws = _outproj_weights(w_out[l])
    peer_w = _peer_weights(w_q_peer[l], sub_k1[l], sub_k2[l]) + (
        u_tab[l].astype(BF16), v_tab[l].T.astype(BF16))

    q, gate, u, z, kvsel_t, ksvs_t, kvwin_t, kwvw_t = _project(x_prompt, norm_mix[l], w_tok, w_feat, PROJ_TILE, BF16)
    n_pg = t // PAGE_SIZE
    table_p = jnp.broadcast_to(jnp.arange(n_pg, dtype=jnp.int32), (b, n_pg))
    prompt_page = lambda k: pl.BlockSpec((None, 2, LANES, PAGE_SIZE),
                                         lambda bb, j, tb: (bb, 0, 0, tb[bb, j * CMP_PAGES + k]))
    cmp_p = _compress(kvsel_t.reshape(b, 4, LANES, t), prompt_page, table_p, *cmp_w)
    a_p = _nsa_prompt(q.reshape(b, t, QPAD), gate.reshape(b, t, LANES), ksvs_t, kwvw_t, cmp_p)
    s_p, h_p = _s5(u.reshape(b, t, SSM_WIDTH), z.reshape(b, t, SSM_WIDTH),
                   jnp.zeros((b, 1, 2 * N_STATE), F32), s5_w, S5_CHUNK, BF16)
    y1p = _outproj(x_prompt.reshape(b * t, d), a_p.reshape(b * t, QPAD), s_p.reshape(b * t, SSM_WIDTH),
                   wa, ws, PROJ_TILE)
    y_prompt = _peer_block(y1p, norm_ffn[l], peer_w, norm_final, PEER_TILE).reshape(b, t, d)
    kv_prompt = _feat_to_tokens(kvsel_t, 4)
    win_prompt = _feat_to_tokens(kvwin_t[:, :, t - min(WINDOW, t):], 2)
    ssm_prompt = _rows_to_state(h_p)

    qs, gate_s, u_s, z_s, kvsel_st, ksvs_st, kvwin_st, kwvw_st = _project(
        x_sample.reshape(1, db * ts, d), norm_mix[l], w_tok, w_feat, db * ts, F32)
    pages_s = cache_kv[l].transpose(0, 2, 3, 4, 1).reshape(n_pool, 4, LANES, PAGE_SIZE)
    sample_page = lambda k: pl.BlockSpec((None, 2, LANES, PAGE_SIZE),
                                         lambda bb, j, tb: (tb[bb, j * CMP_PAGES + k], 0, 0, 0))
    cmp_s = _compress(pages_s, sample_page, page_table, *cmp_w)
    new_page = lambda a: jnp.pad(a[0].reshape(2 * LANES, db, ts).transpose(1, 0, 2),
                                 ((0, 0), (0, 0), (0, PAGE_SIZE - ts)))
    win_t = cache_win[l].transpose(0, 2, 3, 4, 1).reshape(db, 2 * LANES, wb)
    a_s = _nsa_sample(qs.reshape(db, ts, QPAD), gate_s.reshape(db, ts, LANES), pages_s, page_table, cmp_s,
                      win_t, new_page(ksvs_st), new_page(kwvw_st))
    s_s, h_s = _s5(u_s.reshape(db, ts, SSM_WIDTH), z_s.reshape(db, ts, SSM_WIDTH),
                   _state_to_rows(state_ssm[l].astype(F32)), s5_w, ts, F32)
    y1s = _outproj(x_sample.reshape(db * ts, d), a_s.reshape(db * ts, QPAD), s_s.reshape(db * ts, SSM_WIDTH),
                   wa, ws, db * ts)
    y_sample = _peer_block(y1s, norm_ffn[l], peer_w, norm_final, PEER_TILE).reshape(db, ts, d)
    per_tok = lambda a, n_types: a[0].reshape(n_types, N_KV, HEAD_DIM, db, ts).transpose(3, 4, 0, 1, 2)
    kv_sample = per_tok(kvsel_st, 4)
    win_new = per_tok(kvwin_st, 2).astype(cache_win.dtype)
    win_sample = jnp.concatenate([cache_win[l], win_new], axis=1)[:, ts:]
    ssm_sample = _rows_to_state(h_s)

    return (y_prompt, y_sample, kv_prompt[None], kv_sample[None], win_prompt[None], win_sample[None],
            ssm_prompt[None], ssm_sample[None])
```

```python
import functools
import math

import jax
import jax.numpy as jnp
from jax import lax
from jax.experimental import pallas as pl
from jax.experimental.pallas import tpu as pltpu

D_MODEL = 1024
DEPTH = 1
PAGE_SIZE = 128
N_HEADS = 8
N_KV = 2
HEAD_DIM = 64
HPG = N_HEADS // N_KV
CMP_LEN = 32
CMP_STRIDE = 16
SEL_BLOCK = 64
N_SEL = 16
WINDOW = 512
SSM_WIDTH = 512
SSM_GROUP = 16
N_SSM_GROUPS = SSM_WIDTH // SSM_GROUP
SSM_STATE = 64
N_KEYS = 128
N_EXPERTS = N_KEYS * N_KEYS
PEER_HEADS = 8
PEER_DK = 128
PEER_TOPK = 16
NSA_WIDTH = N_HEADS * HEAD_DIM
KV_WIDTH = N_KV * HEAD_DIM
GATE_WIDTH = 3 * N_HEADS
EPS = 1e-6
NEG = -1e30
FORCE = 1e4
LOG2_E = 1.0 / math.log(2.0)
LOWEST = -3.0e38

LANES = 128
BF16_ROWS = 16
VMEM_LIMIT = 56 * 1024 * 1024

N_STATE = N_SSM_GROUPS * SSM_STATE
QPAD = N_HEADS * LANES
F32 = jnp.float32
BF16 = jnp.bfloat16

assert KV_WIDTH == LANES and PAGE_SIZE == LANES


def _cparams(*sem):
    return pltpu.CompilerParams(dimension_semantics=sem, vmem_limit_bytes=VMEM_LIMIT)


def _nt_dot(a, b):
    return lax.dot_general(a, b, (((1,), (1,)), ((), ())), preferred_element_type=F32)


def _dot(a, b):
    return jnp.dot(a, b, preferred_element_type=F32)


def _rmsnorm(x, g):
    return x * lax.rsqrt(jnp.mean(x * x, axis=-1, keepdims=True) + EPS) * g


_PROJ_COLS = (("q", QPAD), ("gate", LANES), ("u", SSM_WIDTH), ("z", SSM_WIDTH))
KVT_ROWS = 6 * KV_WIDTH


def _proj_kernel(x_ref, g_ref, w_ref, wt_ref, q_ref, gate_ref, u_ref, z_ref,
                 kvsel_ref, ksvs_ref, kvwin_ref, kwvw_ref):
    xn = _rmsnorm(x_ref[...], g_ref[...]).astype(BF16)
    off = 0
    for (name, width), ref in zip(_PROJ_COLS, (q_ref, gate_ref, u_ref, z_ref)):
        ref[...] = _dot(xn, w_ref[:, off:off + width]).astype(ref.dtype)
        off += width
    kvt = _nt_dot(wt_ref[...], xn)
    kvsel_ref[0] = kvt[0:4 * KV_WIDTH]
    ksvs_ref[0] = kvt[2 * KV_WIDTH:4 * KV_WIDTH].astype(BF16)
    kvwin_ref[0] = kvt[4 * KV_WIDTH:]
    kwvw_ref[0] = kvt[4 * KV_WIDTH:].astype(BF16)


def _proj_weights(w_in):
    c0 = NSA_WIDTH
    c1 = c0 + 6 * KV_WIDTH
    c2 = c1 + GATE_WIDTH
    wq = w_in[:, :c0].reshape(D_MODEL, N_KV, HPG, 1, HEAD_DIM) * (HEAD_DIM ** -0.5 * LOG2_E)
    slot = jnp.eye(N_KV, dtype=F32).reshape(1, N_KV, 1, N_KV, 1)
    wq_pad = (wq * slot).reshape(D_MODEL, QPAD)
    wg = jnp.pad(w_in[:, c1:c2], ((0, 0), (0, LANES - GATE_WIDTH)))
    w = jnp.concatenate([wq_pad, wg, w_in[:, c2:]], axis=1)
    return w.astype(BF16), w_in[:, c0:c1].T.astype(BF16)


def _project(x3d, norm_g, w_tok, w_feat, tm, q_dtype):
    s, t, _ = x3d.shape
    nt = t // tm
    n = s * t
    row = lambda w: pl.BlockSpec((tm, w), lambda b, i: (b * nt + i, 0))
    feat = lambda r: pl.BlockSpec((1, r, tm), lambda b, i: (b, 0, i))
    const = lambda a: pl.BlockSpec(a.shape, lambda b, i: (0, 0))
    shapes = [
        jax.ShapeDtypeStruct((n, QPAD), q_dtype),
        jax.ShapeDtypeStruct((n, LANES), F32),
        jax.ShapeDtypeStruct((n, SSM_WIDTH), F32),
        jax.ShapeDtypeStruct((n, SSM_WIDTH), F32),
        jax.ShapeDtypeStruct((s, 4 * KV_WIDTH, t), F32),
        jax.ShapeDtypeStruct((s, 2 * KV_WIDTH, t), BF16),
        jax.ShapeDtypeStruct((s, 2 * KV_WIDTH, t), F32),
        jax.ShapeDtypeStruct((s, 2 * KV_WIDTH, t), BF16),
    ]
    g2 = norm_g.reshape(1, D_MODEL)
    return pl.pallas_call(
        _proj_kernel,
        grid=(s, nt),
        in_specs=[row(D_MODEL), const(g2), const(w_tok), const(w_feat)],
        out_specs=[row(sh.shape[1]) for sh in shapes[:4]] + [feat(sh.shape[1]) for sh in shapes[4:]],
        out_shape=shapes,
        compiler_params=_cparams("parallel", "parallel"),
        name="proj",
    )(x3d.reshape(n, D_MODEL), g2, w_tok, w_feat)


CMP_PAGES = 8


def _compress_kernel(tbl_ref, *refs, n_pages):
    del tbl_ref
    page_refs = refs[:CMP_PAGES]
    w1_ref, pe_ref, w2_ref, out_ref, slabk_ref, slabv_ref = refs[CMP_PAGES:]
    j = pl.program_id(1)
    seq = n_pages * PAGE_SIZE
    slabs = (slabk_ref, slabv_ref)

    for k, page in enumerate(page_refs):
        rows = pl.ds(pl.multiple_of((j * CMP_PAGES + k) * PAGE_SIZE, PAGE_SIZE), PAGE_SIZE)
        for c, slab in enumerate(slabs):
            slab[rows, :] = page[c].T

    @pl.when(j == n_pages // CMP_PAGES - 1)
    def _():
        nb = seq // CMP_STRIDE
        for c, slab in enumerate(slabs):
            head = jnp.zeros((nb, LANES), F32)
            tail = jnp.zeros((nb, LANES), F32)
            for s in range(CMP_STRIDE):
                x = slab[pl.ds(s, nb, stride=CMP_STRIDE), :]
                head = head + _dot((x + pe_ref[c, s:s + 1, :]).astype(BF16), w1_ref[c, s])
                s2 = CMP_STRIDE + s
                tail = tail + _dot((x + pe_ref[c, s2:s2 + 1, :]).astype(BF16), w1_ref[c, s2])
            hid = head + pltpu.roll(tail, nb - 1, 0)
            out = _dot(jax.nn.gelu(hid).astype(BF16), w2_ref[c])
            out_ref[0, :, c * LANES:(c + 1) * LANES] = out.astype(BF16)


def _compress_weights(w1, w2, pe):
    eye = jnp.eye(N_KV, dtype=F32)
    bd = lambda m: jnp.einsum("...de,gk->...gdke", m, eye).reshape(m.shape[:-2] + (LANES, LANES))
    pe2 = jnp.concatenate([pe, pe], axis=-1)
    return bd(w1).astype(BF16), pe2, bd(w2).astype(BF16)


def _compress(pages, page_spec, table, w1bd, pe2, w2bd):
    s, p = table.shape
    assert CMP_LEN == 2 * CMP_STRIDE and p % CMP_PAGES == 0
    seq = p * PAGE_SIZE
    const = lambda a: pl.BlockSpec(a.shape, lambda b, j, t: (0,) * a.ndim)
    grid_spec = pltpu.PrefetchScalarGridSpec(
        num_scalar_prefetch=1,
        grid=(s, p // CMP_PAGES),
        in_specs=[page_spec(k) for k in range(CMP_PAGES)] + [const(w1bd), const(pe2), const(w2bd)],
        out_specs=pl.BlockSpec((1, seq // CMP_STRIDE, 2 * LANES), lambda b, j, t: (b, 0, 0)),
        scratch_shapes=[pltpu.VMEM((seq, LANES), F32), pltpu.VMEM((seq, LANES), F32)],
    )
    return pl.pallas_call(
        functools.partial(_compress_kernel, n_pages=p),
        grid_spec=grid_spec,
        out_shape=jax.ShapeDtypeStruct((s, seq // CMP_STRIDE, 2 * LANES), BF16),
        compiler_params=_cparams("parallel", "arbitrary"),
        name="compress",
    )(table, *([pages] * CMP_PAGES), w1bd, pe2, w2bd)


Q_TILE = 128
KEY_CHUNK = 512
N_BLK_PAD = 128
LOG2_SEL_BLOCK = int(math.log2(SEL_BLOCK))
WIN_KEYS = WINDOW + Q_TILE


def _cover_t(n_cmp_pad):
    n = jnp.arange(n_cmp_pad, dtype=jnp.int32)[None, :] * CMP_STRIDE
    s = jnp.arange(N_BLK_PAD, dtype=jnp.int32)[:, None] * SEL_BLOCK
    return ((n < s + SEL_BLOCK) & (n + CMP_LEN > s)).astype(BF16)


def _split3_nt(w, x):
    hi = x.astype(BF16)
    r1 = x - hi.astype(F32)
    mid = r1.astype(BF16)
    lo = (r1 - mid.astype(F32)).astype(BF16)
    return _nt_dot(w, hi) + _nt_dot(w, mid) + _nt_dot(w, lo)


def _take_top(s, n, out_ref=None):
    kidx = lax.broadcasted_iota(jnp.int32, s.shape, 0).astype(F32)
    rem = s
    rank = jnp.full(s.shape, float(n), F32)
    for a in range(n):
        mx = jnp.max(rem, axis=0, keepdims=True)
        first = jnp.min(jnp.where(rem == mx, kidx, float(s.shape[0])), axis=0, keepdims=True)
        taken = kidx == first
        if out_ref is not None:
            out_ref[a:a + 1, :] = mx
        rank = jnp.where(taken, float(a), rank)
        rem = jnp.where(taken, LOWEST, rem)
    return rank


def _topk_block_mask(imp_t, n_keep):
    return jnp.where(_take_top(imp_t, n_keep) < float(n_keep), 0.0, NEG)


def _cmp_branch(q, s_ref, cmp_ref, ocmp_ref, t_pos, rows):
    ncp = cmp_ref.shape[1]
    kc = cmp_ref[0, :, 0:LANES]
    vc = cmp_ref[0, :, LANES:2 * LANES]
    s_ref[:, 0:ncp] = _nt_dot(q, kc)
    n_idx = lax.broadcasted_iota(jnp.int32, (rows, ncp), 1)
    ok = (n_idx * CMP_STRIDE + (CMP_LEN - 1) <= t_pos) & (n_idx < ncp - 1)
    imps = []
    for g in range(N_KV):
        psum = jnp.zeros((rows, ncp), F32)
        for h in range(HPG):
            hh = g * HPG + h
            s = jnp.where(ok, s_ref[hh * rows:(hh + 1) * rows, 0:ncp], NEG)
            m = jnp.max(s, axis=-1, keepdims=True)
            e = jnp.where(ok, jnp.exp2(s - m), 0.0)
            l = jnp.sum(e, axis=-1, keepdims=True)
            p = e * jnp.where(l > 0.0, 1.0 / l, 0.0)
            ocmp_ref[hh] = _dot(p.astype(BF16), vc)
            psum = psum + p
        imps.append(psum)
    return imps


def _with_ones_rows(vt, g):
    row = lax.broadcasted_iota(jnp.int32, vt.shape, 0)
    own = (row >= g * HEAD_DIM) & (row < (g + 1) * HEAD_DIM)
    return jnp.where(own, vt, jnp.ones((), vt.dtype))


def _normalise(acc, g):
    lane = lax.broadcasted_iota(jnp.int32, acc.shape, 1)
    own = (lane >= g * HEAD_DIM) & (lane < (g + 1) * HEAD_DIM)
    return jnp.where(own, acc / pltpu.roll(acc, HEAD_DIM, 1), 0.0)


def _nsa_prompt_kernel(q_ref, gate_ref, kvs_ref, kvw_ref, cmp_ref, covt_ref, expand_ref, out_ref,
                       qs_ref, s_ref, m_ref, acc_ref, ocmp_ref):
    i = pl.program_id(1)
    t0 = i * Q_TILE
    for h in range(N_HEADS):
        qs_ref[h * Q_TILE:(h + 1) * Q_TILE, 0:LANES] = q_ref[0, :, h * LANES:(h + 1) * LANES]
    t_col = t0 + lax.broadcasted_iota(jnp.int32, (Q_TILE, 1), 0)

    psums = _cmp_branch(qs_ref[:, 0:LANES], s_ref, cmp_ref, ocmp_ref, t_col, Q_TILE)
    s_idx = lax.broadcasted_iota(jnp.int32, (N_BLK_PAD, Q_TILE), 0)
    t_row = t0 + lax.broadcasted_iota(jnp.int32, (N_BLK_PAD, Q_TILE), 1)
    cur = jnp.right_shift(t_row, LOG2_SEL_BLOCK)
    forced = (s_idx == 0) | (s_idx == cur) | (s_idx == cur - 1)
    for g in range(N_KV):
        imp_t = _split3_nt(covt_ref[...], psums[g])
        imp_t = jnp.where(forced, FORCE, imp_t)
        imp_t = jnp.where(s_idx * SEL_BLOCK <= t_row, imp_t, NEG)
        msel = _topk_block_mask(imp_t, N_SEL).T.astype(BF16)
        for h in range(HPG):
            hh = g * HPG + h
            qs_ref[hh * Q_TILE:(hh + 1) * Q_TILE, LANES:2 * LANES] = msel

    m_ref[...] = jnp.full(m_ref.shape, NEG, F32)
    acc_ref[...] = jnp.zeros(acc_ref.shape, F32)
    key_lane = lax.broadcasted_iota(jnp.int32, (Q_TILE, KEY_CHUNK), 1)

    def chunk(c, diagonal):
        k0 = pl.multiple_of(c * KEY_CHUNK, KEY_CHUNK)
        keys = pl.ds(k0, KEY_CHUNK)
        k_aug = jnp.concatenate([kvs_ref[0, 0:LANES, keys], expand_ref[:, keys]], axis=0)
        vt = kvs_ref[0, LANES:2 * LANES, keys]
        s_ref[:, 0:KEY_CHUNK] = _dot(qs_ref[...], k_aug)
        if diagonal:
            causal = jnp.where(k0 + key_lane <= t_col, 0.0, NEG)
        for g in range(N_KV):
            vt_g = _with_ones_rows(vt, g)
            for h in range(HPG):
                hh = g * HPG + h
                s = s_ref[hh * Q_TILE:(hh + 1) * Q_TILE, 0:KEY_CHUNK]
                if diagonal:
                    s = s + causal
                m_old = m_ref[hh]
                m_new = jnp.maximum(m_old, jnp.max(s, axis=-1, keepdims=True))
                p = jnp.exp2(s - m_new[:, 0:1])
                acc_ref[hh] = jnp.exp2(m_old - m_new) * acc_ref[hh] + _nt_dot(p.astype(BF16), vt_g)
                m_ref[hh] = m_new

    n_full = t0 // KEY_CHUNK

    def full_chunk(c, carry):
        chunk(c, False)
        return carry

    lax.fori_loop(0, n_full, full_chunk, 0)
    chunk(n_full, True)

    start = pl.multiple_of(jnp.maximum(t0 - WINDOW, 0), Q_TILE)
    kwt = kvw_ref[0, 0:LANES, pl.ds(start, WIN_KEYS)]
    vwt = kvw_ref[0, LANES:2 * LANES, pl.ds(start, WIN_KEYS)]
    s_ref[:, 0:WIN_KEYS] = _dot(qs_ref[:, 0:LANES], kwt)
    dist = t_col - (start + lax.broadcasted_iota(jnp.int32, (Q_TILE, WIN_KEYS), 1))
    bias_w = jnp.where((dist >= 0) & (dist <= WINDOW), 0.0, NEG)

    gates = jax.nn.sigmoid(gate_ref[0])
    for g in range(N_KV):
        vwt_g = _with_ones_rows(vwt, g)
        for h in range(HPG):
            hh = g * HPG + h
            s = s_ref[hh * Q_TILE:(hh + 1) * Q_TILE, 0:WIN_KEYS] + bias_w
            p = jnp.exp2(s - jnp.max(s, axis=-1, keepdims=True))
            o_win = _normalise(_nt_dot(p.astype(BF16), vwt_g), g)
            o_sel = _normalise(acc_ref[hh], g)
            o = (gates[:, 3 * hh:3 * hh + 1] * ocmp_ref[hh] + gates[:, 3 * hh + 1:3 * hh + 2] * o_sel
                 + gates[:, 3 * hh + 2:3 * hh + 3] * o_win)
            out_ref[0, :, hh * LANES:(hh + 1) * LANES] = o.astype(out_ref.dtype)


def _block_expand(t):
    key_blk = jnp.arange(t, dtype=jnp.int32)[None, :] // SEL_BLOCK
    return (jnp.arange(N_BLK_PAD, dtype=jnp.int32)[:, None] == key_blk).astype(BF16)


def _nsa_prompt(q, gates, kvs, kvw, cmp):
    b, t, _ = q.shape
    ncp = cmp.shape[1]
    tile = lambda w: pl.BlockSpec((1, Q_TILE, w), lambda bb, i: (bb, i, 0))
    whole = lambda r, w: pl.BlockSpec((1, r, w), lambda bb, i: (bb, 0, 0))
    const = lambda r, w: pl.BlockSpec((r, w), lambda bb, i: (0, 0))
    rows = N_HEADS * Q_TILE
    return pl.pallas_call(
        _nsa_prompt_kernel,
        grid=(b, t // Q_TILE),
        in_specs=[tile(QPAD), tile(LANES), whole(2 * LANES, t), whole(2 * LANES, t),
                  whole(ncp, 2 * LANES), const(N_BLK_PAD, ncp), const(N_BLK_PAD, t)],
        out_specs=tile(QPAD),
        out_shape=jax.ShapeDtypeStruct((b, t, QPAD), BF16),
        scratch_shapes=[
            pltpu.VMEM((rows, 2 * LANES), BF16),
            pltpu.VMEM((rows, max(WIN_KEYS, ncp)), F32),
            pltpu.VMEM((N_HEADS, Q_TILE, LANES), F32),
            pltpu.VMEM((N_HEADS, Q_TILE, LANES), F32),
            pltpu.VMEM((N_HEADS, Q_TILE, LANES), F32),
        ],
        compiler_params=_cparams("parallel", "arbitrary"),
        name="nsa_prompt",
    )(q, gates, kvs, kvw, cmp, _cover_t(ncp), _block_expand(t))


SEL_PAGES = 8


def _flash_update(s, vt, m_ref, l_ref, acc_ref):
    m_old = m_ref[...]
    m_new = jnp.maximum(m_old, jnp.max(s, axis=-1, keepdims=True))
    alpha = jnp.exp2(m_old - m_new)
    p = jnp.exp2(s - m_new[:, 0:1])
    l_ref[...] = alpha * l_ref[...] + jnp.sum(p, axis=-1, keepdims=True)
    acc_ref[...] = alpha * acc_ref[...] + _nt_dot(p.astype(BF16), vt)
    m_ref[...] = m_new


def _nsa_sample_kernel(tbl_ref, *refs, n_pages, tq):
    del tbl_ref
    q_ref, gate_ref = refs[:2]
    page_refs = refs[2:2 + SEL_PAGES]
    (cmp_ref, win_ref, newkv_ref, newwin_ref, covt_ref, out_ref, qs_ref, s_ref, msel_ref,
     m_ref, l_ref, acc_ref, ocmp_ref) = refs[2 + SEL_PAGES:]
    j = pl.program_id(1)
    past_len = n_pages * PAGE_SIZE
    rows = N_HEADS * tq
    step_keys = SEL_PAGES * PAGE_SIZE
    i_col = lax.broadcasted_iota(jnp.int32, (rows, 1), 0) & (tq - 1)

    @pl.when(j == 0)
    def _():
        for h in range(N_HEADS):
            qs_ref[h * tq:(h + 1) * tq, :] = q_ref[0, :, h * LANES:(h + 1) * LANES]
        t_col = past_len + lax.broadcasted_iota(jnp.int32, (tq, 1), 0)
        psums = _cmp_branch(qs_ref[...].astype(BF16), s_ref, cmp_ref, ocmp_ref, t_col, tq)
        s_idx = lax.broadcasted_iota(jnp.int32, (N_BLK_PAD, LANES), 0)
        t_row = past_len + lax.broadcasted_iota(jnp.int32, (N_BLK_PAD, LANES), 1)
        cur = jnp.right_shift(t_row, LOG2_SEL_BLOCK)
        forced = (s_idx == 0) | (s_idx == cur) | (s_idx == cur - 1)
        ncp = cmp_ref.shape[1]
        for g in range(N_KV):
            psum = jnp.concatenate([psums[g], jnp.zeros((LANES - tq, ncp), F32)], axis=0)
            imp_t = jnp.where(forced, FORCE, _split3_nt(covt_ref[...], psum))
            mask_t = _topk_block_mask(imp_t, N_SEL - 1)
            msel_ref[g] = mask_t.T.astype(BF16)
        m_ref[...] = jnp.full(m_ref.shape, NEG, F32)
        l_ref[...] = jnp.zeros(l_ref.shape, F32)
        acc_ref[...] = jnp.zeros(acc_ref.shape, F32)

    q = qs_ref[...].astype(BF16)
    kt = jnp.concatenate([page[0].astype(BF16) for page in page_refs], axis=1)
    vt = jnp.concatenate([page[1].astype(BF16) for page in page_refs], axis=1)
    blk_row = lax.broadcasted_iota(jnp.int32, (N_BLK_PAD, step_keys), 0)
    key_pos = j * step_keys + lax.broadcasted_iota(jnp.int32, (N_BLK_PAD, step_keys), 1)
    expand = jnp.where(blk_row == jnp.right_shift(key_pos, LOG2_SEL_BLOCK), 1.0, 0.0).astype(BF16)
    bias_g = [_dot(msel_ref[g], expand)[0:tq, :] for g in range(N_KV)]
    bias = jnp.concatenate([bias_g[hh // HPG] for hh in range(N_HEADS)], axis=0)
    _flash_update(_dot(q, kt) + bias, vt, m_ref, l_ref, acc_ref)

    @pl.when(j == n_pages // SEL_PAGES - 1)
    def _():
        j_lane = lax.broadcasted_iota(jnp.int32, (rows, PAGE_SIZE), 1)
        new_bias = jnp.where(j_lane <= i_col, 0.0, NEG)
        _flash_update(_dot(q, newkv_ref[0, 0:LANES, :]) + new_bias, newkv_ref[0, LANES:2 * LANES, :],
                      m_ref, l_ref, acc_ref)
        wb = win_ref.shape[2]
        dist = wb + i_col - lax.broadcasted_iota(jnp.int32, (rows, wb), 1)
        s1 = _dot(q, win_ref[0, 0:LANES, :].astype(BF16)) + jnp.where((dist >= 0) & (dist <= WINDOW), 0.0, NEG)
        s2 = _dot(q, newwin_ref[0, 0:LANES, :]) + new_bias
        m = jnp.maximum(jnp.max(s1, axis=-1, keepdims=True), jnp.max(s2, axis=-1, keepdims=True))
        p1 = jnp.exp2(s1 - m)
        p2 = jnp.exp2(s2 - m)
        l = jnp.sum(p1, axis=-1, keepdims=True) + jnp.sum(p2, axis=-1, keepdims=True)
        o_win = (_nt_dot(p1.astype(BF16), win_ref[0, LANES:2 * LANES, :].astype(BF16))
                 + _nt_dot(p2.astype(BF16), newwin_ref[0, LANES:2 * LANES, :])) / l
        o_sel = acc_ref[...] / l_ref[...]
        gates = jax.nn.sigmoid(gate_ref[0])
        for hh in range(N_HEADS):
            r = slice(hh * tq, (hh + 1) * tq)
            o = (gates[:, 3 * hh:3 * hh + 1] * ocmp_ref[hh] + gates[:, 3 * hh + 1:3 * hh + 2] * o_sel[r]
                 + gates[:, 3 * hh + 2:3 * hh + 3] * o_win[r])
            out_ref[0, :, hh * LANES:(hh + 1) * LANES] = o


def _nsa_sample(q, gates, pages, table, cmp, win, newkv, newwin):
    s, tq, _ = q.shape
    n_pages = table.shape[1]
    ncp = cmp.shape[1]
    wb = win.shape[2]
    assert (n_pages * PAGE_SIZE) // SEL_BLOCK == N_BLK_PAD and tq <= SEL_BLOCK and tq & (tq - 1) == 0
    assert n_pages % SEL_PAGES == 0
    rows = N_HEADS * tq
    per_seq = lambda r, w: pl.BlockSpec((1, r, w), lambda b, j, t: (b, 0, 0))
    page_spec = lambda k: pl.BlockSpec((None, 2, LANES, PAGE_SIZE),
                                       lambda b, j, t: (t[b, j * SEL_PAGES + k], 1, 0, 0))
    grid_spec = pltpu.PrefetchScalarGridSpec(
        num_scalar_prefetch=1,
        grid=(s, n_pages // SEL_PAGES),
        in_specs=[per_seq(tq, QPAD), per_seq(tq, LANES)] + [page_spec(k) for k in range(SEL_PAGES)] + [
            per_seq(ncp, 2 * LANES), per_seq(2 * LANES, wb),
            per_seq(2 * LANES, PAGE_SIZE), per_seq(2 * LANES, PAGE_SIZE),
            pl.BlockSpec((N_BLK_PAD, ncp), lambda b, j, t: (0, 0))],
        out_specs=per_seq(tq, QPAD),
        scratch_shapes=[
            pltpu.VMEM((rows, LANES), F32),
            pltpu.VMEM((rows, ncp), F32),
            pltpu.VMEM((N_KV, LANES, N_BLK_PAD), BF16),
            pltpu.VMEM((rows, LANES), F32),
            pltpu.VMEM((rows, LANES), F32),
            pltpu.VMEM((rows, LANES), F32),
            pltpu.VMEM((N_HEADS, tq, LANES), F32),
        ],
    )
    return pl.pallas_call(
        functools.partial(_nsa_sample_kernel, n_pages=n_pages, tq=tq),
        grid_spec=grid_spec,
        out_shape=jax.ShapeDtypeStruct((s, tq, QPAD), F32),
        compiler_params=_cparams("parallel", "arbitrary"),
        name="nsa_sample",
    )(table, q, gates, *([pages] * SEL_PAGES), cmp, win, newkv, newwin, _cover_t(ncp))


S5_LANE_BLOCK = 512


def _s5_kernel(u_ref, z_ref, h0_ref, bbd_ref, lam_ref, cbd_ref, d_ref, y_ref, hl_ref,
               bu_ref, hs_ref, h_ref):
    c = pl.program_id(1)
    steps = u_ref.shape[1]

    @pl.when(c == 0)
    def _():
        h_ref[...] = h0_ref[0]

    u = u_ref[0]
    bu_ref[...] = _dot(u.astype(BF16), bbd_ref[...])
    for blk in range(N_STATE // S5_LANE_BLOCK):
        re = slice(blk * S5_LANE_BLOCK, (blk + 1) * S5_LANE_BLOCK)
        im = slice(N_STATE + blk * S5_LANE_BLOCK, N_STATE + (blk + 1) * S5_LANE_BLOCK)
        lr = lam_ref[:, re]
        li = lam_ref[:, im]

        def step(t, carry, re=re, im=im, lr=lr, li=li):
            hr, hi = carry
            row = pl.ds(t, 1)
            nr = lr * hr - li * hi + bu_ref[row, re]
            ni = lr * hi + li * hr + bu_ref[row, im]
            hs_ref[row, re] = nr
            hs_ref[row, im] = ni
            return nr, ni

        hr, hi = lax.fori_loop(0, steps, step, (h_ref[:, re], h_ref[:, im]),
                               unroll=min(8, steps))
        h_ref[:, re] = hr
        h_ref[:, im] = hi
    y = _dot(hs_ref[...].astype(BF16), cbd_ref[...])
    y = y + d_ref[...] * u
    y_ref[0] = (jax.nn.gelu(y) * jax.nn.sigmoid(z_ref[0])).astype(y_ref.dtype)

    @pl.when(c == pl.num_programs(1) - 1)
    def _():
        hl_ref[0] = h_ref[...]


def _s5_weights(lam_re, lam_im, log_dt, b_re, b_im, c_re, c_im, d_skip):
    lam = lax.complex(lam_re.astype(F32), lam_im.astype(F32))
    dt = jnp.exp(log_dt.astype(F32))[:, None]
    lam_bar = jnp.exp(lam * dt)
    b_bar = ((lam_bar - 1.0) / lam)[..., None] * lax.complex(b_re.astype(F32), b_im.astype(F32))
    eye = jnp.eye(N_SSM_GROUPS, dtype=F32)
    def in_bd(b):
        return jnp.einsum("gph,gk->ghkp", b, eye).reshape(SSM_WIDTH, N_STATE)
    bbd = jnp.concatenate([in_bd(b_bar.real), in_bd(b_bar.imag)], axis=1)
    def out_bd(cm):
        return jnp.einsum("ghp,gk->gpkh", cm, eye).reshape(N_STATE, SSM_WIDTH)
    cbd = jnp.concatenate([out_bd(c_re.astype(F32)), -out_bd(c_im.astype(F32))], axis=0)
    lam_row = jnp.concatenate([lam_bar.real.reshape(1, N_STATE), lam_bar.imag.reshape(1, N_STATE)], axis=1)
    return bbd.astype(BF16), lam_row, cbd.astype(BF16), d_skip.astype(F32).reshape(1, SSM_WIDTH)


def _s5(u, z, h0, s5w, chunk, out_dtype):
    bbd, lam_row, cbd, d_row = s5w
    s, t, _ = u.shape
    const = lambda shape: pl.BlockSpec(shape, lambda b, c: (0,) * len(shape))
    return pl.pallas_call(
        _s5_kernel,
        grid=(s, t // chunk),
        in_specs=[
            pl.BlockSpec((1, chunk, SSM_WIDTH), lambda b, c: (b, c, 0)),
            pl.BlockSpec((1, chunk, SSM_WIDTH), lambda b, c: (b, c, 0)),
            pl.BlockSpec((1, 1, 2 * N_STATE), lambda b, c: (b, 0, 0)),
            const((SSM_WIDTH, 2 * N_STATE)), const((1, 2 * N_STATE)),
            const((2 * N_STATE, SSM_WIDTH)), const((1, SSM_WIDTH)),
        ],
        out_specs=[
            pl.BlockSpec((1, chunk, SSM_WIDTH), lambda b, c: (b, c, 0)),
            pl.BlockSpec((1, 1, 2 * N_STATE), lambda b, c: (b, 0, 0)),
        ],
        out_shape=[jax.ShapeDtypeStruct((s, t, SSM_WIDTH), out_dtype),
                   jax.ShapeDtypeStruct((s, 1, 2 * N_STATE), F32)],
        scratch_shapes=[pltpu.VMEM((chunk, 2 * N_STATE), F32),
                        pltpu.VMEM((chunk, 2 * N_STATE), F32),
                        pltpu.VMEM((1, 2 * N_STATE), F32)],
        compiler_params=_cparams("parallel", "arbitrary"),
        name="s5",
    )(u, z, h0, bbd, lam_row, cbd, d_row)


def _outproj_kernel(x_ref, a_ref, s_ref, wa_ref, ws_ref, y_ref):
    y = x_ref[...]
    y = y + _dot(a_ref[...].astype(BF16), wa_ref[...])
    y = y + _dot(s_ref[...].astype(BF16), ws_ref[...])
    y_ref[...] = y


def _outproj_weights(w_out):
    wa = w_out[:NSA_WIDTH].reshape(N_KV, HPG, 1, HEAD_DIM, D_MODEL)
    slot = jnp.eye(N_KV, dtype=F32).reshape(N_KV, 1, N_KV, 1, 1)
    return (wa * slot).reshape(QPAD, D_MODEL).astype(BF16), w_out[NSA_WIDTH:].astype(BF16)


def _outproj(x2d, a_out, s_out, wa, ws, tm):
    n = x2d.shape[0]
    row = lambda w: pl.BlockSpec((tm, w), lambda i: (i, 0))
    return pl.pallas_call(
        _outproj_kernel,
        grid=(n // tm,),
        in_specs=[row(D_MODEL), row(QPAD), row(SSM_WIDTH),
                  pl.BlockSpec((QPAD, D_MODEL), lambda i: (0, 0)),
                  pl.BlockSpec((SSM_WIDTH, D_MODEL), lambda i: (0, 0))],
        out_specs=row(D_MODEL),
        out_shape=jax.ShapeDtypeStruct((n, D_MODEL), F32),
        compiler_params=_cparams("parallel"),
        name="outproj",
    )(x2d, a_out, s_out, wa, ws)


PEER_HALF = PEER_HEADS * PEER_DK // 2
_CAND_COUNTS = tuple(PEER_TOPK // (a + 1) for a in range(PEER_TOPK))
N_CAND = sum(_CAND_COUNTS)
N_CAND_PAD = -(-N_CAND // 8) * 8


def _peer_route_kernel(y_ref, g_ref, wq_ref, k1_ref, k2_ref, xn_ref, c1_ref, e1_ref, r2_ref, e2_ref,
                       s1_ref, s2_ref, v1_ref, v2_ref, cand_ref):
    xn = _rmsnorm(y_ref[...], g_ref[...]).astype(BF16)
    xn_ref[...] = xn
    q = _dot(xn, wq_ref[...]).astype(BF16)
    s1_ref[...] = _nt_dot(k1_ref[...], q[:, :PEER_HALF])
    s2_ref[...] = _nt_dot(k2_ref[...], q[:, PEER_HALF:])
    t = y_ref.shape[0]
    cand_ref[N_CAND:N_CAND_PAD, :] = jnp.full((N_CAND_PAD - N_CAND, t), LOWEST, F32)
    for h in range(PEER_HEADS):
        rows = slice(h * N_KEYS, (h + 1) * N_KEYS)
        s1 = s1_ref[rows, :]
        s2 = s2_ref[rows, :]
        rank1 = _take_top(s1, PEER_TOPK, v1_ref)
        rank2 = _take_top(s2, PEER_TOPK, v2_ref)
        v1 = v1_ref[...]
        v2 = v2_ref[...]
        off = 0
        for a, nb in enumerate(_CAND_COUNTS):
            cand_ref[off:off + nb, :] = v1[a:a + 1, :] + v2[0:nb, :]
            off += nb
        cand = cand_ref[...]
        taken = jnp.where(_take_top(cand, PEER_TOPK) < float(PEER_TOPK), 1.0, 0.0)
        z = jnp.sum(taken * jnp.exp(cand - cand[0:1, :]), axis=0, keepdims=True)
        count = jnp.zeros((N_KEYS, t), F32)
        off = 0
        for a, nb in enumerate(_CAND_COUNTS):
            n_a = jnp.sum(taken[off:off + nb, :], axis=0, keepdims=True)
            count = jnp.where(rank1 == float(a), n_a, count)
            off += nb
        c1_ref[rows, :] = count
        e1_ref[rows, :] = jnp.exp(s1 - v1[0:1, :]) / z
        r2_ref[rows, :] = rank2.astype(BF16)
        e2_ref[rows, :] = jnp.exp(s2 - v2[0:1, :]).astype(BF16)


def _peer_weights(w_q, sub_k1, sub_k2):
    wq = w_q.reshape(D_MODEL, PEER_HEADS, 2, PEER_DK // 2).transpose(0, 2, 1, 3).reshape(D_MODEL, 2 * PEER_HALF)
    eye = jnp.eye(PEER_HEADS, dtype=F32)
    bd = lambda k: jnp.einsum("hkd,hj->hkjd", k, eye).reshape(PEER_HEADS * N_KEYS, PEER_HALF)
    return wq.astype(BF16), bd(sub_k1).astype(BF16), bd(sub_k2).astype(BF16)


def _peer_route(y2d, norm_g, wq, k1bd, k2bd, tm):
    n = y2d.shape[0]
    hk = PEER_HEADS * N_KEYS
    const = lambda a: pl.BlockSpec(a.shape, lambda i: (0, 0))
    col = lambda r: pl.BlockSpec((r, tm), lambda i: (0, i))
    tshape = lambda dt: jax.ShapeDtypeStruct((hk, n), dt)
    return pl.pallas_call(
        _peer_route_kernel,
        grid=(n // tm,),
        in_specs=[pl.BlockSpec((tm, D_MODEL), lambda i: (i, 0)),
                  pl.BlockSpec((1, D_MODEL), lambda i: (0, 0)), const(wq), const(k1bd), const(k2bd)],
        out_specs=[pl.BlockSpec((tm, D_MODEL), lambda i: (i, 0)), col(hk), col(hk), col(hk), col(hk)],
        out_shape=[jax.ShapeDtypeStruct((n, D_MODEL), BF16),
                   tshape(F32),
                   tshape(F32),
                   tshape(BF16),
                   tshape(BF16)],
        scratch_shapes=[pltpu.VMEM((hk, tm), F32), pltpu.VMEM((hk, tm), F32),
                        pltpu.VMEM((PEER_TOPK, tm), F32), pltpu.VMEM((PEER_TOPK, tm), F32),
                        pltpu.VMEM((N_CAND_PAD, tm), F32)],
        compiler_params=_cparams("parallel"),
        name="peer_route",
    )(y2d, norm_g.reshape(1, D_MODEL), wq, k1bd, k2bd)


EXPERT_BLOCK = 1024
I1_PER_BLOCK = EXPERT_BLOCK // N_KEYS
N_EXPERT_BLOCKS = N_EXPERTS // EXPERT_BLOCK


def _peer_dense_kernel(y_ref, xn_ref, u_ref, vt_ref, c1_ref, e1_ref, r2_ref, e2_ref,
                       gf_ref, out_ref, acc_ref, act_ref, ga_ref):
    j = pl.program_id(1)
    t = xn_ref.shape[0]

    @pl.when(j == 0)
    def _():
        acc_ref[...] = jnp.zeros(acc_ref.shape, F32)
        act_ref[...] = jnp.zeros(act_ref.shape, BF16)
        ga_ref[...] = jnp.zeros(ga_ref.shape, BF16)

    def stages(cur, prev):
        jb = jnp.clip(j - 1, 0, N_EXPERT_BLOCKS - 1)
        for ii in range(I1_PER_BLOCK):
            erows = slice(ii * N_KEYS, (ii + 1) * N_KEYS)
            for tc in range(t // LANES):
                cols = slice(tc * LANES, (tc + 1) * LANES)
                gate = jnp.zeros((N_KEYS, LANES), BF16)
                for h in range(PEER_HEADS):
                    grp = pl.ds(pl.multiple_of(h * N_KEYS + jb * I1_PER_BLOCK, I1_PER_BLOCK), I1_PER_BLOCK)
                    krows = slice(h * N_KEYS, (h + 1) * N_KEYS)
                    count = c1_ref[grp, cols][ii:ii + 1, :].astype(BF16)
                    e1 = e1_ref[grp, cols][ii:ii + 1, :].astype(BF16)
                    gate = gate + jnp.where(r2_ref[krows, cols] < count, e2_ref[krows, cols] * e1,
                                            jnp.zeros((), BF16))
                ga_ref[prev, erows, cols] = gate * act_ref[prev, erows, cols]

        acc_ref[...] += _dot(vt_ref[...], ga_ref[cur])

        act_ref[cur] = jax.nn.gelu(_nt_dot(u_ref[...], xn_ref[...])).astype(BF16)

    for parity in range(2):
        pl.when(j % 2 == parity)(functools.partial(stages, parity, 1 - parity))

    @pl.when(j == pl.num_programs(1) - 1)
    def _():
        y = y_ref[...] + acc_ref[...].T
        out_ref[...] = _rmsnorm(y, gf_ref[...])


def _peer_dense(y2d, xn, u_bf, vt_bf, c1, e1, r2, e2, norm_f, tm):
    n = y2d.shape[0]
    hk = PEER_HEADS * N_KEYS
    tok = lambda w: pl.BlockSpec((tm, w), lambda i, j: (i, 0))
    col = lambda r: pl.BlockSpec((r, tm), lambda i, j: (0, i))
    last = N_EXPERT_BLOCKS - 1
    return pl.pallas_call(
        _peer_dense_kernel,
        grid=(n // tm, N_EXPERT_BLOCKS + 2),
        in_specs=[tok(D_MODEL), tok(D_MODEL),
                  pl.BlockSpec((EXPERT_BLOCK, D_MODEL), lambda i, j: (jnp.minimum(j, last), 0)),
                  pl.BlockSpec((D_MODEL, EXPERT_BLOCK), lambda i, j: (0, jnp.maximum(j - 2, 0))),
                  col(hk), col(hk), col(hk), col(hk),
                  pl.BlockSpec((1, D_MODEL), lambda i, j: (0, 0))],
        out_specs=tok(D_MODEL),
        out_shape=jax.ShapeDtypeStruct((n, D_MODEL), F32),
        scratch_shapes=[pltpu.VMEM((D_MODEL, tm), F32),
                        pltpu.VMEM((2, EXPERT_BLOCK, tm), BF16),
                        pltpu.VMEM((2, EXPERT_BLOCK, tm), BF16)],
        compiler_params=_cparams("parallel", "arbitrary"),
        name="peer_dense",
    )(y2d, xn, u_bf, vt_bf, c1, e1, r2, e2, norm_f.reshape(1, D_MODEL))


PROJ_TILE = 512
PEER_TILE = 256
S5_CHUNK = 256


def _state_to_rows(st):
    s = st.shape[0]
    return jnp.concatenate([st[..., 0].reshape(s, 1, N_STATE), st[..., 1].reshape(s, 1, N_STATE)], axis=-1)


def _rows_to_state(h):
    s = h.shape[0]
    shape = (s, N_SSM_GROUPS, SSM_STATE)
    return jnp.stack([h[:, 0, :N_STATE].reshape(shape), h[:, 0, N_STATE:].reshape(shape)], axis=-1)


def _feat_to_tokens(a, n_types):
    s, _, t = a.shape
    return a.reshape(s, n_types, N_KV, HEAD_DIM, t).transpose(0, 4, 1, 2, 3)


def _peer_block(y2d, norm_g, peer_w, norm_f, tm):
    wq, k1bd, k2bd, u_bf, vt_bf = peer_w
    xn, c1, e1, r2, e2 = _peer_route(y2d, norm_g, wq, k1bd, k2bd, tm)
    return _peer_dense(y2d, xn, u_bf, vt_bf, c1, e1, r2, e2, norm_f, tm)


def kernel(x_prompt, x_sample, cache_kv, cache_win, state_ssm, page_table, norm_mix, w_in, w_cmp1, w_cmp2, pe_cmp, lam_re, lam_im, log_dt, b_re, b_im, c_re, c_im, d_skip, w_out, norm_ffn, w_q_peer, sub_k1, sub_k2, u_tab, v_tab, norm_final):
    b, t, d = x_prompt.shape
    db, ts, _ = x_sample.shape
    assert w_in.shape[0] == DEPTH == 1 and d == D_MODEL
    l = 0
    n_pool = cache_kv.shape[1]
    wb = cache_win.shape[2]

    w_tok, w_feat = _proj_weights(w_in[l])
    cmp_w = _compress_weights(w_cmp1[l], w_cmp2[l], pe_cmp[l])
    s5_w = _s5_weights(lam_re[l], lam_im[l], log_dt[l], b_re[l], b_im[l], c_re[l], c_im[l], d_skip[l])
    wa, ws = _outproj_weights(w_out[l])
    peer_w = _peer_weights(w_q_peer[l], sub_k1[l], sub_k2[l]) + (
        u_tab[l].astype(BF16), v_tab[l].T.astype(BF16))

    q, gate, u, z, kvsel_t, ksvs_t, kvwin_t, kwvw_t = _project(x_prompt, norm_mix[l], w_tok, w_feat, PROJ_TILE, BF16)
    n_pg = t // PAGE_SIZE
    table_p = jnp.broadcast_to(jnp.arange(n_pg, dtype=jnp.int32), (b, n_pg))
    prompt_page = lambda k: pl.BlockSpec((None, 2, LANES, PAGE_SIZE),
                                         lambda bb, j, tb: (bb, 0, 0, tb[bb, j * CMP_PAGES + k]))
    cmp_p = _compress(kvsel_t.reshape(b, 4, LANES, t), prompt_page, table_p, *cmp_w)
    a_p = _nsa_prompt(q.reshape(b, t, QPAD), gate.reshape(b, t, LANES), ksvs_t, kwvw_t, cmp_p)
    s_p, h_p = _s5(u.reshape(b, t, SSM_WIDTH), z.reshape(b, t, SSM_WIDTH),
                   jnp.zeros((b, 1, 2 * N_STATE), F32), s5_w, S5_CHUNK, BF16)
    y1p = _outproj(x_prompt.reshape(b * t, d), a_p.reshape(b * t, QPAD), s_p.reshape(b * t, SSM_WIDTH),
                   wa, ws, PROJ_TILE)
    y_prompt = _peer_block(y1p, norm_ffn[l], peer_w, norm_final, PEER_TILE).reshape(b, t, d)
    kv_prompt = _feat_to_tokens(kvsel_t, 4)
    win_prompt = _feat_to_tokens(kvwin_t[:, :, t - min(WINDOW, t):], 2)
    ssm_prompt = _rows_to_state(h_p)

    qs, gate_s, u_s, z_s, kvsel_st, ksvs_st, kvwin_st, kwvw_st = _project(
        x_sample.reshape(1, db * ts, d), norm_mix[l], w_tok, w_feat, db * ts, F32)
    pages_s = cache_kv[l].transpose(0, 2, 3, 4, 1).reshape(n_pool, 4, LANES, PAGE_SIZE)
    sample_page = lambda k: pl.BlockSpec((None, 2, LANES, PAGE_SIZE),
                                         lambda bb, j, tb: (tb[bb, j * CMP_PAGES + k], 0, 0, 0))
    cmp_s = _compress(pages_s, sample_page, page_table, *cmp_w)
    new_page = lambda a: jnp.pad(a[0].reshape(2 * LANES, db, ts).transpose(1, 0, 2),
                                 ((0, 0), (0, 0), (0, PAGE_SIZE - ts)))
    win_t = cache_win[l].transpose(0, 2, 3, 4, 1).reshape(db, 2 * LANES, wb)
    a_s = _nsa_sample(qs.reshape(db, ts, QPAD), gate_s.reshape(db, ts, LANES), pages_s, page_table, cmp_s,
                      win_t, new_page(ksvs_st), new_page(kwvw_st))
    s_s, h_s = _s5(u_s.reshape(db, ts, SSM_WIDTH), z_s.reshape(db, ts, SSM_WIDTH),
                   _state_to_rows(state_ssm[l].astype(F32)), s5_w, ts, F32)
    y1s = _outproj(x_sample.reshape(db * ts, d), a_s.reshape(db * ts, QPAD), s_s.reshape(db * ts, SSM_WIDTH),
                   wa, ws, db * ts)
    y_sample = _peer_block(y1s, norm_ffn[l], peer_w, norm_final, PEER_TILE).reshape(db, ts, d)
    per_tok = lambda a, n_types: a[0].reshape(n_types, N_KV, HEAD_DIM, db, ts).transpose(3, 4, 0, 1, 2)
    kv_sample = per_tok(kvsel_st, 4)
    win_new = per_tok(kvwin_st, 2).astype(cache_win.dtype)
    win_sample = jnp.concatenate([cache_win[l], win_new], axis=1)[:, ts:]
    ssm_sample = _rows_to_state(h_s)

    return (y_prompt, y_sample, kv_prompt[None], kv_sample[None], win_prompt[None], win_sample[None],
            ssm_prompt[None], ssm_sample[None])
```

```python
import functools
import math

import jax
import jax.numpy as jnp
from jax import lax
from jax.experimental import pallas as pl
from jax.experimental.pallas import tpu as pltpu

D_MODEL = 1024
DEPTH = 1
PAGE_SIZE = 128
N_HEADS = 8
N_KV = 2
HEAD_DIM = 64
HPG = N_HEADS // N_KV
CMP_LEN = 32
CMP_STRIDE = 16
SEL_BLOCK = 64
N_SEL = 16
WINDOW = 512
SSM_WIDTH = 512
SSM_GROUP = 16
N_SSM_GROUPS = SSM_WIDTH // SSM_GROUP
SSM_STATE = 64
N_KEYS = 128
N_EXPERTS = N_KEYS * N_KEYS
PEER_HEADS = 8
PEER_DK = 128
PEER_TOPK = 16
NSA_WIDTH = N_HEADS * HEAD_DIM
KV_WIDTH = N_KV * HEAD_DIM
GATE_WIDTH = 3 * N_HEADS
EPS = 1e-6
NEG = -1e30
FORCE = 1e4
LOG2_E = 1.0 / math.log(2.0)
LOWEST = -3.0e38

LANES = 128
BF16_ROWS = 16
VMEM_LIMIT = 56 * 1024 * 1024

N_STATE = N_SSM_GROUPS * SSM_STATE
QPAD = N_HEADS * LANES
F32 = jnp.float32
BF16 = jnp.bfloat16

assert KV_WIDTH == LANES and PAGE_SIZE == LANES


def _cparams(*sem):
    return pltpu.CompilerParams(dimension_semantics=sem, vmem_limit_bytes=VMEM_LIMIT)


def _nt_dot(a, b):
    return lax.dot_general(a, b, (((1,), (1,)), ((), ())), preferred_element_type=F32)


def _dot(a, b):
    return jnp.dot(a, b, preferred_element_type=F32)


def _rmsnorm(x, g):
    return x * lax.rsqrt(jnp.mean(x * x, axis=-1, keepdims=True) + EPS) * g


_GELU_C1 = 2.0 * math.sqrt(2.0 / math.pi) * LOG2_E
_GELU_C3 = _GELU_C1 * 0.044715


def _gelu_tanh(x):
    return x / (1.0 + jnp.exp2(x * (-_GELU_C1 - _GELU_C3 * (x * x))))


_PROJ_COLS = (("q", QPAD), ("gate", LANES), ("u", SSM_WIDTH), ("z", SSM_WIDTH))
KVT_ROWS = 6 * KV_WIDTH


def _proj_kernel(x_ref, g_ref, w_ref, wt_ref, q_ref, gate_ref, u_ref, z_ref,
                 kvsel_ref, ksvs_ref, kvwin_ref, kwvw_ref):
    xn = _rmsnorm(x_ref[...], g_ref[...]).astype(BF16)
    off = 0
    for (name, width), ref in zip(_PROJ_COLS, (q_ref, gate_ref, u_ref, z_ref)):
        ref[...] = _dot(xn, w_ref[:, off:off + width]).astype(ref.dtype)
        off += width
    kvt = _nt_dot(wt_ref[...], xn)
    kvsel_ref[0] = kvt[0:4 * KV_WIDTH]
    ksvs_ref[0] = kvt[2 * KV_WIDTH:4 * KV_WIDTH].astype(BF16)
    kvwin_ref[0] = kvt[4 * KV_WIDTH:]
    kwvw_ref[0] = kvt[4 * KV_WIDTH:].astype(BF16)


def _proj_weights(w_in):
    c0 = NSA_WIDTH
    c1 = c0 + 6 * KV_WIDTH
    c2 = c1 + GATE_WIDTH
    wq = w_in[:, :c0].reshape(D_MODEL, N_KV, HPG, 1, HEAD_DIM) * (HEAD_DIM ** -0.5 * LOG2_E)
    slot = jnp.eye(N_KV, dtype=F32).reshape(1, N_KV, 1, N_KV, 1)
    wq_pad = (wq * slot).reshape(D_MODEL, QPAD)
    wg = jnp.pad(w_in[:, c1:c2], ((0, 0), (0, LANES - GATE_WIDTH)))
    w = jnp.concatenate([wq_pad, wg, w_in[:, c2:]], axis=1)
    return w.astype(BF16), w_in[:, c0:c1].T.astype(BF16)


def _project(x3d, norm_g, w_tok, w_feat, tm, q_dtype):
    s, t, _ = x3d.shape
    nt = t // tm
    n = s * t
    row = lambda w: pl.BlockSpec((tm, w), lambda b, i: (b * nt + i, 0))
    feat = lambda r: pl.BlockSpec((1, r, tm), lambda b, i: (b, 0, i))
    const = lambda a: pl.BlockSpec(a.shape, lambda b, i: (0, 0))
    shapes = [
        jax.ShapeDtypeStruct((n, QPAD), q_dtype),
        jax.ShapeDtypeStruct((n, LANES), F32),
        jax.ShapeDtypeStruct((n, SSM_WIDTH), F32),
        jax.ShapeDtypeStruct((n, SSM_WIDTH), F32),
        jax.ShapeDtypeStruct((s, 4 * KV_WIDTH, t), F32),
        jax.ShapeDtypeStruct((s, 2 * KV_WIDTH, t), BF16),
        jax.ShapeDtypeStruct((s, 2 * KV_WIDTH, t), F32),
        jax.ShapeDtypeStruct((s, 2 * KV_WIDTH, t), BF16),
    ]
    g2 = norm_g.reshape(1, D_MODEL)
    return pl.pallas_call(
        _proj_kernel,
        grid=(s, nt),
        in_specs=[row(D_MODEL), const(g2), const(w_tok), const(w_feat)],
        out_specs=[row(sh.shape[1]) for sh in shapes[:4]] + [feat(sh.shape[1]) for sh in shapes[4:]],
        out_shape=shapes,
        compiler_params=_cparams("parallel", "parallel"),
        name="proj",
    )(x3d.reshape(n, D_MODEL), g2, w_tok, w_feat)


CMP_PAGES = 8


def _compress_kernel(tbl_ref, *refs, n_pages):
    del tbl_ref
    page_refs = refs[:CMP_PAGES]
    w1_ref, pe_ref, w2_ref, out_ref, slabk_ref, slabv_ref = refs[CMP_PAGES:]
    j = pl.program_id(1)
    seq = n_pages * PAGE_SIZE
    slabs = (slabk_ref, slabv_ref)

    for k, page in enumerate(page_refs):
        rows = pl.ds(pl.multiple_of((j * CMP_PAGES + k) * PAGE_SIZE, PAGE_SIZE), PAGE_SIZE)
        for c, slab in enumerate(slabs):
            slab[rows, :] = page[c].T

    @pl.when(j == n_pages // CMP_PAGES - 1)
    def _():
        nb = seq // CMP_STRIDE
        for c, slab in enumerate(slabs):
            head = jnp.zeros((nb, LANES), F32)
            tail = jnp.zeros((nb, LANES), F32)
            for s in range(CMP_STRIDE):
                x = slab[pl.ds(s, nb, stride=CMP_STRIDE), :]
                head = head + _dot((x + pe_ref[c, s:s + 1, :]).astype(BF16), w1_ref[c, s])
                s2 = CMP_STRIDE + s
                tail = tail + _dot((x + pe_ref[c, s2:s2 + 1, :]).astype(BF16), w1_ref[c, s2])
            hid = head + pltpu.roll(tail, nb - 1, 0)
            out = _dot(jax.nn.gelu(hid).astype(BF16), w2_ref[c])
            out_ref[0, :, c * LANES:(c + 1) * LANES] = out.astype(BF16)


def _compress_weights(w1, w2, pe):
    eye = jnp.eye(N_KV, dtype=F32)
    bd = lambda m: jnp.einsum("...de,gk->...gdke", m, eye).reshape(m.shape[:-2] + (LANES, LANES))
    pe2 = jnp.concatenate([pe, pe], axis=-1)
    return bd(w1).astype(BF16), pe2, bd(w2).astype(BF16)


def _compress(pages, page_spec, table, w1bd, pe2, w2bd):
    s, p = table.shape
    assert CMP_LEN == 2 * CMP_STRIDE and p % CMP_PAGES == 0
    seq = p * PAGE_SIZE
    const = lambda a: pl.BlockSpec(a.shape, lambda b, j, t: (0,) * a.ndim)
    grid_spec = pltpu.PrefetchScalarGridSpec(
        num_scalar_prefetch=1,
        grid=(s, p // CMP_PAGES),
        in_specs=[page_spec(k) for k in range(CMP_PAGES)] + [const(w1bd), const(pe2), const(w2bd)],
        out_specs=pl.BlockSpec((1, seq // CMP_STRIDE, 2 * LANES), lambda b, j, t: (b, 0, 0)),
        scratch_shapes=[pltpu.VMEM((seq, LANES), F32), pltpu.VMEM((seq, LANES), F32)],
    )
    return pl.pallas_call(
        functools.partial(_compress_kernel, n_pages=p),
        grid_spec=grid_spec,
        out_shape=jax.ShapeDtypeStruct((s, seq // CMP_STRIDE, 2 * LANES), BF16),
        compiler_params=_cparams("parallel", "arbitrary"),
        name="compress",
    )(table, *([pages] * CMP_PAGES), w1bd, pe2, w2bd)


Q_TILE = 128
KEY_CHUNK = 512
N_BLK_PAD = 128
LOG2_SEL_BLOCK = int(math.log2(SEL_BLOCK))
WIN_KEYS = WINDOW + Q_TILE


def _cover_t(n_cmp_pad):
    n = jnp.arange(n_cmp_pad, dtype=jnp.int32)[None, :] * CMP_STRIDE
    s = jnp.arange(N_BLK_PAD, dtype=jnp.int32)[:, None] * SEL_BLOCK
    return ((n < s + SEL_BLOCK) & (n + CMP_LEN > s)).astype(BF16)


def _split3_nt(w, x):
    hi = x.astype(BF16)
    r1 = x - hi.astype(F32)
    mid = r1.astype(BF16)
    lo = (r1 - mid.astype(F32)).astype(BF16)
    return _nt_dot(w, hi) + _nt_dot(w, mid) + _nt_dot(w, lo)


def _take_top(s, n, out_ref=None):
    kidx = lax.broadcasted_iota(jnp.int32, s.shape, 0).astype(F32)
    rem = s
    rank = jnp.full(s.shape, float(n), F32)
    for a in range(n):
        mx = jnp.max(rem, axis=0, keepdims=True)
        first = jnp.min(jnp.where(rem == mx, kidx, float(s.shape[0])), axis=0, keepdims=True)
        taken = kidx == first
        if out_ref is not None:
            out_ref[a:a + 1, :] = mx
        rank = jnp.where(taken, float(a), rank)
        rem = jnp.where(taken, LOWEST, rem)
    return rank


def _topk_block_mask(imp_t, n_keep):
    return jnp.where(_take_top(imp_t, n_keep) < float(n_keep), 0.0, NEG)


def _cmp_branch(q, s_ref, cmp_ref, ocmp_ref, t_pos, rows):
    ncp = cmp_ref.shape[1]
    kc = cmp_ref[0, :, 0:LANES]
    vc = cmp_ref[0, :, LANES:2 * LANES]
    s_ref[:, 0:ncp] = _nt_dot(q, kc)
    n_idx = lax.broadcasted_iota(jnp.int32, (rows, ncp), 1)
    ok = (n_idx * CMP_STRIDE + (CMP_LEN - 1) <= t_pos) & (n_idx < ncp - 1)
    imps = []
    for g in range(N_KV):
        psum = jnp.zeros((rows, ncp), F32)
        for h in range(HPG):
            hh = g * HPG + h
            s = jnp.where(ok, s_ref[hh * rows:(hh + 1) * rows, 0:ncp], NEG)
            m = jnp.max(s, axis=-1, keepdims=True)
            e = jnp.where(ok, jnp.exp2(s - m), 0.0)
            l = jnp.sum(e, axis=-1, keepdims=True)
            p = e * jnp.where(l > 0.0, 1.0 / l, 0.0)
            ocmp_ref[hh] = _dot(p.astype(BF16), vc)
            psum = psum + p
        imps.append(psum)
    return imps


def _with_ones_rows(vt, g):
    row = lax.broadcasted_iota(jnp.int32, vt.shape, 0)
    own = (row >= g * HEAD_DIM) & (row < (g + 1) * HEAD_DIM)
    return jnp.where(own, vt, jnp.ones((), vt.dtype))


def _normalise(acc, g):
    lane = lax.broadcasted_iota(jnp.int32, acc.shape, 1)
    own = (lane >= g * HEAD_DIM) & (lane < (g + 1) * HEAD_DIM)
    return jnp.where(own, acc / pltpu.roll(acc, HEAD_DIM, 1), 0.0)


def _nsa_prompt_kernel(q_ref, gate_ref, kvs_ref, kvw_ref, cmp_ref, covt_ref, expand_ref, out_ref,
                       qs_ref, s_ref, m_ref, acc_ref, ocmp_ref):
    i = pl.program_id(1)
    t0 = i * Q_TILE
    for h in range(N_HEADS):
        qs_ref[h * Q_TILE:(h + 1) * Q_TILE, 0:LANES] = q_ref[0, :, h * LANES:(h + 1) * LANES]
    t_col = t0 + lax.broadcasted_iota(jnp.int32, (Q_TILE, 1), 0)

    psums = _cmp_branch(qs_ref[:, 0:LANES], s_ref, cmp_ref, ocmp_ref, t_col, Q_TILE)
    s_idx = lax.broadcasted_iota(jnp.int32, (N_BLK_PAD, Q_TILE), 0)
    t_row = t0 + lax.broadcasted_iota(jnp.int32, (N_BLK_PAD, Q_TILE), 1)
    cur = jnp.right_shift(t_row, LOG2_SEL_BLOCK)
    forced = (s_idx == 0) | (s_idx == cur) | (s_idx == cur - 1)
    for g in range(N_KV):
        imp_t = _split3_nt(covt_ref[...], psums[g])
        imp_t = jnp.where(forced, FORCE, imp_t)
        imp_t = jnp.where(s_idx * SEL_BLOCK <= t_row, imp_t, NEG)
        msel = _topk_block_mask(imp_t, N_SEL).T.astype(BF16)
        for h in range(HPG):
            hh = g * HPG + h
            qs_ref[hh * Q_TILE:(hh + 1) * Q_TILE, LANES:2 * LANES] = msel

    m_ref[...] = jnp.full(m_ref.shape, NEG, F32)
    acc_ref[...] = jnp.zeros(acc_ref.shape, F32)
    key_lane = lax.broadcasted_iota(jnp.int32, (Q_TILE, KEY_CHUNK), 1)

    def chunk(c, diagonal):
        k0 = pl.multiple_of(c * KEY_CHUNK, KEY_CHUNK)
        keys = pl.ds(k0, KEY_CHUNK)
        k_aug = jnp.concatenate([kvs_ref[0, 0:LANES, keys], expand_ref[:, keys]], axis=0)
        vt = kvs_ref[0, LANES:2 * LANES, keys]
        s_ref[:, 0:KEY_CHUNK] = _dot(qs_ref[...], k_aug)
        if diagonal:
            causal = jnp.where(k0 + key_lane <= t_col, 0.0, NEG)
        for g in range(N_KV):
            vt_g = _with_ones_rows(vt, g)
            for h in range(HPG):
                hh = g * HPG + h
                s = s_ref[hh * Q_TILE:(hh + 1) * Q_TILE, 0:KEY_CHUNK]
                if diagonal:
                    s = s + causal
                m_old = m_ref[hh]
                m_new = jnp.maximum(m_old, jnp.max(s, axis=-1, keepdims=True))
                p = jnp.exp2(s - m_new[:, 0:1])
                acc_ref[hh] = jnp.exp2(m_old - m_new) * acc_ref[hh] + _nt_dot(p.astype(BF16), vt_g)
                m_ref[hh] = m_new

    n_full = t0 // KEY_CHUNK

    def full_chunk(c, carry):
        chunk(c, False)
        return carry

    lax.fori_loop(0, n_full, full_chunk, 0)
    chunk(n_full, True)

    start = pl.multiple_of(jnp.maximum(t0 - WINDOW, 0), Q_TILE)
    kwt = kvw_ref[0, 0:LANES, pl.ds(start, WIN_KEYS)]
    vwt = kvw_ref[0, LANES:2 * LANES, pl.ds(start, WIN_KEYS)]
    s_ref[:, 0:WIN_KEYS] = _dot(qs_ref[:, 0:LANES], kwt)
    dist = t_col - (start + lax.broadcasted_iota(jnp.int32, (Q_TILE, WIN_KEYS), 1))
    bias_w = jnp.where((dist >= 0) & (dist <= WINDOW), 0.0, NEG)

    gates = jax.nn.sigmoid(gate_ref[0])
    for g in range(N_KV):
        vwt_g = _with_ones_rows(vwt, g)
        for h in range(HPG):
            hh = g * HPG + h
            s = s_ref[hh * Q_TILE:(hh + 1) * Q_TILE, 0:WIN_KEYS] + bias_w
            p = jnp.exp2(s - jnp.max(s, axis=-1, keepdims=True))
            o_win = _normalise(_nt_dot(p.astype(BF16), vwt_g), g)
            o_sel = _normalise(acc_ref[hh], g)
            o = (gates[:, 3 * hh:3 * hh + 1] * ocmp_ref[hh] + gates[:, 3 * hh + 1:3 * hh + 2] * o_sel
                 + gates[:, 3 * hh + 2:3 * hh + 3] * o_win)
            out_ref[0, :, hh * LANES:(hh + 1) * LANES] = o.astype(out_ref.dtype)


def _block_expand(t):
    key_blk = jnp.arange(t, dtype=jnp.int32)[None, :] // SEL_BLOCK
    return (jnp.arange(N_BLK_PAD, dtype=jnp.int32)[:, None] == key_blk).astype(BF16)


def _nsa_prompt(q, gates, kvs, kvw, cmp):
    b, t, _ = q.shape
    ncp = cmp.shape[1]
    tile = lambda w: pl.BlockSpec((1, Q_TILE, w), lambda bb, i: (bb, i, 0))
    whole = lambda r, w: pl.BlockSpec((1, r, w), lambda bb, i: (bb, 0, 0))
    const = lambda r, w: pl.BlockSpec((r, w), lambda bb, i: (0, 0))
    rows = N_HEADS * Q_TILE
    return pl.pallas_call(
        _nsa_prompt_kernel,
        grid=(b, t // Q_TILE),
        in_specs=[tile(QPAD), tile(LANES), whole(2 * LANES, t), whole(2 * LANES, t),
                  whole(ncp, 2 * LANES), const(N_BLK_PAD, ncp), const(N_BLK_PAD, t)],
        out_specs=tile(QPAD),
        out_shape=jax.ShapeDtypeStruct((b, t, QPAD), BF16),
        scratch_shapes=[
            pltpu.VMEM((rows, 2 * LANES), BF16),
            pltpu.VMEM((rows, max(WIN_KEYS, ncp)), F32),
            pltpu.VMEM((N_HEADS, Q_TILE, LANES), F32),
            pltpu.VMEM((N_HEADS, Q_TILE, LANES), F32),
            pltpu.VMEM((N_HEADS, Q_TILE, LANES), F32),
        ],
        compiler_params=_cparams("parallel", "arbitrary"),
        name="nsa_prompt",
    )(q, gates, kvs, kvw, cmp, _cover_t(ncp), _block_expand(t))


SEL_PAGES = 8


def _flash_update(s, vt, m_ref, l_ref, acc_ref):
    m_old = m_ref[...]
    m_new = jnp.maximum(m_old, jnp.max(s, axis=-1, keepdims=True))
    alpha = jnp.exp2(m_old - m_new)
    p = jnp.exp2(s - m_new[:, 0:1])
    l_ref[...] = alpha * l_ref[...] + jnp.sum(p, axis=-1, keepdims=True)
    acc_ref[...] = alpha * acc_ref[...] + _nt_dot(p.astype(BF16), vt)
    m_ref[...] = m_new


def _nsa_sample_kernel(tbl_ref, *refs, n_pages, tq):
    del tbl_ref
    q_ref, gate_ref = refs[:2]
    page_refs = refs[2:2 + SEL_PAGES]
    (cmp_ref, win_ref, newkv_ref, newwin_ref, covt_ref, out_ref, qs_ref, s_ref, msel_ref,
     m_ref, l_ref, acc_ref, ocmp_ref) = refs[2 + SEL_PAGES:]
    j = pl.program_id(1)
    past_len = n_pages * PAGE_SIZE
    rows = N_HEADS * tq
    step_keys = SEL_PAGES * PAGE_SIZE
    i_col = lax.broadcasted_iota(jnp.int32, (rows, 1), 0) & (tq - 1)

    @pl.when(j == 0)
    def _():
        for h in range(N_HEADS):
            qs_ref[h * tq:(h + 1) * tq, :] = q_ref[0, :, h * LANES:(h + 1) * LANES]
        t_col = past_len + lax.broadcasted_iota(jnp.int32, (tq, 1), 0)
        psums = _cmp_branch(qs_ref[...].astype(BF16), s_ref, cmp_ref, ocmp_ref, t_col, tq)
        s_idx = lax.broadcasted_iota(jnp.int32, (N_BLK_PAD, LANES), 0)
        t_row = past_len + lax.broadcasted_iota(jnp.int32, (N_BLK_PAD, LANES), 1)
        cur = jnp.right_shift(t_row, LOG2_SEL_BLOCK)
        forced = (s_idx == 0) | (s_idx == cur) | (s_idx == cur - 1)
        ncp = cmp_ref.shape[1]
        for g in range(N_KV):
            psum = jnp.concatenate([psums[g], jnp.zeros((LANES - tq, ncp), F32)], axis=0)
            imp_t = jnp.where(forced, FORCE, _split3_nt(covt_ref[...], psum))
            mask_t = _topk_block_mask(imp_t, N_SEL - 1)
            msel_ref[g] = mask_t.T.astype(BF16)
        m_ref[...] = jnp.full(m_ref.shape, NEG, F32)
        l_ref[...] = jnp.zeros(l_ref.shape, F32)
        acc_ref[...] = jnp.zeros(acc_ref.shape, F32)

    q = qs_ref[...].astype(BF16)
    kt = jnp.concatenate([page[0].astype(BF16) for page in page_refs], axis=1)
    vt = jnp.concatenate([page[1].astype(BF16) for page in page_refs], axis=1)
    blk_row = lax.broadcasted_iota(jnp.int32, (N_BLK_PAD, step_keys), 0)
    key_pos = j * step_keys + lax.broadcasted_iota(jnp.int32, (N_BLK_PAD, step_keys), 1)
    expand = jnp.where(blk_row == jnp.right_shift(key_pos, LOG2_SEL_BLOCK), 1.0, 0.0).astype(BF16)
    bias_g = [_dot(msel_ref[g], expand)[0:tq, :] for g in range(N_KV)]
    bias = jnp.concatenate([bias_g[hh // HPG] for hh in range(N_HEADS)], axis=0)
    _flash_update(_dot(q, kt) + bias, vt, m_ref, l_ref, acc_ref)

    @pl.when(j == n_pages // SEL_PAGES - 1)
    def _():
        j_lane = lax.broadcasted_iota(jnp.int32, (rows, PAGE_SIZE), 1)
        new_bias = jnp.where(j_lane <= i_col, 0.0, NEG)
        _flash_update(_dot(q, newkv_ref[0, 0:LANES, :]) + new_bias, newkv_ref[0, LANES:2 * LANES, :],
                      m_ref, l_ref, acc_ref)
        wb = win_ref.shape[2]
        dist = wb + i_col - lax.broadcasted_iota(jnp.int32, (rows, wb), 1)
        s1 = _dot(q, win_ref[0, 0:LANES, :].astype(BF16)) + jnp.where((dist >= 0) & (dist <= WINDOW), 0.0, NEG)
        s2 = _dot(q, newwin_ref[0, 0:LANES, :]) + new_bias
        m = jnp.maximum(jnp.max(s1, axis=-1, keepdims=True), jnp.max(s2, axis=-1, keepdims=True))
        p1 = jnp.exp2(s1 - m)
        p2 = jnp.exp2(s2 - m)
        l = jnp.sum(p1, axis=-1, keepdims=True) + jnp.sum(p2, axis=-1, keepdims=True)
        o_win = (_nt_dot(p1.astype(BF16), win_ref[0, LANES:2 * LANES, :].astype(BF16))
                 + _nt_dot(p2.astype(BF16), newwin_ref[0, LANES:2 * LANES, :])) / l
        o_sel = acc_ref[...] / l_ref[...]
        gates = jax.nn.sigmoid(gate_ref[0])
        for hh in range(N_HEADS):
            r = slice(hh * tq, (hh + 1) * tq)
            o = (gates[:, 3 * hh:3 * hh + 1] * ocmp_ref[hh] + gates[:, 3 * hh + 1:3 * hh + 2] * o_sel[r]
                 + gates[:, 3 * hh + 2:3 * hh + 3] * o_win[r])
            out_ref[0, :, hh * LANES:(hh + 1) * LANES] = o


def _nsa_sample(q, gates, pages, table, cmp, win, newkv, newwin):
    s, tq, _ = q.shape
    n_pages = table.shape[1]
    ncp = cmp.shape[1]
    wb = win.shape[2]
    assert (n_pages * PAGE_SIZE) // SEL_BLOCK == N_BLK_PAD and tq <= SEL_BLOCK and tq & (tq - 1) == 0
    assert n_pages % SEL_PAGES == 0
    rows = N_HEADS * tq
    per_seq = lambda r, w: pl.BlockSpec((1, r, w), lambda b, j, t: (b, 0, 0))
    page_spec = lambda k: pl.BlockSpec((None, 2, LANES, PAGE_SIZE),
                                       lambda b, j, t: (t[b, j * SEL_PAGES + k], 1, 0, 0))
    grid_spec = pltpu.PrefetchScalarGridSpec(
        num_scalar_prefetch=1,
        grid=(s, n_pages // SEL_PAGES),
        in_specs=[per_seq(tq, QPAD), per_seq(tq, LANES)] + [page_spec(k) for k in range(SEL_PAGES)] + [
            per_seq(ncp, 2 * LANES), per_seq(2 * LANES, wb),
            per_seq(2 * LANES, PAGE_SIZE), per_seq(2 * LANES, PAGE_SIZE),
            pl.BlockSpec((N_BLK_PAD, ncp), lambda b, j, t: (0, 0))],
        out_specs=per_seq(tq, QPAD),
        scratch_shapes=[
            pltpu.VMEM((rows, LANES), F32),
            pltpu.VMEM((rows, ncp), F32),
            pltpu.VMEM((N_KV, LANES, N_BLK_PAD), BF16),
            pltpu.VMEM((rows, LANES), F32),
            pltpu.VMEM((rows, LANES), F32),
            pltpu.VMEM((rows, LANES), F32),
            pltpu.VMEM((N_HEADS, tq, LANES), F32),
        ],
    )
    return pl.pallas_call(
        functools.partial(_nsa_sample_kernel, n_pages=n_pages, tq=tq),
        grid_spec=grid_spec,
        out_shape=jax.ShapeDtypeStruct((s, tq, QPAD), F32),
        compiler_params=_cparams("parallel", "arbitrary"),
        name="nsa_sample",
    )(table, q, gates, *([pages] * SEL_PAGES), cmp, win, newkv, newwin, _cover_t(ncp))


S5_LANE_BLOCK = 512


def _s5_kernel(u_ref, z_ref, h0_ref, bbd_ref, lam_ref, cbd_ref, d_ref, y_ref, hl_ref,
               bu_ref, hs_ref, h_ref):
    c = pl.program_id(1)
    steps = u_ref.shape[1]

    @pl.when(c == 0)
    def _():
        h_ref[...] = h0_ref[0]

    u = u_ref[0]
    ub = u.astype(BF16)
    halves = [(slice(k * SSM_WIDTH // 2, (k + 1) * SSM_WIDTH // 2),
               [slice(p * N_STATE + k * N_STATE // 2, p * N_STATE + (k + 1) * N_STATE // 2) for p in range(2)])
              for k in range(2)]
    for ch, parts in halves:
        for st in parts:
            bu_ref[:, st] = _dot(ub[:, ch], bbd_ref[ch, st])
    for blk in range(N_STATE // S5_LANE_BLOCK):
        re = slice(blk * S5_LANE_BLOCK, (blk + 1) * S5_LANE_BLOCK)
        im = slice(N_STATE + blk * S5_LANE_BLOCK, N_STATE + (blk + 1) * S5_LANE_BLOCK)
        lr = lam_ref[:, re]
        li = lam_ref[:, im]

        def step(t, carry, re=re, im=im, lr=lr, li=li):
            hr, hi = carry
            row = pl.ds(t, 1)
            nr = lr * hr - li * hi + bu_ref[row, re]
            ni = lr * hi + li * hr + bu_ref[row, im]
            hs_ref[row, re] = nr
            hs_ref[row, im] = ni
            return nr, ni

        hr, hi = lax.fori_loop(0, steps, step, (h_ref[:, re], h_ref[:, im]),
                               unroll=min(8, steps))
        h_ref[:, re] = hr
        h_ref[:, im] = hi
    for ch, parts in halves:
        y = sum(_dot(hs_ref[:, st].astype(BF16), cbd_ref[st, ch]) for st in parts)
        y = y + d_ref[:, ch] * u[:, ch]
        y_ref[0, :, ch] = (jax.nn.gelu(y) * jax.nn.sigmoid(z_ref[0, :, ch])).astype(y_ref.dtype)

    @pl.when(c == pl.num_programs(1) - 1)
    def _():
        hl_ref[0] = h_ref[...]


def _s5_weights(lam_re, lam_im, log_dt, b_re, b_im, c_re, c_im, d_skip):
    lam = lax.complex(lam_re.astype(F32), lam_im.astype(F32))
    dt = jnp.exp(log_dt.astype(F32))[:, None]
    lam_bar = jnp.exp(lam * dt)
    b_bar = ((lam_bar - 1.0) / lam)[..., None] * lax.complex(b_re.astype(F32), b_im.astype(F32))
    eye = jnp.eye(N_SSM_GROUPS, dtype=F32)
    def in_bd(b):
        return jnp.einsum("gph,gk->ghkp", b, eye).reshape(SSM_WIDTH, N_STATE)
    bbd = jnp.concatenate([in_bd(b_bar.real), in_bd(b_bar.imag)], axis=1)
    def out_bd(cm):
        return jnp.einsum("ghp,gk->gpkh", cm, eye).reshape(N_STATE, SSM_WIDTH)
    cbd = jnp.concatenate([out_bd(c_re.astype(F32)), -out_bd(c_im.astype(F32))], axis=0)
    lam_row = jnp.concatenate([lam_bar.real.reshape(1, N_STATE), lam_bar.imag.reshape(1, N_STATE)], axis=1)
    return bbd.astype(BF16), lam_row, cbd.astype(BF16), d_skip.astype(F32).reshape(1, SSM_WIDTH)


def _s5(u, z, h0, s5w, chunk, out_dtype):
    bbd, lam_row, cbd, d_row = s5w
    s, t, _ = u.shape
    const = lambda shape: pl.BlockSpec(shape, lambda b, c: (0,) * len(shape))
    return pl.pallas_call(
        _s5_kernel,
        grid=(s, t // chunk),
        in_specs=[
            pl.BlockSpec((1, chunk, SSM_WIDTH), lambda b, c: (b, c, 0)),
            pl.BlockSpec((1, chunk, SSM_WIDTH), lambda b, c: (b, c, 0)),
            pl.BlockSpec((1, 1, 2 * N_STATE), lambda b, c: (b, 0, 0)),
            const((SSM_WIDTH, 2 * N_STATE)), const((1, 2 * N_STATE)),
            const((2 * N_STATE, SSM_WIDTH)), const((1, SSM_WIDTH)),
        ],
        out_specs=[
            pl.BlockSpec((1, chunk, SSM_WIDTH), lambda b, c: (b, c, 0)),
            pl.BlockSpec((1, 1, 2 * N_STATE), lambda b, c: (b, 0, 0)),
        ],
        out_shape=[jax.ShapeDtypeStruct((s, t, SSM_WIDTH), out_dtype),
                   jax.ShapeDtypeStruct((s, 1, 2 * N_STATE), F32)],
        scratch_shapes=[pltpu.VMEM((chunk, 2 * N_STATE), F32),
                        pltpu.VMEM((chunk, 2 * N_STATE), F32),
                        pltpu.VMEM((1, 2 * N_STATE), F32)],
        compiler_params=_cparams("parallel", "arbitrary"),
        name="s5",
    )(u, z, h0, bbd, lam_row, cbd, d_row)


def _outproj_kernel(x_ref, a_ref, s_ref, wa_ref, ws_ref, y_ref):
    y = x_ref[...]
    y = y + _dot(a_ref[...].astype(BF16), wa_ref[...])
    y = y + _dot(s_ref[...].astype(BF16), ws_ref[...])
    y_ref[...] = y


def _outproj_weights(w_out):
    wa = w_out[:NSA_WIDTH].reshape(N_KV, HPG, 1, HEAD_DIM, D_MODEL)
    slot = jnp.eye(N_KV, dtype=F32).reshape(N_KV, 1, N_KV, 1, 1)
    return (wa * slot).reshape(QPAD, D_MODEL).astype(BF16), w_out[NSA_WIDTH:].astype(BF16)


def _outproj(x2d, a_out, s_out, wa, ws, tm):
    n = x2d.shape[0]
    row = lambda w: pl.BlockSpec((tm, w), lambda i: (i, 0))
    return pl.pallas_call(
        _outproj_kernel,
        grid=(n // tm,),
        in_specs=[row(D_MODEL), row(QPAD), row(SSM_WIDTH),
                  pl.BlockSpec((QPAD, D_MODEL), lambda i: (0, 0)),
                  pl.BlockSpec((SSM_WIDTH, D_MODEL), lambda i: (0, 0))],
        out_specs=row(D_MODEL),
        out_shape=jax.ShapeDtypeStruct((n, D_MODEL), F32),
        compiler_params=_cparams("parallel"),
        name="outproj",
    )(x2d, a_out, s_out, wa, ws)


PEER_HALF = PEER_HEADS * PEER_DK // 2
_CAND_COUNTS = tuple(PEER_TOPK // (a + 1) for a in range(PEER_TOPK))
N_CAND = sum(_CAND_COUNTS)
N_CAND_PAD = -(-N_CAND // 8) * 8


def _peer_route_kernel(y_ref, g_ref, wq_ref, k1_ref, k2_ref, xn_ref, c1_ref, e1_ref, r2_ref, e2_ref,
                       s1_ref, s2_ref, v1_ref, v2_ref, cand_ref):
    xn = _rmsnorm(y_ref[...], g_ref[...]).astype(BF16)
    xn_ref[...] = xn
    q = _dot(xn, wq_ref[...]).astype(BF16)
    s1_ref[...] = _nt_dot(k1_ref[...], q[:, :PEER_HALF])
    s2_ref[...] = _nt_dot(k2_ref[...], q[:, PEER_HALF:])
    t = y_ref.shape[0]
    cand_ref[N_CAND:N_CAND_PAD, :] = jnp.full((N_CAND_PAD - N_CAND, t), LOWEST, F32)
    for h in range(PEER_HEADS):
        rows = slice(h * N_KEYS, (h + 1) * N_KEYS)
        s1 = s1_ref[rows, :]
        s2 = s2_ref[rows, :]
        rank1 = _take_top(s1, PEER_TOPK, v1_ref)
        rank2 = _take_top(s2, PEER_TOPK, v2_ref)
        v1 = v1_ref[...]
        v2 = v2_ref[...]
        off = 0
        for a, nb in enumerate(_CAND_COUNTS):
            cand_ref[off:off + nb, :] = v1[a:a + 1, :] + v2[0:nb, :]
            off += nb
        cand = cand_ref[...]
        taken = jnp.where(_take_top(cand, PEER_TOPK) < float(PEER_TOPK), 1.0, 0.0)
        z = jnp.sum(taken * jnp.exp(cand - cand[0:1, :]), axis=0, keepdims=True)
        count = jnp.zeros((N_KEYS, t), F32)
        off = 0
        for a, nb in enumerate(_CAND_COUNTS):
            n_a = jnp.sum(taken[off:off + nb, :], axis=0, keepdims=True)
            count = jnp.where(rank1 == float(a), n_a, count)
            off += nb
        c1_ref[rows, :] = count
        e1_ref[rows, :] = jnp.exp(s1 - v1[0:1, :]) / z
        r2_ref[rows, :] = rank2.astype(BF16)
        e2_ref[rows, :] = jnp.exp(s2 - v2[0:1, :]).astype(BF16)


def _peer_weights(w_q, sub_k1, sub_k2):
    wq = w_q.reshape(D_MODEL, PEER_HEADS, 2, PEER_DK // 2).transpose(0, 2, 1, 3).reshape(D_MODEL, 2 * PEER_HALF)
    eye = jnp.eye(PEER_HEADS, dtype=F32)
    bd = lambda k: jnp.einsum("hkd,hj->hkjd", k, eye).reshape(PEER_HEADS * N_KEYS, PEER_HALF)
    return wq.astype(BF16), bd(sub_k1).astype(BF16), bd(sub_k2).astype(BF16)


def _peer_route(y2d, norm_g, wq, k1bd, k2bd, tm):
    n = y2d.shape[0]
    hk = PEER_HEADS * N_KEYS
    const = lambda a: pl.BlockSpec(a.shape, lambda i: (0, 0))
    col = lambda r: pl.BlockSpec((r, tm), lambda i: (0, i))
    tshape = lambda dt: jax.ShapeDtypeStruct((hk, n), dt)
    return pl.pallas_call(
        _peer_route_kernel,
        grid=(n // tm,),
        in_specs=[pl.BlockSpec((tm, D_MODEL), lambda i: (i, 0)),
                  pl.BlockSpec((1, D_MODEL), lambda i: (0, 0)), const(wq), const(k1bd), const(k2bd)],
        out_specs=[pl.BlockSpec((tm, D_MODEL), lambda i: (i, 0)), col(hk), col(hk), col(hk), col(hk)],
        out_shape=[jax.ShapeDtypeStruct((n, D_MODEL), BF16),
                   tshape(F32),
                   tshape(F32),
                   tshape(BF16),
                   tshape(BF16)],
        scratch_shapes=[pltpu.VMEM((hk, tm), F32), pltpu.VMEM((hk, tm), F32),
                        pltpu.VMEM((PEER_TOPK, tm), F32), pltpu.VMEM((PEER_TOPK, tm), F32),
                        pltpu.VMEM((N_CAND_PAD, tm), F32)],
        compiler_params=_cparams("parallel"),
        name="peer_route",
    )(y2d, norm_g.reshape(1, D_MODEL), wq, k1bd, k2bd)


EXPERT_BLOCK = 1024
I1_PER_BLOCK = EXPERT_BLOCK // N_KEYS
N_EXPERT_BLOCKS = N_EXPERTS // EXPERT_BLOCK


def _peer_dense_kernel(y_ref, xn_ref, u_ref, vt_ref, c1_ref, e1_ref, r2_ref, e2_ref,
                       gf_ref, out_ref, acc_ref, act_ref, ga_ref):
    j = pl.program_id(1)
    t = xn_ref.shape[0]

    @pl.when(j == 0)
    def _():
        acc_ref[...] = jnp.zeros(acc_ref.shape, F32)
        act_ref[...] = jnp.zeros(act_ref.shape, BF16)
        ga_ref[...] = jnp.zeros(ga_ref.shape, BF16)

    def gate_stage(prev):
        jb = j - 1
        for ii in range(I1_PER_BLOCK):
            erows = slice(ii * N_KEYS, (ii + 1) * N_KEYS)
            for tc in range(t // LANES):
                cols = slice(tc * LANES, (tc + 1) * LANES)
                gate = jnp.zeros((N_KEYS, LANES), BF16)
                for h in range(PEER_HEADS):
                    grp = pl.ds(pl.multiple_of(h * N_KEYS + jb * I1_PER_BLOCK, I1_PER_BLOCK), I1_PER_BLOCK)
                    krows = slice(h * N_KEYS, (h + 1) * N_KEYS)
                    count = c1_ref[grp, cols][ii:ii + 1, :].astype(BF16)
                    e1 = e1_ref[grp, cols][ii:ii + 1, :].astype(BF16)
                    gate = gate + jnp.where(r2_ref[krows, cols] < count, e2_ref[krows, cols] * e1,
                                            jnp.zeros((), BF16))
                ga_ref[prev, erows, cols] = gate * act_ref[prev, erows, cols]

    def stages(cur, prev):
        pl.when((j >= 1) & (j <= N_EXPERT_BLOCKS))(functools.partial(gate_stage, prev))

        acc_ref[...] += _dot(vt_ref[...], ga_ref[cur])

        act_ref[cur] = _gelu_tanh(_nt_dot(u_ref[...], xn_ref[...])).astype(BF16)

    for parity in range(2):
        pl.when(j % 2 == parity)(functools.partial(stages, parity, 1 - parity))

    @pl.when(j == pl.num_programs(1) - 1)
    def _():
        y = y_ref[...] + acc_ref[...].T
        out_ref[...] = _rmsnorm(y, gf_ref[...])


def _peer_dense(y2d, xn, u_bf, vt_bf, c1, e1, r2, e2, norm_f, tm):
    n = y2d.shape[0]
    hk = PEER_HEADS * N_KEYS
    tok = lambda w: pl.BlockSpec((tm, w), lambda i, j: (i, 0))
    col = lambda r: pl.BlockSpec((r, tm), lambda i, j: (0, i))
    last = N_EXPERT_BLOCKS - 1
    return pl.pallas_call(
        _peer_dense_kernel,
        grid=(n // tm, N_EXPERT_BLOCKS + 2),
        in_specs=[tok(D_MODEL), tok(D_MODEL),
                  pl.BlockSpec((EXPERT_BLOCK, D_MODEL), lambda i, j: (jnp.minimum(j, last), 0)),
                  pl.BlockSpec((D_MODEL, EXPERT_BLOCK), lambda i, j: (0, jnp.maximum(j - 2, 0))),
                  col(hk), col(hk), col(hk), col(hk),
                  pl.BlockSpec((1, D_MODEL), lambda i, j: (0, 0))],
        out_specs=tok(D_MODEL),
        out_shape=jax.ShapeDtypeStruct((n, D_MODEL), F32),
        scratch_shapes=[pltpu.VMEM((D_MODEL, tm), F32),
                        pltpu.VMEM((2, EXPERT_BLOCK, tm), BF16),
                        pltpu.VMEM((2, EXPERT_BLOCK, tm), BF16)],
        compiler_params=_cparams("parallel", "arbitrary"),
        name="peer_dense",
    )(y2d, xn, u_bf, vt_bf, c1, e1, r2, e2, norm_f.reshape(1, D_MODEL))


PROJ_TILE = 512
PEER_TILE = 256
PEER_DENSE_TILE = 512
S5_CHUNK = 256


def _state_to_rows(st):
    s = st.shape[0]
    return jnp.concatenate([st[..., 0].reshape(s, 1, N_STATE), st[..., 1].reshape(s, 1, N_STATE)], axis=-1)


def _rows_to_state(h):
    s = h.shape[0]
    shape = (s, N_SSM_GROUPS, SSM_STATE)
    return jnp.stack([h[:, 0, :N_STATE].reshape(shape), h[:, 0, N_STATE:].reshape(shape)], axis=-1)


def _feat_to_tokens(a, n_types):
    s, _, t = a.shape
    return a.reshape(s, n_types, N_KV, HEAD_DIM, t).transpose(0, 4, 1, 2, 3)


def _peer_block(y2d, norm_g, peer_w, norm_f, tm):
    wq, k1bd, k2bd, u_bf, vt_bf = peer_w
    xn, c1, e1, r2, e2 = _peer_route(y2d, norm_g, wq, k1bd, k2bd, tm)
    return _peer_dense(y2d, xn, u_bf, vt_bf, c1, e1, r2, e2, norm_f, min(PEER_DENSE_TILE, y2d.shape[0]))


def kernel(x_prompt, x_sample, cache_kv, cache_win, state_ssm, page_table, norm_mix, w_in, w_cmp1, w_cmp2, pe_cmp, lam_re, lam_im, log_dt, b_re, b_im, c_re, c_im, d_skip, w_out, norm_ffn, w_q_peer, sub_k1, sub_k2, u_tab, v_tab, norm_final):
    b, t, d = x_prompt.shape
    db, ts, _ = x_sample.shape
    assert w_in.shape[0] == DEPTH == 1 and d == D_MODEL
    l = 0
    n_pool = cache_kv.shape[1]
    wb = cache_win.shape[2]

    w_tok, w_feat = _proj_weights(w_in[l])
    cmp_w = _compress_weights(w_cmp1[l], w_cmp2[l], pe_cmp[l])
    s5_w = _s5_weights(lam_re[l], lam_im[l], log_dt[l], b_re[l], b_im[l], c_re[l], c_im[l], d_skip[l])
    wa, ws = _outproj_weights(w_out[l])
    peer_w = _peer_weights(w_q_peer[l], sub_k1[l], sub_k2[l]) + (
        u_tab[l].astype(BF16), v_tab[l].T.astype(BF16))

    q, gate, u, z, kvsel_t, ksvs_t, kvwin_t, kwvw_t = _project(x_prompt, norm_mix[l], w_tok, w_feat, PROJ_TILE, BF16)
    n_pg = t // PAGE_SIZE
    table_p = jnp.broadcast_to(jnp.arange(n_pg, dtype=jnp.int32), (b, n_pg))
    prompt_page = lambda k: pl.BlockSpec((None, 2, LANES, PAGE_SIZE),
                                         lambda bb, j, tb: (bb, 0, 0, tb[bb, j * CMP_PAGES + k]))
    cmp_p = _compress(kvsel_t.reshape(b, 4, LANES, t), prompt_page, table_p, *cmp_w)
    a_p = _nsa_prompt(q.reshape(b, t, QPAD), gate.reshape(b, t, LANES), ksvs_t, kwvw_t, cmp_p)
    s_p, h_p = _s5(u.reshape(b, t, SSM_WIDTH), z.reshape(b, t, SSM_WIDTH),
                   jnp.zeros((b, 1, 2 * N_STATE), F32), s5_w, S5_CHUNK, BF16)
    y1p = _outproj(x_prompt.reshape(b * t, d), a_p.reshape(b * t, QPAD), s_p.reshape(b * t, SSM_WIDTH),
                   wa, ws, PROJ_TILE)
    y_prompt = _peer_block(y1p, norm_ffn[l], peer_w, norm_final, PEER_TILE).reshape(b, t, d)
    kv_prompt = _feat_to_tokens(kvsel_t, 4)
    win_prompt = _feat_to_tokens(kvwin_t[:, :, t - min(WINDOW, t):], 2)
    ssm_prompt = _rows_to_state(h_p)

    qs, gate_s, u_s, z_s, kvsel_st, ksvs_st, kvwin_st, kwvw_st = _project(
        x_sample.reshape(1, db * ts, d), norm_mix[l], w_tok, w_feat, db * ts, F32)
    pages_s = cache_kv[l].transpose(0, 2, 3, 4, 1).reshape(n_pool, 4, LANES, PAGE_SIZE)
    sample_page = lambda k: pl.BlockSpec((None, 2, LANES, PAGE_SIZE),
                                         lambda bb, j, tb: (tb[bb, j * CMP_PAGES + k], 0, 0, 0))
    cmp_s = _compress(pages_s, sample_page, page_table, *cmp_w)
    new_page = lambda a: jnp.pad(a[0].reshape(2 * LANES, db, ts).transpose(1, 0, 2),
                                 ((0, 0), (0, 0), (0, PAGE_SIZE - ts)))
    win_t = cache_win[l].transpose(0, 2, 3, 4, 1).reshape(db, 2 * LANES, wb)
    a_s = _nsa_sample(qs.reshape(db, ts, QPAD), gate_s.reshape(db, ts, LANES), pages_s, page_table, cmp_s,
                      win_t, new_page(ksvs_st), new_page(kwvw_st))
    s_s, h_s = _s5(u_s.reshape(db, ts, SSM_WIDTH), z_s.reshape(db, ts, SSM_WIDTH),
                   _state_to_rows(state_ssm[l].astype(F32)), s5_w, ts, F32)
    y1s = _outproj(x_sample.reshape(db * ts, d), a_s.reshape(db * ts, QPAD), s_s.reshape(db * ts, SSM_WIDTH),
                   wa, ws, db * ts)
    y_sample = _peer_block(y1s, norm_ffn[l], peer_w, norm_final, PEER_TILE).reshape(db, ts, d)
    per_tok = lambda a, n_types: a[0].reshape(n_types, N_KV, HEAD_DIM, db, ts).transpose(3, 4, 0, 1, 2)
    kv_sample = per_tok(kvsel_st, 4)
    win_new = per_tok(kvwin_st, 2).astype(cache_win.dtype)
    win_sample = jnp.concatenate([cache_win[l], win_new], axis=1)[:, ts:]
    ssm_sample = _rows_to_state(h_s)

    return (y_prompt, y_sample, kv_prompt[None], kv_sample[None], win_prompt[None], win_sample[None],
            ssm_prompt[None], ssm_sample[None])
```

```python
import functools
import math

import jax
import jax.numpy as jnp
from jax import lax
from jax.experimental import pallas as pl
from jax.experimental.pallas import tpu as pltpu

D_MODEL = 1024
DEPTH = 1
PAGE_SIZE = 128
N_HEADS = 8
N_KV = 2
HEAD_DIM = 64
HPG = N_HEADS // N_KV
CMP_LEN = 32
CMP_STRIDE = 16
SEL_BLOCK = 64
N_SEL = 16
WINDOW = 512
SSM_WIDTH = 512
SSM_GROUP = 16
N_SSM_GROUPS = SSM_WIDTH // SSM_GROUP
SSM_STATE = 64
N_KEYS = 128
N_EXPERTS = N_KEYS * N_KEYS
PEER_HEADS = 8
PEER_DK = 128
PEER_TOPK = 16
NSA_WIDTH = N_HEADS * HEAD_DIM
KV_WIDTH = N_KV * HEAD_DIM
GATE_WIDTH = 3 * N_HEADS
EPS = 1e-6
NEG = -1e30
FORCE = 1e4
LOG2_E = 1.0 / math.log(2.0)
LOWEST = -3.0e38

LANES = 128
BF16_ROWS = 16
VMEM_LIMIT = 56 * 1024 * 1024

N_STATE = N_SSM_GROUPS * SSM_STATE
QPAD = N_HEADS * LANES
F32 = jnp.float32
BF16 = jnp.bfloat16

assert KV_WIDTH == LANES and PAGE_SIZE == LANES


def _cparams(*sem):
    return pltpu.CompilerParams(dimension_semantics=sem, vmem_limit_bytes=VMEM_LIMIT)


def _nt_dot(a, b):
    return lax.dot_general(a, b, (((1,), (1,)), ((), ())), preferred_element_type=F32)


def _dot(a, b):
    return jnp.dot(a, b, preferred_element_type=F32)


def _rmsnorm(x, g):
    return x * lax.rsqrt(jnp.mean(x * x, axis=-1, keepdims=True) + EPS) * g


_GELU_C1 = 2.0 * math.sqrt(2.0 / math.pi) * LOG2_E
_GELU_C3 = _GELU_C1 * 0.044715


def _gelu_tanh(x):
    return x / (1.0 + jnp.exp2(x * (-_GELU_C1 - _GELU_C3 * (x * x))))


_PROJ_COLS = (("q", QPAD), ("gate", LANES), ("u", SSM_WIDTH), ("z", SSM_WIDTH), ("kswk", 2 * KV_WIDTH))
KVT_ROWS = 6 * KV_WIDTH


def _proj_kernel(x_ref, g_ref, w_ref, wt_ref, q_ref, gate_ref, u_ref, z_ref, kswk_ref,
                 qt_ref, kvsel_ref, ksvs_ref, kvwin_ref, kwvw_ref):
    xn = _rmsnorm(x_ref[...], g_ref[...]).astype(BF16)
    off = 0
    for (name, width), ref in zip(_PROJ_COLS, (q_ref, gate_ref, u_ref, z_ref, kswk_ref)):
        ref[...] = _dot(xn, w_ref[:, off:off + width]).astype(ref.dtype)
        off += width
    qt_ref[0] = _nt_dot(wt_ref[0:QPAD, :], xn).astype(BF16)
    kvt = _nt_dot(wt_ref[QPAD:, :], xn)
    kvsel_ref[0] = kvt[0:4 * KV_WIDTH]
    ksvs_ref[0] = kvt[2 * KV_WIDTH:4 * KV_WIDTH].astype(BF16)
    kvwin_ref[0] = kvt[4 * KV_WIDTH:]
    kwvw_ref[0] = kvt[4 * KV_WIDTH:].astype(BF16)


def _proj_weights(w_in):
    c0 = NSA_WIDTH
    c1 = c0 + 6 * KV_WIDTH
    c2 = c1 + GATE_WIDTH
    wq = w_in[:, :c0].reshape(D_MODEL, N_KV, HPG, 1, HEAD_DIM) * (HEAD_DIM ** -0.5 * LOG2_E)
    slot = jnp.eye(N_KV, dtype=F32).reshape(1, N_KV, 1, N_KV, 1)
    wq_pad = (wq * slot).reshape(D_MODEL, QPAD)
    wg = jnp.pad(w_in[:, c1:c2], ((0, 0), (0, LANES - GATE_WIDTH)))
    ks = w_in[:, c0 + 2 * KV_WIDTH:c0 + 3 * KV_WIDTH]
    kw = w_in[:, c0 + 4 * KV_WIDTH:c0 + 5 * KV_WIDTH]
    w = jnp.concatenate([wq_pad, wg, w_in[:, c2:], ks, kw], axis=1)
    wt = jnp.concatenate([wq_pad, w_in[:, c0:c1]], axis=1).T
    return w.astype(BF16), wt.astype(BF16)


def _project(x3d, norm_g, w_tok, w_feat, tm, q_dtype):
    s, t, _ = x3d.shape
    nt = t // tm
    n = s * t
    row = lambda w: pl.BlockSpec((tm, w), lambda b, i: (b * nt + i, 0))
    feat = lambda r: pl.BlockSpec((1, r, tm), lambda b, i: (b, 0, i))
    const = lambda a: pl.BlockSpec(a.shape, lambda b, i: (0, 0))
    shapes = [
        jax.ShapeDtypeStruct((n, QPAD), q_dtype),
        jax.ShapeDtypeStruct((n, LANES), F32),
        jax.ShapeDtypeStruct((n, SSM_WIDTH), F32),
        jax.ShapeDtypeStruct((n, SSM_WIDTH), F32),
        jax.ShapeDtypeStruct((n, 2 * KV_WIDTH), BF16),
        jax.ShapeDtypeStruct((s, QPAD, t), BF16),
        jax.ShapeDtypeStruct((s, 4 * KV_WIDTH, t), F32),
        jax.ShapeDtypeStruct((s, 2 * KV_WIDTH, t), BF16),
        jax.ShapeDtypeStruct((s, 2 * KV_WIDTH, t), F32),
        jax.ShapeDtypeStruct((s, 2 * KV_WIDTH, t), BF16),
    ]
    g2 = norm_g.reshape(1, D_MODEL)
    return pl.pallas_call(
        _proj_kernel,
        grid=(s, nt),
        in_specs=[row(D_MODEL), const(g2), const(w_tok), const(w_feat)],
        out_specs=[row(sh.shape[1]) for sh in shapes[:5]] + [feat(sh.shape[1]) for sh in shapes[5:]],
        out_shape=shapes,
        compiler_params=_cparams("parallel", "parallel"),
        name="proj",
    )(x3d.reshape(n, D_MODEL), g2, w_tok, w_feat)


CMP_PAGES = 8


def _compress_kernel(tbl_ref, *refs, n_pages):
    del tbl_ref
    page_refs = refs[:CMP_PAGES]
    w1_ref, pe_ref, w2_ref, out_ref, slabk_ref, slabv_ref = refs[CMP_PAGES:]
    j = pl.program_id(1)
    seq = n_pages * PAGE_SIZE
    slabs = (slabk_ref, slabv_ref)

    for k, page in enumerate(page_refs):
        rows = pl.ds(pl.multiple_of((j * CMP_PAGES + k) * PAGE_SIZE, PAGE_SIZE), PAGE_SIZE)
        for c, slab in enumerate(slabs):
            slab[rows, :] = page[c].T

    @pl.when(j == n_pages // CMP_PAGES - 1)
    def _():
        nb = seq // CMP_STRIDE
        for c, slab in enumerate(slabs):
            head = jnp.zeros((nb, LANES), F32)
            tail = jnp.zeros((nb, LANES), F32)
            for s in range(CMP_STRIDE):
                x = slab[pl.ds(s, nb, stride=CMP_STRIDE), :]
                head = head + _dot((x + pe_ref[c, s:s + 1, :]).astype(BF16), w1_ref[c, s])
                s2 = CMP_STRIDE + s
                tail = tail + _dot((x + pe_ref[c, s2:s2 + 1, :]).astype(BF16), w1_ref[c, s2])
            hid = head + pltpu.roll(tail, nb - 1, 0)
            out = _dot(jax.nn.gelu(hid).astype(BF16), w2_ref[c])
            out_ref[0, :, c * LANES:(c + 1) * LANES] = out.astype(BF16)


def _compress_weights(w1, w2, pe):
    eye = jnp.eye(N_KV, dtype=F32)
    bd = lambda m: jnp.einsum("...de,gk->...gdke", m, eye).reshape(m.shape[:-2] + (LANES, LANES))
    pe2 = jnp.concatenate([pe, pe], axis=-1)
    return bd(w1).astype(BF16), pe2, bd(w2).astype(BF16)


def _compress(pages, page_spec, table, w1bd, pe2, w2bd):
    s, p = table.shape
    assert CMP_LEN == 2 * CMP_STRIDE and p % CMP_PAGES == 0
    seq = p * PAGE_SIZE
    const = lambda a: pl.BlockSpec(a.shape, lambda b, j, t: (0,) * a.ndim)
    grid_spec = pltpu.PrefetchScalarGridSpec(
        num_scalar_prefetch=1,
        grid=(s, p // CMP_PAGES),
        in_specs=[page_spec(k) for k in range(CMP_PAGES)] + [const(w1bd), const(pe2), const(w2bd)],
        out_specs=pl.BlockSpec((1, seq // CMP_STRIDE, 2 * LANES), lambda b, j, t: (b, 0, 0)),
        scratch_shapes=[pltpu.VMEM((seq, LANES), F32), pltpu.VMEM((seq, LANES), F32)],
    )
    return pl.pallas_call(
        functools.partial(_compress_kernel, n_pages=p),
        grid_spec=grid_spec,
        out_shape=jax.ShapeDtypeStruct((s, seq // CMP_STRIDE, 2 * LANES), BF16),
        compiler_params=_cparams("parallel", "arbitrary"),
        name="compress",
    )(table, *([pages] * CMP_PAGES), w1bd, pe2, w2bd)


Q_TILE = 128
KEY_CHUNK = 512
N_BLK_PAD = 128
LOG2_SEL_BLOCK = int(math.log2(SEL_BLOCK))
WIN_KEYS = WINDOW + Q_TILE


def _cover_t(n_cmp_pad):
    n = jnp.arange(n_cmp_pad, dtype=jnp.int32)[None, :] * CMP_STRIDE
    s = jnp.arange(N_BLK_PAD, dtype=jnp.int32)[:, None] * SEL_BLOCK
    return ((n < s + SEL_BLOCK) & (n + CMP_LEN > s)).astype(BF16)


def _split3_nt(w, x):
    hi = x.astype(BF16)
    r1 = x - hi.astype(F32)
    mid = r1.astype(BF16)
    lo = (r1 - mid.astype(F32)).astype(BF16)
    return _nt_dot(w, hi) + _nt_dot(w, mid) + _nt_dot(w, lo)


def _take_top(s, n, out_ref=None, out_row=0, exact=True):
    kidx = lax.broadcasted_iota(jnp.int32, s.shape, 0).astype(F32)
    rem = s
    rank = jnp.full(s.shape, float(n), F32)
    for a in range(n):
        mx = jnp.max(rem, axis=0, keepdims=True)
        if exact:
            first = jnp.min(jnp.where(rem == mx, kidx, float(s.shape[0])), axis=0, keepdims=True)
            taken = kidx == first
        else:
            taken = rem == mx
        if out_ref is not None:
            out_ref[out_row + a:out_row + a + 1, :] = mx
        rank = jnp.where(taken, float(a), rank)
        rem = jnp.where(taken, LOWEST, rem)
    return rank


def _n_taken(rank, n):
    return jnp.sum(jnp.where(rank < float(n), 1.0, 0.0), axis=0, keepdims=True)


def _topk_block_mask(imp_t, n_keep):
    return jnp.where(_take_top(imp_t, n_keep) < float(n_keep), 0.0, NEG)


def _cmp_branch(q, s_ref, cmp_ref, ocmp_ref, t_pos, rows):
    ncp = cmp_ref.shape[1]
    kc = cmp_ref[0, :, 0:LANES]
    vc = cmp_ref[0, :, LANES:2 * LANES]
    s_ref[:, 0:ncp] = _nt_dot(q, kc)
    n_idx = lax.broadcasted_iota(jnp.int32, (rows, ncp), 1)
    ok = (n_idx * CMP_STRIDE + (CMP_LEN - 1) <= t_pos) & (n_idx < ncp - 1)
    imps = []
    for g in range(N_KV):
        psum = jnp.zeros((rows, ncp), F32)
        for h in range(HPG):
            hh = g * HPG + h
            s = jnp.where(ok, s_ref[hh * rows:(hh + 1) * rows, 0:ncp], NEG)
            m = jnp.max(s, axis=-1, keepdims=True)
            e = jnp.where(ok, jnp.exp2(s - m), 0.0)
            l = jnp.sum(e, axis=-1, keepdims=True)
            p = e * jnp.where(l > 0.0, 1.0 / l, 0.0)
            ocmp_ref[hh] = _dot(p.astype(BF16), vc)
            psum = psum + p
        imps.append(psum)
    return imps


def _with_ones_rows(vt, g):
    row = lax.broadcasted_iota(jnp.int32, vt.shape, 0)
    own = (row >= g * HEAD_DIM) & (row < (g + 1) * HEAD_DIM)
    return jnp.where(own, vt, jnp.ones((), vt.dtype))


def _block_expand(t):
    key_blk = jnp.arange(t, dtype=jnp.int32)[None, :] // SEL_BLOCK
    return (jnp.arange(N_BLK_PAD, dtype=jnp.int32)[:, None] == key_blk).astype(BF16)


def _nsa_prompt_t_kernel(q_ref, qt_ref, gate_ref, ktok_ref, vs_ref, vw_ref, cmp_ref, covt_ref, expt_ref,
                         out_ref, qs_ref, qst_ref, sc_ref, s_ref, p_ref, m_ref, acc_ref, ocmp_ref):
    i = pl.program_id(1)
    t0 = i * Q_TILE
    group_cols = HPG * Q_TILE
    for h in range(N_HEADS):
        cols = slice(h * Q_TILE, (h + 1) * Q_TILE)
        qs_ref[cols, :] = q_ref[0, :, h * LANES:(h + 1) * LANES]
        qst_ref[0:LANES, cols] = qt_ref[0, h * LANES:(h + 1) * LANES, :]
    t_col = t0 + lax.broadcasted_iota(jnp.int32, (Q_TILE, 1), 0)

    psums = _cmp_branch(qs_ref[...], sc_ref, cmp_ref, ocmp_ref, t_col, Q_TILE)
    s_idx = lax.broadcasted_iota(jnp.int32, (N_BLK_PAD, Q_TILE), 0)
    t_row = t0 + lax.broadcasted_iota(jnp.int32, (N_BLK_PAD, Q_TILE), 1)
    cur = jnp.right_shift(t_row, LOG2_SEL_BLOCK)
    forced = (s_idx == 0) | (s_idx == cur) | (s_idx == cur - 1)
    for g in range(N_KV):
        imp_t = _split3_nt(covt_ref[...], psums[g])
        imp_t = jnp.where(forced, FORCE, imp_t)
        imp_t = jnp.where(s_idx * SEL_BLOCK <= t_row, imp_t, NEG)
        mask_t = _topk_block_mask(imp_t, N_SEL).astype(BF16)
        for h in range(HPG):
            hh = g * HPG + h
            qst_ref[LANES:2 * LANES, hh * Q_TILE:(hh + 1) * Q_TILE] = mask_t

    def attend(n_keys, v_aug, bias_t):
        for g in range(N_KV):
            alphas = []
            for h in range(HPG):
                hh = g * HPG + h
                s = s_ref[0:n_keys, hh * Q_TILE:(hh + 1) * Q_TILE]
                if bias_t is not None:
                    s = s + bias_t
                m_old = m_ref[hh:hh + 1, :]
                m_new = jnp.maximum(m_old, jnp.max(s, axis=0, keepdims=True))
                p_ref[0:n_keys, h * Q_TILE:(h + 1) * Q_TILE] = jnp.exp2(s - m_new).astype(BF16)
                alphas.append(jnp.exp2(m_old - m_new))
                m_ref[hh:hh + 1, :] = m_new
            alpha = jnp.concatenate(alphas, axis=1)
            acc_ref[g] = alpha * acc_ref[g] + _dot(v_aug[g], p_ref[0:n_keys, :])

    def reset():
        m_ref[...] = jnp.full(m_ref.shape, NEG, F32)
        acc_ref[...] = jnp.zeros(acc_ref.shape, F32)

    def outputs():
        outs = []
        for g in range(N_KV):
            lsum = (1 - g) * HEAD_DIM
            for h in range(HPG):
                a = acc_ref[g, :, h * Q_TILE:(h + 1) * Q_TILE]
                outs.append(a / a[lsum:lsum + 1, :])
        return outs

    reset()
    key_sub = lax.broadcasted_iota(jnp.int32, (KEY_CHUNK, Q_TILE), 0)
    t_lane = t0 + lax.broadcasted_iota(jnp.int32, (KEY_CHUNK, Q_TILE), 1)

    def chunk(c, diagonal):
        k0 = pl.multiple_of(c * KEY_CHUNK, KEY_CHUNK)
        keys = pl.ds(k0, KEY_CHUNK)
        lhs = jnp.concatenate([ktok_ref[0, keys, 0:LANES], expt_ref[keys, :]], axis=1)
        s_ref[0:KEY_CHUNK, :] = _dot(lhs, qst_ref[...])
        vt = vs_ref[0, LANES:2 * LANES, keys]
        causal = jnp.where(k0 + key_sub <= t_lane, 0.0, NEG) if diagonal else None
        attend(KEY_CHUNK, [_with_ones_rows(vt, g) for g in range(N_KV)], causal)

    n_full = t0 // KEY_CHUNK

    def full_chunk(c, carry):
        chunk(c, False)
        return carry

    lax.fori_loop(0, n_full, full_chunk, 0)
    chunk(n_full, True)
    o_sel = outputs()

    reset()
    start = pl.multiple_of(jnp.maximum(t0 - WINDOW, 0), Q_TILE)
    wkeys = pl.ds(start, WIN_KEYS)
    s_ref[0:WIN_KEYS, :] = _dot(ktok_ref[0, wkeys, LANES:2 * LANES], qst_ref[0:LANES, :])
    vwt = vw_ref[0, LANES:2 * LANES, wkeys]
    dist = (t0 + lax.broadcasted_iota(jnp.int32, (WIN_KEYS, Q_TILE), 1)
            - (start + lax.broadcasted_iota(jnp.int32, (WIN_KEYS, Q_TILE), 0)))
    bias_w = jnp.where((dist >= 0) & (dist <= WINDOW), 0.0, NEG)
    attend(WIN_KEYS, [_with_ones_rows(vwt, g) for g in range(N_KV)], bias_w)
    o_win = outputs()

    gates = jax.nn.sigmoid(gate_ref[0])
    gates_t = gates.T
    for hh in range(N_HEADS):
        o_t = gates_t[3 * hh + 1:3 * hh + 2, :] * o_sel[hh] + gates_t[3 * hh + 2:3 * hh + 3, :] * o_win[hh]
        o = gates[:, 3 * hh:3 * hh + 1] * ocmp_ref[hh] + o_t.T
        out_ref[0, :, hh * LANES:(hh + 1) * LANES] = o.astype(out_ref.dtype)


def _nsa_prompt_t(q, qt, gates, kswk, ksvs_t, kwvw_t, cmp):
    b, t, _ = q.shape
    ncp = cmp.shape[1]
    tile = lambda w: pl.BlockSpec((1, Q_TILE, w), lambda bb, i: (bb, i, 0))
    whole = lambda r, w: pl.BlockSpec((1, r, w), lambda bb, i: (bb, 0, 0))
    const = lambda r, w: pl.BlockSpec((r, w), lambda bb, i: (0, 0))
    rows = N_HEADS * Q_TILE
    expand_t = _block_expand(t).T
    return pl.pallas_call(
        _nsa_prompt_t_kernel,
        grid=(b, t // Q_TILE),
        in_specs=[tile(QPAD), pl.BlockSpec((1, QPAD, Q_TILE), lambda bb, i: (bb, 0, i)), tile(LANES),
                  whole(t, 2 * LANES), whole(2 * LANES, t), whole(2 * LANES, t),
                  whole(ncp, 2 * LANES), const(N_BLK_PAD, ncp), const(t, N_BLK_PAD)],
        out_specs=tile(QPAD),
        out_shape=jax.ShapeDtypeStruct((b, t, QPAD), BF16),
        scratch_shapes=[
            pltpu.VMEM((rows, LANES), BF16),
            pltpu.VMEM((2 * LANES, rows), BF16),
            pltpu.VMEM((rows, ncp), F32),
            pltpu.VMEM((WIN_KEYS, rows), F32),
            pltpu.VMEM((WIN_KEYS, HPG * Q_TILE), BF16),
            pltpu.VMEM((N_HEADS, Q_TILE), F32),
            pltpu.VMEM((N_KV, LANES, HPG * Q_TILE), F32),
            pltpu.VMEM((N_HEADS, Q_TILE, LANES), F32),
        ],
        compiler_params=_cparams("parallel", "arbitrary"),
        name="nsa_prompt",
    )(q, qt, gates, kswk, ksvs_t, kwvw_t, cmp, _cover_t(ncp), expand_t)


SEL_PAGES = 8


def _flash_update(s, vt, m_ref, l_ref, acc_ref):
    m_old = m_ref[...]
    m_new = jnp.maximum(m_old, jnp.max(s, axis=-1, keepdims=True))
    alpha = jnp.exp2(m_old - m_new)
    p = jnp.exp2(s - m_new[:, 0:1])
    l_ref[...] = alpha * l_ref[...] + jnp.sum(p, axis=-1, keepdims=True)
    acc_ref[...] = alpha * acc_ref[...] + _nt_dot(p.astype(BF16), vt)
    m_ref[...] = m_new


def _nsa_sample_kernel(tbl_ref, *refs, n_pages, tq):
    del tbl_ref
    q_ref, gate_ref = refs[:2]
    page_refs = refs[2:2 + SEL_PAGES]
    (cmp_ref, win_ref, newkv_ref, newwin_ref, covt_ref, out_ref, qs_ref, s_ref, msel_ref,
     m_ref, l_ref, acc_ref, ocmp_ref) = refs[2 + SEL_PAGES:]
    j = pl.program_id(1)
    past_len = n_pages * PAGE_SIZE
    rows = N_HEADS * tq
    step_keys = SEL_PAGES * PAGE_SIZE
    i_col = lax.broadcasted_iota(jnp.int32, (rows, 1), 0) & (tq - 1)

    @pl.when(j == 0)
    def _():
        for h in range(N_HEADS):
            qs_ref[h * tq:(h + 1) * tq, :] = q_ref[0, :, h * LANES:(h + 1) * LANES]
        t_col = past_len + lax.broadcasted_iota(jnp.int32, (tq, 1), 0)
        psums = _cmp_branch(qs_ref[...].astype(BF16), s_ref, cmp_ref, ocmp_ref, t_col, tq)
        s_idx = lax.broadcasted_iota(jnp.int32, (N_BLK_PAD, LANES), 0)
        t_row = past_len + lax.broadcasted_iota(jnp.int32, (N_BLK_PAD, LANES), 1)
        cur = jnp.right_shift(t_row, LOG2_SEL_BLOCK)
        forced = (s_idx == 0) | (s_idx == cur) | (s_idx == cur - 1)
        ncp = cmp_ref.shape[1]
        for g in range(N_KV):
            psum = jnp.concatenate([psums[g], jnp.zeros((LANES - tq, ncp), F32)], axis=0)
            imp_t = jnp.where(forced, FORCE, _split3_nt(covt_ref[...], psum))
            mask_t = _topk_block_mask(imp_t, N_SEL - 1)
            msel_ref[g] = mask_t.T.astype(BF16)
        m_ref[...] = jnp.full(m_ref.shape, NEG, F32)
        l_ref[...] = jnp.zeros(l_ref.shape, F32)
        acc_ref[...] = jnp.zeros(acc_ref.shape, F32)

    q = qs_ref[...].astype(BF16)
    kt = jnp.concatenate([page[0].astype(BF16) for page in page_refs], axis=1)
    vt = jnp.concatenate([page[1].astype(BF16) for page in page_refs], axis=1)
    blk_row = lax.broadcasted_iota(jnp.int32, (N_BLK_PAD, step_keys), 0)
    key_pos = j * step_keys + lax.broadcasted_iota(jnp.int32, (N_BLK_PAD, step_keys), 1)
    expand = jnp.where(blk_row == jnp.right_shift(key_pos, LOG2_SEL_BLOCK), 1.0, 0.0).astype(BF16)
    bias_g = [_dot(msel_ref[g], expand)[0:tq, :] for g in range(N_KV)]
    bias = jnp.concatenate([bias_g[hh // HPG] for hh in range(N_HEADS)], axis=0)
    _flash_update(_dot(q, kt) + bias, vt, m_ref, l_ref, acc_ref)

    @pl.when(j == n_pages // SEL_PAGES - 1)
    def _():
        j_lane = lax.broadcasted_iota(jnp.int32, (rows, PAGE_SIZE), 1)
        new_bias = jnp.where(j_lane <= i_col, 0.0, NEG)
        _flash_update(_dot(q, newkv_ref[0, 0:LANES, :]) + new_bias, newkv_ref[0, LANES:2 * LANES, :],
                      m_ref, l_ref, acc_ref)
        wb = win_ref.shape[2]
        dist = wb + i_col - lax.broadcasted_iota(jnp.int32, (rows, wb), 1)
        s1 = _dot(q, win_ref[0, 0:LANES, :].astype(BF16)) + jnp.where((dist >= 0) & (dist <= WINDOW), 0.0, NEG)
        s2 = _dot(q, newwin_ref[0, 0:LANES, :]) + new_bias
        m = jnp.maximum(jnp.max(s1, axis=-1, keepdims=True), jnp.max(s2, axis=-1, keepdims=True))
        p1 = jnp.exp2(s1 - m)
        p2 = jnp.exp2(s2 - m)
        l = jnp.sum(p1, axis=-1, keepdims=True) + jnp.sum(p2, axis=-1, keepdims=True)
        o_win = (_nt_dot(p1.astype(BF16), win_ref[0, LANES:2 * LANES, :].astype(BF16))
                 + _nt_dot(p2.astype(BF16), newwin_ref[0, LANES:2 * LANES, :])) / l
        o_sel = acc_ref[...] / l_ref[...]
        gates = jax.nn.sigmoid(gate_ref[0])
        for hh in range(N_HEADS):
            r = slice(hh * tq, (hh + 1) * tq)
            o = (gates[:, 3 * hh:3 * hh + 1] * ocmp_ref[hh] + gates[:, 3 * hh + 1:3 * hh + 2] * o_sel[r]
                 + gates[:, 3 * hh + 2:3 * hh + 3] * o_win[r])
            out_ref[0, :, hh * LANES:(hh + 1) * LANES] = o


def _nsa_sample(q, gates, pages, table, cmp, win, newkv, newwin):
    s, tq, _ = q.shape
    n_pages = table.shape[1]
    ncp = cmp.shape[1]
    wb = win.shape[2]
    assert (n_pages * PAGE_SIZE) // SEL_BLOCK == N_BLK_PAD and tq <= SEL_BLOCK and tq & (tq - 1) == 0
    assert n_pages % SEL_PAGES == 0
    rows = N_HEADS * tq
    per_seq = lambda r, w: pl.BlockSpec((1, r, w), lambda b, j, t: (b, 0, 0))
    page_spec = lambda k: pl.BlockSpec((None, 2, LANES, PAGE_SIZE),
                                       lambda b, j, t: (t[b, j * SEL_PAGES + k], 1, 0, 0))
    grid_spec = pltpu.PrefetchScalarGridSpec(
        num_scalar_prefetch=1,
        grid=(s, n_pages // SEL_PAGES),
        in_specs=[per_seq(tq, QPAD), per_seq(tq, LANES)] + [page_spec(k) for k in range(SEL_PAGES)] + [
            per_seq(ncp, 2 * LANES), per_seq(2 * LANES, wb),
            per_seq(2 * LANES, PAGE_SIZE), per_seq(2 * LANES, PAGE_SIZE),
            pl.BlockSpec((N_BLK_PAD, ncp), lambda b, j, t: (0, 0))],
        out_specs=per_seq(tq, QPAD),
        scratch_shapes=[
            pltpu.VMEM((rows, LANES), F32),
            pltpu.VMEM((rows, ncp), F32),
            pltpu.VMEM((N_KV, LANES, N_BLK_PAD), BF16),
            pltpu.VMEM((rows, LANES), F32),
            pltpu.VMEM((rows, LANES), F32),
            pltpu.VMEM((rows, LANES), F32),
            pltpu.VMEM((N_HEADS, tq, LANES), F32),
        ],
    )
    return pl.pallas_call(
        functools.partial(_nsa_sample_kernel, n_pages=n_pages, tq=tq),
        grid_spec=grid_spec,
        out_shape=jax.ShapeDtypeStruct((s, tq, QPAD), F32),
        compiler_params=_cparams("parallel", "arbitrary"),
        name="nsa_sample",
    )(table, q, gates, *([pages] * SEL_PAGES), cmp, win, newkv, newwin, _cover_t(ncp))


S5_LANE_BLOCK = 512


def _s5_kernel(u_ref, z_ref, h0_ref, bbd_ref, lam_ref, cbd_ref, d_ref, y_ref, hl_ref,
               bu_ref, hs_ref, h_ref):
    c = pl.program_id(1)
    steps = u_ref.shape[1]

    @pl.when(c == 0)
    def _():
        h_ref[...] = h0_ref[0]

    u = u_ref[0]
    ub = u.astype(BF16)
    halves = [(slice(k * SSM_WIDTH // 2, (k + 1) * SSM_WIDTH // 2),
               [slice(p * N_STATE + k * N_STATE // 2, p * N_STATE + (k + 1) * N_STATE // 2) for p in range(2)])
              for k in range(2)]
    for ch, parts in halves:
        for st in parts:
            bu_ref[:, st] = _dot(ub[:, ch], bbd_ref[ch, st])
    for blk in range(N_STATE // S5_LANE_BLOCK):
        re = slice(blk * S5_LANE_BLOCK, (blk + 1) * S5_LANE_BLOCK)
        im = slice(N_STATE + blk * S5_LANE_BLOCK, N_STATE + (blk + 1) * S5_LANE_BLOCK)
        lr = lam_ref[:, re]
        li = lam_ref[:, im]

        def step(t, carry, re=re, im=im, lr=lr, li=li):
            hr, hi = carry
            row = pl.ds(t, 1)
            nr = lr * hr - li * hi + bu_ref[row, re]
            ni = lr * hi + li * hr + bu_ref[row, im]
            hs_ref[row, re] = nr
            hs_ref[row, im] = ni
            return nr, ni

        hr, hi = lax.fori_loop(0, steps, step, (h_ref[:, re], h_ref[:, im]),
                               unroll=min(8, steps))
        h_ref[:, re] = hr
        h_ref[:, im] = hi
    for ch, parts in halves:
        y = sum(_dot(hs_ref[:, st].astype(BF16), cbd_ref[st, ch]) for st in parts)
        y = y + d_ref[:, ch] * u[:, ch]
        y_ref[0, :, ch] = (jax.nn.gelu(y) * jax.nn.sigmoid(z_ref[0, :, ch])).astype(y_ref.dtype)

    @pl.when(c == pl.num_programs(1) - 1)
    def _():
        hl_ref[0] = h_ref[...]


def _s5_weights(lam_re, lam_im, log_dt, b_re, b_im, c_re, c_im, d_skip):
    lam = lax.complex(lam_re.astype(F32), lam_im.astype(F32))
    dt = jnp.exp(log_dt.astype(F32))[:, None]
    lam_bar = jnp.exp(lam * dt)
    b_bar = ((lam_bar - 1.0) / lam)[..., None] * lax.complex(b_re.astype(F32), b_im.astype(F32))
    eye = jnp.eye(N_SSM_GROUPS, dtype=F32)
    def in_bd(b):
        return jnp.einsum("gph,gk->ghkp", b, eye).reshape(SSM_WIDTH, N_STATE)
    bbd = jnp.concatenate([in_bd(b_bar.real), in_bd(b_bar.imag)], axis=1)
    def out_bd(cm):
        return jnp.einsum("ghp,gk->gpkh", cm, eye).reshape(N_STATE, SSM_WIDTH)
    cbd = jnp.concatenate([out_bd(c_re.astype(F32)), -out_bd(c_im.astype(F32))], axis=0)
    lam_row = jnp.concatenate([lam_bar.real.reshape(1, N_STATE), lam_bar.imag.reshape(1, N_STATE)], axis=1)
    return bbd.astype(BF16), lam_row, cbd.astype(BF16), d_skip.astype(F32).reshape(1, SSM_WIDTH)


def _s5(u, z, h0, s5w, chunk, out_dtype):
    bbd, lam_row, cbd, d_row = s5w
    s, t, _ = u.shape
    const = lambda shape: pl.BlockSpec(shape, lambda b, c: (0,) * len(shape))
    return pl.pallas_call(
        _s5_kernel,
        grid=(s, t // chunk),
        in_specs=[
            pl.BlockSpec((1, chunk, SSM_WIDTH), lambda b, c: (b, c, 0)),
            pl.BlockSpec((1, chunk, SSM_WIDTH), lambda b, c: (b, c, 0)),
            pl.BlockSpec((1, 1, 2 * N_STATE), lambda b, c: (b, 0, 0)),
            const((SSM_WIDTH, 2 * N_STATE)), const((1, 2 * N_STATE)),
            const((2 * N_STATE, SSM_WIDTH)), const((1, SSM_WIDTH)),
        ],
        out_specs=[
            pl.BlockSpec((1, chunk, SSM_WIDTH), lambda b, c: (b, c, 0)),
            pl.BlockSpec((1, 1, 2 * N_STATE), lambda b, c: (b, 0, 0)),
        ],
        out_shape=[jax.ShapeDtypeStruct((s, t, SSM_WIDTH), out_dtype),
                   jax.ShapeDtypeStruct((s, 1, 2 * N_STATE), F32)],
        scratch_shapes=[pltpu.VMEM((chunk, 2 * N_STATE), F32),
                        pltpu.VMEM((chunk, 2 * N_STATE), F32),
                        pltpu.VMEM((1, 2 * N_STATE), F32)],
        compiler_params=_cparams("parallel", "arbitrary"),
        name="s5",
    )(u, z, h0, bbd, lam_row, cbd, d_row)


def _outproj_kernel(x_ref, a_ref, s_ref, wa_ref, ws_ref, y_ref):
    y = x_ref[...]
    y = y + _dot(a_ref[...].astype(BF16), wa_ref[...])
    y = y + _dot(s_ref[...].astype(BF16), ws_ref[...])
    y_ref[...] = y


def _outproj_weights(w_out):
    wa = w_out[:NSA_WIDTH].reshape(N_KV, HPG, 1, HEAD_DIM, D_MODEL)
    slot = jnp.eye(N_KV, dtype=F32).reshape(N_KV, 1, N_KV, 1, 1)
    return (wa * slot).reshape(QPAD, D_MODEL).astype(BF16), w_out[NSA_WIDTH:].astype(BF16)


def _outproj(x2d, a_out, s_out, wa, ws, tm):
    n = x2d.shape[0]
    row = lambda w: pl.BlockSpec((tm, w), lambda i: (i, 0))
    return pl.pallas_call(
        _outproj_kernel,
        grid=(n // tm,),
        in_specs=[row(D_MODEL), row(QPAD), row(SSM_WIDTH),
                  pl.BlockSpec((QPAD, D_MODEL), lambda i: (0, 0)),
                  pl.BlockSpec((SSM_WIDTH, D_MODEL), lambda i: (0, 0))],
        out_specs=row(D_MODEL),
        out_shape=jax.ShapeDtypeStruct((n, D_MODEL), F32),
        compiler_params=_cparams("parallel"),
        name="outproj",
    )(x2d, a_out, s_out, wa, ws)


PEER_HALF = PEER_HEADS * PEER_DK // 2
_CAND_COUNTS = tuple(PEER_TOPK // (a + 1) for a in range(PEER_TOPK))
N_CAND = sum(_CAND_COUNTS)
N_CAND_PAD = -(-N_CAND // 8) * 8


def _peer_route_kernel(y_ref, g_ref, wq_ref, k1_ref, k2_ref, xn_ref, c1_ref, e1_ref, r2_ref, e2_ref,
                       s1_ref, s2_ref, v1_ref, v2_ref, cand_ref, rank1_ref, rank2_ref, rankc_ref):
    xn = _rmsnorm(y_ref[...], g_ref[...]).astype(BF16)
    xn_ref[...] = xn
    q = _dot(xn, wq_ref[...]).astype(BF16)
    s1_ref[...] = _nt_dot(k1_ref[...], q[:, :PEER_HALF])
    s2_ref[...] = _nt_dot(k2_ref[...], q[:, PEER_HALF:])
    t = y_ref.shape[0]
    krows = lambda h: slice(h * N_KEYS, (h + 1) * N_KEYS)
    crows = lambda h: slice(h * N_CAND_PAD, (h + 1) * N_CAND_PAD)

    def key_ranks(exact):
        worst = jnp.zeros((1, t), F32)
        for h in range(PEER_HEADS):
            for s_ref, rank_ref, v_ref in ((s1_ref, rank1_ref, v1_ref), (s2_ref, rank2_ref, v2_ref)):
                rank = _take_top(s_ref[krows(h), :], PEER_TOPK, v_ref, h * PEER_TOPK, exact)
                rank_ref[krows(h), :] = rank
                worst = jnp.maximum(worst, _n_taken(rank, PEER_TOPK))
        return worst

    worst = key_ranks(False)

    @pl.when(jnp.max(worst) > float(PEER_TOPK))
    def _():
        key_ranks(True)

    for h in range(PEER_HEADS):
        v1 = v1_ref[h * PEER_TOPK:(h + 1) * PEER_TOPK, :]
        v2 = v2_ref[h * PEER_TOPK:(h + 1) * PEER_TOPK, :]
        off = h * N_CAND_PAD
        for a, nb in enumerate(_CAND_COUNTS):
            cand_ref[off:off + nb, :] = v1[a:a + 1, :] + v2[0:nb, :]
            off += nb
        cand_ref[off:(h + 1) * N_CAND_PAD, :] = jnp.full(((h + 1) * N_CAND_PAD - off, t), LOWEST, F32)

    def cand_ranks(exact):
        worst = jnp.zeros((1, t), F32)
        for h in range(PEER_HEADS):
            rank = _take_top(cand_ref[crows(h), :], PEER_TOPK, exact=exact)
            rankc_ref[crows(h), :] = rank
            worst = jnp.maximum(worst, _n_taken(rank, PEER_TOPK))
        return worst

    worst = cand_ranks(False)

    @pl.when(jnp.max(worst) > float(PEER_TOPK))
    def _():
        cand_ranks(True)

    for h in range(PEER_HEADS):
        rows = krows(h)
        top = slice(h * PEER_TOPK, h * PEER_TOPK + 1)
        cand = cand_ref[crows(h), :]
        taken = jnp.where(rankc_ref[crows(h), :] < float(PEER_TOPK), 1.0, 0.0)
        z = jnp.sum(taken * jnp.exp(cand - cand[0:1, :]), axis=0, keepdims=True)
        rank1 = rank1_ref[rows, :]
        count = jnp.zeros((N_KEYS, t), F32)
        off = 0
        for a, nb in enumerate(_CAND_COUNTS):
            n_a = jnp.sum(taken[off:off + nb, :], axis=0, keepdims=True)
            count = jnp.where(rank1 == float(a), n_a, count)
            off += nb
        c1_ref[rows, :] = count
        e1_ref[rows, :] = jnp.exp(s1_ref[rows, :] - v1_ref[top, :]) / z
        r2_ref[rows, :] = rank2_ref[rows, :].astype(BF16)
        e2_ref[rows, :] = jnp.exp(s2_ref[rows, :] - v2_ref[top, :]).astype(BF16)


def _peer_weights(w_q, sub_k1, sub_k2):
    wq = w_q.reshape(D_MODEL, PEER_HEADS, 2, PEER_DK // 2).transpose(0, 2, 1, 3).reshape(D_MODEL, 2 * PEER_HALF)
    eye = jnp.eye(PEER_HEADS, dtype=F32)
    bd = lambda k: jnp.einsum("hkd,hj->hkjd", k, eye).reshape(PEER_HEADS * N_KEYS, PEER_HALF)
    return wq.astype(BF16), bd(sub_k1).astype(BF16), bd(sub_k2).astype(BF16)


def _peer_route(y2d, norm_g, wq, k1bd, k2bd, tm):
    n = y2d.shape[0]
    hk = PEER_HEADS * N_KEYS
    const = lambda a: pl.BlockSpec(a.shape, lambda i: (0, 0))
    col = lambda r: pl.BlockSpec((r, tm), lambda i: (0, i))
    tshape = lambda dt: jax.ShapeDtypeStruct((hk, n), dt)
    return pl.pallas_call(
        _peer_route_kernel,
        grid=(n // tm,),
        in_specs=[pl.BlockSpec((tm, D_MODEL), lambda i: (i, 0)),
                  pl.BlockSpec((1, D_MODEL), lambda i: (0, 0)), const(wq), const(k1bd), const(k2bd)],
        out_specs=[pl.BlockSpec((tm, D_MODEL), lambda i: (i, 0)), col(hk), col(hk), col(hk), col(hk)],
        out_shape=[jax.ShapeDtypeStruct((n, D_MODEL), BF16),
                   tshape(F32),
                   tshape(F32),
                   tshape(BF16),
                   tshape(BF16)],
        scratch_shapes=[pltpu.VMEM((hk, tm), F32), pltpu.VMEM((hk, tm), F32),
                        pltpu.VMEM((PEER_HEADS * PEER_TOPK, tm), F32),
                        pltpu.VMEM((PEER_HEADS * PEER_TOPK, tm), F32),
                        pltpu.VMEM((PEER_HEADS * N_CAND_PAD, tm), F32),
                        pltpu.VMEM((hk, tm), F32), pltpu.VMEM((hk, tm), F32),
                        pltpu.VMEM((PEER_HEADS * N_CAND_PAD, tm), F32)],
        compiler_params=_cparams("parallel"),
        name="peer_route",
    )(y2d, norm_g.reshape(1, D_MODEL), wq, k1bd, k2bd)


EXPERT_BLOCK = 1024
I1_PER_BLOCK = EXPERT_BLOCK // N_KEYS
N_EXPERT_BLOCKS = N_EXPERTS // EXPERT_BLOCK


def _peer_dense_kernel(y_ref, xn_ref, u_ref, vt_ref, c1_ref, e1_ref, r2_ref, e2_ref,
                       gf_ref, out_ref, acc_ref, act_ref, ga_ref):
    j = pl.program_id(1)
    t = xn_ref.shape[0]

    @pl.when(j == 0)
    def _():
        acc_ref[...] = jnp.zeros(acc_ref.shape, F32)
        act_ref[...] = jnp.zeros(act_ref.shape, BF16)
        ga_ref[...] = jnp.zeros(ga_ref.shape, BF16)

    def gate_stage(prev):
        jb = j - 1
        for ii in range(I1_PER_BLOCK):
            erows = slice(ii * N_KEYS, (ii + 1) * N_KEYS)
            for tc in range(t // LANES):
                cols = slice(tc * LANES, (tc + 1) * LANES)
                gate = jnp.zeros((N_KEYS, LANES), BF16)
                for h in range(PEER_HEADS):
                    grp = pl.ds(pl.multiple_of(h * N_KEYS + jb * I1_PER_BLOCK, I1_PER_BLOCK), I1_PER_BLOCK)
                    krows = slice(h * N_KEYS, (h + 1) * N_KEYS)
                    count = c1_ref[grp, cols][ii:ii + 1, :].astype(BF16)
                    e1 = e1_ref[grp, cols][ii:ii + 1, :].astype(BF16)
                    gate = gate + jnp.where(r2_ref[krows, cols] < count, e2_ref[krows, cols] * e1,
                                            jnp.zeros((), BF16))
                ga_ref[prev, erows, cols] = gate * act_ref[prev, erows, cols]

    def stages(cur, prev):
        pl.when((j >= 1) & (j <= N_EXPERT_BLOCKS))(functools.partial(gate_stage, prev))

        acc_ref[...] += _dot(vt_ref[...], ga_ref[cur])

        act_ref[cur] = _gelu_tanh(_nt_dot(u_ref[...], xn_ref[...])).astype(BF16)

    for parity in range(2):
        pl.when(j % 2 == parity)(functools.partial(stages, parity, 1 - parity))

    @pl.when(j == pl.num_programs(1) - 1)
    def _():
        y = y_ref[...] + acc_ref[...].T
        out_ref[...] = _rmsnorm(y, gf_ref[...])


def _peer_dense(y2d, xn, u_bf, vt_bf, c1, e1, r2, e2, norm_f, tm):
    n = y2d.shape[0]
    hk = PEER_HEADS * N_KEYS
    tok = lambda w: pl.BlockSpec((tm, w), lambda i, j: (i, 0))
    col = lambda r: pl.BlockSpec((r, tm), lambda i, j: (0, i))
    last = N_EXPERT_BLOCKS - 1
    return pl.pallas_call(
        _peer_dense_kernel,
        grid=(n // tm, N_EXPERT_BLOCKS + 2),
        in_specs=[tok(D_MODEL), tok(D_MODEL),
                  pl.BlockSpec((EXPERT_BLOCK, D_MODEL), lambda i, j: (jnp.minimum(j, last), 0)),
                  pl.BlockSpec((D_MODEL, EXPERT_BLOCK), lambda i, j: (0, jnp.maximum(j - 2, 0))),
                  col(hk), col(hk), col(hk), col(hk),
                  pl.BlockSpec((1, D_MODEL), lambda i, j: (0, 0))],
        out_specs=tok(D_MODEL),
        out_shape=jax.ShapeDtypeStruct((n, D_MODEL), F32),
        scratch_shapes=[pltpu.VMEM((D_MODEL, tm), F32),
                        pltpu.VMEM((2, EXPERT_BLOCK, tm), BF16),
                        pltpu.VMEM((2, EXPERT_BLOCK, tm), BF16)],
        compiler_params=_cparams("parallel", "arbitrary"),
        name="peer_dense",
    )(y2d, xn, u_bf, vt_bf, c1, e1, r2, e2, norm_f.reshape(1, D_MODEL))


PROJ_TILE = 512
PEER_TILE = 256
PEER_DENSE_TILE = 512
S5_CHUNK = 256


def _state_to_rows(st):
    s = st.shape[0]
    return jnp.concatenate([st[..., 0].reshape(s, 1, N_STATE), st[..., 1].reshape(s, 1, N_STATE)], axis=-1)


def _rows_to_state(h):
    s = h.shape[0]
    shape = (s, N_SSM_GROUPS, SSM_STATE)
    return jnp.stack([h[:, 0, :N_STATE].reshape(shape), h[:, 0, N_STATE:].reshape(shape)], axis=-1)


def _feat_to_tokens(a, n_types):
    s, _, t = a.shape
    return a.reshape(s, n_types, N_KV, HEAD_DIM, t).transpose(0, 4, 1, 2, 3)


def _peer_block(y2d, norm_g, peer_w, norm_f, tm):
    wq, k1bd, k2bd, u_bf, vt_bf = peer_w
    xn, c1, e1, r2, e2 = _peer_route(y2d, norm_g, wq, k1bd, k2bd, tm)
    return _peer_dense(y2d, xn, u_bf, vt_bf, c1, e1, r2, e2, norm_f, min(PEER_DENSE_TILE, y2d.shape[0]))


def kernel(x_prompt, x_sample, cache_kv, cache_win, state_ssm, page_table, norm_mix, w_in, w_cmp1, w_cmp2, pe_cmp, lam_re, lam_im, log_dt, b_re, b_im, c_re, c_im, d_skip, w_out, norm_ffn, w_q_peer, sub_k1, sub_k2, u_tab, v_tab, norm_final):
    b, t, d = x_prompt.shape
    db, ts, _ = x_sample.shape
    assert w_in.shape[0] == DEPTH == 1 and d == D_MODEL
    l = 0
    n_pool = cache_kv.shape[1]
    wb = cache_win.shape[2]

    w_tok, w_feat = _proj_weights(w_in[l])
    cmp_w = _compress_weights(w_cmp1[l], w_cmp2[l], pe_cmp[l])
    s5_w = _s5_weights(lam_re[l], lam_im[l], log_dt[l], b_re[l], b_im[l], c_re[l], c_im[l], d_skip[l])
    wa, ws = _outproj_weights(w_out[l])
    peer_w = _peer_weights(w_q_peer[l], sub_k1[l], sub_k2[l]) + (
        u_tab[l].astype(BF16), v_tab[l].T.astype(BF16))

    q, gate, u, z, kswk, q_t, kvsel_t, ksvs_t, kvwin_t, kwvw_t = _project(
        x_prompt, norm_mix[l], w_tok, w_feat, PROJ_TILE, BF16)
    n_pg = t // PAGE_SIZE
    table_p = jnp.broadcast_to(jnp.arange(n_pg, dtype=jnp.int32), (b, n_pg))
    prompt_page = lambda k: pl.BlockSpec((None, 2, LANES, PAGE_SIZE),
                                         lambda bb, j, tb: (bb, 0, 0, tb[bb, j * CMP_PAGES + k]))
    cmp_p = _compress(kvsel_t.reshape(b, 4, LANES, t), prompt_page, table_p, *cmp_w)
    a_p = _nsa_prompt_t(q.reshape(b, t, QPAD), q_t, gate.reshape(b, t, LANES),
                        kswk.reshape(b, t, 2 * KV_WIDTH), ksvs_t, kwvw_t, cmp_p)
    s_p, h_p = _s5(u.reshape(b, t, SSM_WIDTH), z.reshape(b, t, SSM_WIDTH),
                   jnp.zeros((b, 1, 2 * N_STATE), F32), s5_w, S5_CHUNK, BF16)
    y1p = _outproj(x_prompt.reshape(b * t, d), a_p.reshape(b * t, QPAD), s_p.reshape(b * t, SSM_WIDTH),
                   wa, ws, PROJ_TILE)
    y_prompt = _peer_block(y1p, norm_ffn[l], peer_w, norm_final, PEER_TILE).reshape(b, t, d)
    kv_prompt = _feat_to_tokens(kvsel_t, 4)
    win_prompt = _feat_to_tokens(kvwin_t[:, :, t - min(WINDOW, t):], 2)
    ssm_prompt = _rows_to_state(h_p)

    qs, gate_s, u_s, z_s, _, _, kvsel_st, ksvs_st, kvwin_st, kwvw_st = _project(
        x_sample.reshape(1, db * ts, d), norm_mix[l], w_tok, w_feat, db * ts, F32)
    pages_s = cache_kv[l].transpose(0, 2, 3, 4, 1).reshape(n_pool, 4, LANES, PAGE_SIZE)
    sample_page = lambda k: pl.BlockSpec((None, 2, LANES, PAGE_SIZE),
                                         lambda bb, j, tb: (tb[bb, j * CMP_PAGES + k], 0, 0, 0))
    cmp_s = _compress(pages_s, sample_page, page_table, *cmp_w)
    new_page = lambda a: jnp.pad(a[0].reshape(2 * LANES, db, ts).transpose(1, 0, 2),
                                 ((0, 0), (0, 0), (0, PAGE_SIZE - ts)))
    win_t = cache_win[l].transpose(0, 2, 3, 4, 1).reshape(db, 2 * LANES, wb)
    a_s = _nsa_sample(qs.reshape(db, ts, QPAD), gate_s.reshape(db, ts, LANES), pages_s, page_table, cmp_s,
                      win_t, new_page(ksvs_st), new_page(kwvw_st))
    s_s, h_s = _s5(u_s.reshape(db, ts, SSM_WIDTH), z_s.reshape(db, ts, SSM_WIDTH),
                   _state_to_rows(state_ssm[l].astype(F32)), s5_w, ts, F32)
    y1s = _outproj(x_sample.reshape(db * ts, d), a_s.reshape(db * ts, QPAD), s_s.reshape(db * ts, SSM_WIDTH),
                   wa, ws, db * ts)
    y_sample = _peer_block(y1s, norm_ffn[l], peer_w, norm_final, PEER_TILE).reshape(db, ts, d)
    per_tok = lambda a, n_types: a[0].reshape(n_types, N_KV, HEAD_DIM, db, ts).transpose(3, 4, 0, 1, 2)
    kv_sample = per_tok(kvsel_st, 4)
    win_new = per_tok(kvwin_st, 2).astype(cache_win.dtype)
    win_sample = jnp.concatenate([cache_win[l], win_new], axis=1)[:, ts:]
    ssm_sample = _rows_to_state(h_s)

    return (y_prompt, y_sample, kv_prompt[None], kv_sample[None], win_prompt[None], win_sample[None],
            ssm_prompt[None], ssm_sample[None])
```

```python
import functools
import math

import jax
import jax.numpy as jnp
from jax import lax
from jax.experimental import pallas as pl
from jax.experimental.pallas import tpu as pltpu

D_MODEL = 1024
DEPTH = 1
PAGE_SIZE = 128
N_HEADS = 8
N_KV = 2
HEAD_DIM = 64
HPG = N_HEADS // N_KV
CMP_LEN = 32
CMP_STRIDE = 16
SEL_BLOCK = 64
N_SEL = 16
WINDOW = 512
SSM_WIDTH = 512
SSM_GROUP = 16
N_SSM_GROUPS = SSM_WIDTH // SSM_GROUP
SSM_STATE = 64
N_KEYS = 128
N_EXPERTS = N_KEYS * N_KEYS
PEER_HEADS = 8
PEER_DK = 128
PEER_TOPK = 16
NSA_WIDTH = N_HEADS * HEAD_DIM
KV_WIDTH = N_KV * HEAD_DIM
GATE_WIDTH = 3 * N_HEADS
EPS = 1e-6
NEG = -1e30
FORCE = 1e4
LOG2_E = 1.0 / math.log(2.0)
LOWEST = -3.0e38

LANES = 128
BF16_ROWS = 16
VMEM_LIMIT = 56 * 1024 * 1024

N_STATE = N_SSM_GROUPS * SSM_STATE
QPAD = N_HEADS * LANES
F32 = jnp.float32
BF16 = jnp.bfloat16

assert KV_WIDTH == LANES and PAGE_SIZE == LANES


def _cparams(*sem):
    return pltpu.CompilerParams(dimension_semantics=sem, vmem_limit_bytes=VMEM_LIMIT)


def _nt_dot(a, b):
    return lax.dot_general(a, b, (((1,), (1,)), ((), ())), preferred_element_type=F32)


def _dot(a, b):
    return jnp.dot(a, b, preferred_element_type=F32)


def _rmsnorm(x, g):
    return x * lax.rsqrt(jnp.mean(x * x, axis=-1, keepdims=True) + EPS) * g


_GELU_C1 = 2.0 * math.sqrt(2.0 / math.pi) * LOG2_E
_GELU_C3 = _GELU_C1 * 0.044715


def _gelu_tanh(x):
    return x / (1.0 + jnp.exp2(x * (-_GELU_C1 - _GELU_C3 * (x * x))))


_PROJ_COLS = (("q", QPAD), ("gate", LANES), ("u", SSM_WIDTH), ("z", SSM_WIDTH), ("kswk", 2 * KV_WIDTH))
KVT_ROWS = 6 * KV_WIDTH


def _proj_kernel(x_ref, g_ref, w_ref, wt_ref, q_ref, gate_ref, u_ref, z_ref, kswk_ref,
                 qt_ref, kvsel_ref, ksvs_ref, kvwin_ref, kwvw_ref):
    xn = _rmsnorm(x_ref[...], g_ref[...]).astype(BF16)
    off = 0
    for (name, width), ref in zip(_PROJ_COLS, (q_ref, gate_ref, u_ref, z_ref, kswk_ref)):
        ref[...] = _dot(xn, w_ref[:, off:off + width]).astype(ref.dtype)
        off += width
    qt_ref[0] = _nt_dot(wt_ref[0:QPAD, :], xn).astype(BF16)
    kvt = _nt_dot(wt_ref[QPAD:, :], xn)
    kvsel_ref[0] = kvt[0:4 * KV_WIDTH]
    ksvs_ref[0] = kvt[2 * KV_WIDTH:4 * KV_WIDTH].astype(BF16)
    kvwin_ref[0] = kvt[4 * KV_WIDTH:]
    kwvw_ref[0] = kvt[4 * KV_WIDTH:].astype(BF16)


def _proj_weights(w_in):
    c0 = NSA_WIDTH
    c1 = c0 + 6 * KV_WIDTH
    c2 = c1 + GATE_WIDTH
    wq = w_in[:, :c0].reshape(D_MODEL, N_KV, HPG, 1, HEAD_DIM) * (HEAD_DIM ** -0.5 * LOG2_E)
    slot = jnp.eye(N_KV, dtype=F32).reshape(1, N_KV, 1, N_KV, 1)
    wq_pad = (wq * slot).reshape(D_MODEL, QPAD)
    wg = jnp.pad(w_in[:, c1:c2], ((0, 0), (0, LANES - GATE_WIDTH)))
    ks = w_in[:, c0 + 2 * KV_WIDTH:c0 + 3 * KV_WIDTH]
    kw = w_in[:, c0 + 4 * KV_WIDTH:c0 + 5 * KV_WIDTH]
    w = jnp.concatenate([wq_pad, wg, w_in[:, c2:], ks, kw], axis=1)
    wt = jnp.concatenate([wq_pad, w_in[:, c0:c1]], axis=1).T
    return w.astype(BF16), wt.astype(BF16)


def _project(x3d, norm_g, w_tok, w_feat, tm, q_dtype):
    s, t, _ = x3d.shape
    nt = t // tm
    n = s * t
    row = lambda w: pl.BlockSpec((tm, w), lambda b, i: (b * nt + i, 0))
    feat = lambda r: pl.BlockSpec((1, r, tm), lambda b, i: (b, 0, i))
    const = lambda a: pl.BlockSpec(a.shape, lambda b, i: (0, 0))
    shapes = [
        jax.ShapeDtypeStruct((n, QPAD), q_dtype),
        jax.ShapeDtypeStruct((n, LANES), F32),
        jax.ShapeDtypeStruct((n, SSM_WIDTH), F32),
        jax.ShapeDtypeStruct((n, SSM_WIDTH), F32),
        jax.ShapeDtypeStruct((n, 2 * KV_WIDTH), BF16),
        jax.ShapeDtypeStruct((s, QPAD, t), BF16),
        jax.ShapeDtypeStruct((s, 4 * KV_WIDTH, t), F32),
        jax.ShapeDtypeStruct((s, 2 * KV_WIDTH, t), BF16),
        jax.ShapeDtypeStruct((s, 2 * KV_WIDTH, t), F32),
        jax.ShapeDtypeStruct((s, 2 * KV_WIDTH, t), BF16),
    ]
    g2 = norm_g.reshape(1, D_MODEL)
    return pl.pallas_call(
        _proj_kernel,
        grid=(s, nt),
        in_specs=[row(D_MODEL), const(g2), const(w_tok), const(w_feat)],
        out_specs=[row(sh.shape[1]) for sh in shapes[:5]] + [feat(sh.shape[1]) for sh in shapes[5:]],
        out_shape=shapes,
        compiler_params=_cparams("parallel", "parallel"),
        name="proj",
    )(x3d.reshape(n, D_MODEL), g2, w_tok, w_feat)


CMP_PAGES = 8


def _compress_kernel(tbl_ref, *refs, n_pages):
    del tbl_ref
    page_refs = refs[:CMP_PAGES]
    w1_ref, pe_ref, w2_ref, out_ref, slabk_ref, slabv_ref = refs[CMP_PAGES:]
    j = pl.program_id(1)
    seq = n_pages * PAGE_SIZE
    slabs = (slabk_ref, slabv_ref)

    for k, page in enumerate(page_refs):
        rows = pl.ds(pl.multiple_of((j * CMP_PAGES + k) * PAGE_SIZE, PAGE_SIZE), PAGE_SIZE)
        for c, slab in enumerate(slabs):
            slab[rows, :] = page[c].T

    @pl.when(j == n_pages // CMP_PAGES - 1)
    def _():
        nb = seq // CMP_STRIDE
        for c, slab in enumerate(slabs):
            head = jnp.zeros((nb, LANES), F32)
            tail = jnp.zeros((nb, LANES), F32)
            for s in range(CMP_STRIDE):
                x = slab[pl.ds(s, nb, stride=CMP_STRIDE), :]
                head = head + _dot((x + pe_ref[c, s:s + 1, :]).astype(BF16), w1_ref[c, s])
                s2 = CMP_STRIDE + s
                tail = tail + _dot((x + pe_ref[c, s2:s2 + 1, :]).astype(BF16), w1_ref[c, s2])
            hid = head + pltpu.roll(tail, nb - 1, 0)
            out = _dot(jax.nn.gelu(hid).astype(BF16), w2_ref[c])
            out_ref[0, :, c * LANES:(c + 1) * LANES] = out.astype(BF16)


def _compress_weights(w1, w2, pe):
    eye = jnp.eye(N_KV, dtype=F32)
    bd = lambda m: jnp.einsum("...de,gk->...gdke", m, eye).reshape(m.shape[:-2] + (LANES, LANES))
    pe2 = jnp.concatenate([pe, pe], axis=-1)
    return bd(w1).astype(BF16), pe2, bd(w2).astype(BF16)


def _compress(pages, page_spec, table, w1bd, pe2, w2bd):
    s, p = table.shape
    assert CMP_LEN == 2 * CMP_STRIDE and p % CMP_PAGES == 0
    seq = p * PAGE_SIZE
    const = lambda a: pl.BlockSpec(a.shape, lambda b, j, t: (0,) * a.ndim)
    grid_spec = pltpu.PrefetchScalarGridSpec(
        num_scalar_prefetch=1,
        grid=(s, p // CMP_PAGES),
        in_specs=[page_spec(k) for k in range(CMP_PAGES)] + [const(w1bd), const(pe2), const(w2bd)],
        out_specs=pl.BlockSpec((1, seq // CMP_STRIDE, 2 * LANES), lambda b, j, t: (b, 0, 0)),
        scratch_shapes=[pltpu.VMEM((seq, LANES), F32), pltpu.VMEM((seq, LANES), F32)],
    )
    return pl.pallas_call(
        functools.partial(_compress_kernel, n_pages=p),
        grid_spec=grid_spec,
        out_shape=jax.ShapeDtypeStruct((s, seq // CMP_STRIDE, 2 * LANES), BF16),
        compiler_params=_cparams("parallel", "arbitrary"),
        name="compress",
    )(table, *([pages] * CMP_PAGES), w1bd, pe2, w2bd)


Q_TILE = 128
KEY_CHUNK = 1024
N_BLK_PAD = 128
LOG2_SEL_BLOCK = int(math.log2(SEL_BLOCK))
WIN_KEYS = WINDOW + Q_TILE


def _cover_t(n_cmp_pad):
    n = jnp.arange(n_cmp_pad, dtype=jnp.int32)[None, :] * CMP_STRIDE
    s = jnp.arange(N_BLK_PAD, dtype=jnp.int32)[:, None] * SEL_BLOCK
    return ((n < s + SEL_BLOCK) & (n + CMP_LEN > s)).astype(BF16)


def _split3_nt(w, x):
    hi = x.astype(BF16)
    r1 = x - hi.astype(F32)
    mid = r1.astype(BF16)
    lo = (r1 - mid.astype(F32)).astype(BF16)
    return _nt_dot(w, hi) + _nt_dot(w, mid) + _nt_dot(w, lo)


def _take_top(s, n, out_ref=None, out_row=0, exact=True):
    kidx = lax.broadcasted_iota(jnp.int32, s.shape, 0).astype(F32)
    rem = s
    rank = jnp.full(s.shape, float(n), F32)
    for a in range(n):
        mx = jnp.max(rem, axis=0, keepdims=True)
        if exact:
            first = jnp.min(jnp.where(rem == mx, kidx, float(s.shape[0])), axis=0, keepdims=True)
            taken = kidx == first
        else:
            taken = rem == mx
        if out_ref is not None:
            out_ref[out_row + a:out_row + a + 1, :] = mx
        rank = jnp.where(taken, float(a), rank)
        rem = jnp.where(taken, LOWEST, rem)
    return rank


def _n_taken(rank, n):
    return jnp.sum(jnp.where(rank < float(n), 1.0, 0.0), axis=0, keepdims=True)


def _topk_block_mask(imp_t, n_keep):
    return jnp.where(_take_top(imp_t, n_keep) < float(n_keep), 0.0, NEG)


def _cmp_branch(q, s_ref, cmp_ref, ocmp_ref, t_pos, rows):
    ncp = cmp_ref.shape[1]
    kc = cmp_ref[0, :, 0:LANES]
    vc = cmp_ref[0, :, LANES:2 * LANES]
    s_ref[:, 0:ncp] = _nt_dot(q, kc)
    n_idx = lax.broadcasted_iota(jnp.int32, (rows, ncp), 1)
    ok = (n_idx * CMP_STRIDE + (CMP_LEN - 1) <= t_pos) & (n_idx < ncp - 1)
    imps = []
    for g in range(N_KV):
        psum = jnp.zeros((rows, ncp), F32)
        for h in range(HPG):
            hh = g * HPG + h
            s = jnp.where(ok, s_ref[hh * rows:(hh + 1) * rows, 0:ncp], NEG)
            m = jnp.max(s, axis=-1, keepdims=True)
            e = jnp.where(ok, jnp.exp2(s - m), 0.0)
            l = jnp.sum(e, axis=-1, keepdims=True)
            p = e * jnp.where(l > 0.0, 1.0 / l, 0.0)
            ocmp_ref[hh] = _dot(p.astype(BF16), vc)
            psum = psum + p
        imps.append(psum)
    return imps


def _with_ones_rows(vt, g):
    row = lax.broadcasted_iota(jnp.int32, vt.shape, 0)
    own = (row >= g * HEAD_DIM) & (row < (g + 1) * HEAD_DIM)
    return jnp.where(own, vt, jnp.ones((), vt.dtype))


def _block_expand(t):
    key_blk = jnp.arange(t, dtype=jnp.int32)[None, :] // SEL_BLOCK
    return (jnp.arange(N_BLK_PAD, dtype=jnp.int32)[:, None] == key_blk).astype(BF16)


def _nsa_prompt_t_kernel(q_ref, qt_ref, gate_ref, ktok_ref, vs_ref, vw_ref, cmp_ref, covt_ref, expt_ref,
                         out_ref, qs_ref, qst_ref, sc_ref, s_ref, p_ref, m_ref, acc_ref, ocmp_ref):
    i = pl.program_id(1)
    t0 = i * Q_TILE
    group_cols = HPG * Q_TILE
    for h in range(N_HEADS):
        cols = slice(h * Q_TILE, (h + 1) * Q_TILE)
        qs_ref[cols, :] = q_ref[0, :, h * LANES:(h + 1) * LANES]
        qst_ref[0:LANES, cols] = qt_ref[0, h * LANES:(h + 1) * LANES, :]
    t_col = t0 + lax.broadcasted_iota(jnp.int32, (Q_TILE, 1), 0)

    psums = _cmp_branch(qs_ref[...], sc_ref, cmp_ref, ocmp_ref, t_col, Q_TILE)
    s_idx = lax.broadcasted_iota(jnp.int32, (N_BLK_PAD, Q_TILE), 0)
    t_row = t0 + lax.broadcasted_iota(jnp.int32, (N_BLK_PAD, Q_TILE), 1)
    cur = jnp.right_shift(t_row, LOG2_SEL_BLOCK)
    forced = (s_idx == 0) | (s_idx == cur) | (s_idx == cur - 1)
    for g in range(N_KV):
        imp_t = _split3_nt(covt_ref[...], psums[g])
        imp_t = jnp.where(forced, FORCE, imp_t)
        imp_t = jnp.where(s_idx * SEL_BLOCK <= t_row, imp_t, NEG)
        mask_t = _topk_block_mask(imp_t, N_SEL).astype(BF16)
        for h in range(HPG):
            hh = g * HPG + h
            qst_ref[LANES:2 * LANES, hh * Q_TILE:(hh + 1) * Q_TILE] = mask_t

    def attend(n_keys, v_aug, bias_t):
        for g in range(N_KV):
            alphas = []
            for h in range(HPG):
                hh = g * HPG + h
                s = s_ref[0:n_keys, hh * Q_TILE:(hh + 1) * Q_TILE]
                if bias_t is not None:
                    s = s + bias_t
                m_old = m_ref[hh:hh + 1, :]
                m_new = jnp.maximum(m_old, jnp.max(s, axis=0, keepdims=True))
                p_ref[0:n_keys, h * Q_TILE:(h + 1) * Q_TILE] = jnp.exp2(s - m_new).astype(BF16)
                alphas.append(jnp.exp2(m_old - m_new))
                m_ref[hh:hh + 1, :] = m_new
            alpha = jnp.concatenate(alphas, axis=1)
            acc_ref[g] = alpha * acc_ref[g] + _dot(v_aug[g], p_ref[0:n_keys, :])

    def reset():
        m_ref[...] = jnp.full(m_ref.shape, NEG, F32)
        acc_ref[...] = jnp.zeros(acc_ref.shape, F32)

    def outputs():
        outs = []
        for g in range(N_KV):
            lsum = (1 - g) * HEAD_DIM
            for h in range(HPG):
                a = acc_ref[g, :, h * Q_TILE:(h + 1) * Q_TILE]
                outs.append(a / a[lsum:lsum + 1, :])
        return outs

    reset()
    key_sub = lax.broadcasted_iota(jnp.int32, (KEY_CHUNK, Q_TILE), 0)
    t_lane = t0 + lax.broadcasted_iota(jnp.int32, (KEY_CHUNK, Q_TILE), 1)

    def chunk(c, diagonal):
        k0 = pl.multiple_of(c * KEY_CHUNK, KEY_CHUNK)
        keys = pl.ds(k0, KEY_CHUNK)
        lhs = jnp.concatenate([ktok_ref[0, keys, 0:LANES], expt_ref[keys, :]], axis=1)
        s_ref[0:KEY_CHUNK, :] = _dot(lhs, qst_ref[...])
        vt = vs_ref[0, LANES:2 * LANES, keys]
        causal = jnp.where(k0 + key_sub <= t_lane, 0.0, NEG) if diagonal else None
        attend(KEY_CHUNK, [_with_ones_rows(vt, g) for g in range(N_KV)], causal)

    n_full = t0 // KEY_CHUNK

    def full_chunk(c, carry):
        chunk(c, False)
        return carry

    lax.fori_loop(0, n_full, full_chunk, 0)
    chunk(n_full, True)
    o_sel = outputs()

    reset()
    start = pl.multiple_of(jnp.maximum(t0 - WINDOW, 0), Q_TILE)
    wkeys = pl.ds(start, WIN_KEYS)
    s_ref[0:WIN_KEYS, :] = _dot(ktok_ref[0, wkeys, LANES:2 * LANES], qst_ref[0:LANES, :])
    vwt = vw_ref[0, LANES:2 * LANES, wkeys]
    dist = (t0 + lax.broadcasted_iota(jnp.int32, (WIN_KEYS, Q_TILE), 1)
            - (start + lax.broadcasted_iota(jnp.int32, (WIN_KEYS, Q_TILE), 0)))
    bias_w = jnp.where((dist >= 0) & (dist <= WINDOW), 0.0, NEG)
    attend(WIN_KEYS, [_with_ones_rows(vwt, g) for g in range(N_KV)], bias_w)
    o_win = outputs()

    gates = jax.nn.sigmoid(gate_ref[0])
    gates_t = gates.T
    for hh in range(N_HEADS):
        o_t = gates_t[3 * hh + 1:3 * hh + 2, :] * o_sel[hh] + gates_t[3 * hh + 2:3 * hh + 3, :] * o_win[hh]
        o = gates[:, 3 * hh:3 * hh + 1] * ocmp_ref[hh] + o_t.T
        out_ref[0, :, hh * LANES:(hh + 1) * LANES] = o.astype(out_ref.dtype)


def _nsa_prompt_t(q, qt, gates, kswk, ksvs_t, kwvw_t, cmp):
    b, t, _ = q.shape
    ncp = cmp.shape[1]
    tile = lambda w: pl.BlockSpec((1, Q_TILE, w), lambda bb, i: (bb, i, 0))
    whole = lambda r, w: pl.BlockSpec((1, r, w), lambda bb, i: (bb, 0, 0))
    const = lambda r, w: pl.BlockSpec((r, w), lambda bb, i: (0, 0))
    rows = N_HEADS * Q_TILE
    expand_t = _block_expand(t).T
    return pl.pallas_call(
        _nsa_prompt_t_kernel,
        grid=(b, t // Q_TILE),
        in_specs=[tile(QPAD), pl.BlockSpec((1, QPAD, Q_TILE), lambda bb, i: (bb, 0, i)), tile(LANES),
                  whole(t, 2 * LANES), whole(2 * LANES, t), whole(2 * LANES, t),
                  whole(ncp, 2 * LANES), const(N_BLK_PAD, ncp), const(t, N_BLK_PAD)],
        out_specs=tile(QPAD),
        out_shape=jax.ShapeDtypeStruct((b, t, QPAD), BF16),
        scratch_shapes=[
            pltpu.VMEM((rows, LANES), BF16),
            pltpu.VMEM((2 * LANES, rows), BF16),
            pltpu.VMEM((rows, ncp), F32),
            pltpu.VMEM((max(WIN_KEYS, KEY_CHUNK), rows), F32),
            pltpu.VMEM((max(WIN_KEYS, KEY_CHUNK), HPG * Q_TILE), BF16),
            pltpu.VMEM((N_HEADS, Q_TILE), F32),
            pltpu.VMEM((N_KV, LANES, HPG * Q_TILE), F32),
            pltpu.VMEM((N_HEADS, Q_TILE, LANES), F32),
        ],
        compiler_params=_cparams("parallel", "arbitrary"),
        name="nsa_prompt",
    )(q, qt, gates, kswk, ksvs_t, kwvw_t, cmp, _cover_t(ncp), expand_t)


SEL_PAGES = 8


def _flash_update(s, vt, m_ref, l_ref, acc_ref):
    m_old = m_ref[...]
    m_new = jnp.maximum(m_old, jnp.max(s, axis=-1, keepdims=True))
    alpha = jnp.exp2(m_old - m_new)
    p = jnp.exp2(s - m_new[:, 0:1])
    l_ref[...] = alpha * l_ref[...] + jnp.sum(p, axis=-1, keepdims=True)
    acc_ref[...] = alpha * acc_ref[...] + _nt_dot(p.astype(BF16), vt)
    m_ref[...] = m_new


def _nsa_sample_kernel(tbl_ref, *refs, n_pages, tq):
    del tbl_ref
    q_ref, gate_ref = refs[:2]
    page_refs = refs[2:2 + SEL_PAGES]
    (cmp_ref, win_ref, newkv_ref, newwin_ref, covt_ref, out_ref, qs_ref, s_ref, msel_ref,
     m_ref, l_ref, acc_ref, ocmp_ref) = refs[2 + SEL_PAGES:]
    j = pl.program_id(1)
    past_len = n_pages * PAGE_SIZE
    rows = N_HEADS * tq
    step_keys = SEL_PAGES * PAGE_SIZE
    i_col = lax.broadcasted_iota(jnp.int32, (rows, 1), 0) & (tq - 1)

    @pl.when(j == 0)
    def _():
        for h in range(N_HEADS):
            qs_ref[h * tq:(h + 1) * tq, :] = q_ref[0, :, h * LANES:(h + 1) * LANES]
        t_col = past_len + lax.broadcasted_iota(jnp.int32, (tq, 1), 0)
        psums = _cmp_branch(qs_ref[...].astype(BF16), s_ref, cmp_ref, ocmp_ref, t_col, tq)
        s_idx = lax.broadcasted_iota(jnp.int32, (N_BLK_PAD, LANES), 0)
        t_row = past_len + lax.broadcasted_iota(jnp.int32, (N_BLK_PAD, LANES), 1)
        cur = jnp.right_shift(t_row, LOG2_SEL_BLOCK)
        forced = (s_idx == 0) | (s_idx == cur) | (s_idx == cur - 1)
        ncp = cmp_ref.shape[1]
        for g in range(N_KV):
            psum = jnp.concatenate([psums[g], jnp.zeros((LANES - tq, ncp), F32)], axis=0)
            imp_t = jnp.where(forced, FORCE, _split3_nt(covt_ref[...], psum))
            mask_t = _topk_block_mask(imp_t, N_SEL - 1)
            msel_ref[g] = mask_t.T.astype(BF16)
        m_ref[...] = jnp.full(m_ref.shape, NEG, F32)
        l_ref[...] = jnp.zeros(l_ref.shape, F32)
        acc_ref[...] = jnp.zeros(acc_ref.shape, F32)

    q = qs_ref[...].astype(BF16)
    kt = jnp.concatenate([page[0].astype(BF16) for page in page_refs], axis=1)
    vt = jnp.concatenate([page[1].astype(BF16) for page in page_refs], axis=1)
    blk_row = lax.broadcasted_iota(jnp.int32, (N_BLK_PAD, step_keys), 0)
    key_pos = j * step_keys + lax.broadcasted_iota(jnp.int32, (N_BLK_PAD, step_keys), 1)
    expand = jnp.where(blk_row == jnp.right_shift(key_pos, LOG2_SEL_BLOCK), 1.0, 0.0).astype(BF16)
    bias_g = [_dot(msel_ref[g], expand)[0:tq, :] for g in range(N_KV)]
    bias = jnp.concatenate([bias_g[hh // HPG] for hh in range(N_HEADS)], axis=0)
    _flash_update(_dot(q, kt) + bias, vt, m_ref, l_ref, acc_ref)

    @pl.when(j == n_pages // SEL_PAGES - 1)
    def _():
        j_lane = lax.broadcasted_iota(jnp.int32, (rows, PAGE_SIZE), 1)
        new_bias = jnp.where(j_lane <= i_col, 0.0, NEG)
        _flash_update(_dot(q, newkv_ref[0, 0:LANES, :]) + new_bias, newkv_ref[0, LANES:2 * LANES, :],
                      m_ref, l_ref, acc_ref)
        wb = win_ref.shape[2]
        dist = wb + i_col - lax.broadcasted_iota(jnp.int32, (rows, wb), 1)
        s1 = _dot(q, win_ref[0, 0:LANES, :].astype(BF16)) + jnp.where((dist >= 0) & (dist <= WINDOW), 0.0, NEG)
        s2 = _dot(q, newwin_ref[0, 0:LANES, :]) + new_bias
        m = jnp.maximum(jnp.max(s1, axis=-1, keepdims=True), jnp.max(s2, axis=-1, keepdims=True))
        p1 = jnp.exp2(s1 - m)
        p2 = jnp.exp2(s2 - m)
        l = jnp.sum(p1, axis=-1, keepdims=True) + jnp.sum(p2, axis=-1, keepdims=True)
        o_win = (_nt_dot(p1.astype(BF16), win_ref[0, LANES:2 * LANES, :].astype(BF16))
                 + _nt_dot(p2.astype(BF16), newwin_ref[0, LANES:2 * LANES, :])) / l
        o_sel = acc_ref[...] / l_ref[...]
        gates = jax.nn.sigmoid(gate_ref[0])
        for hh in range(N_HEADS):
            r = slice(hh * tq, (hh + 1) * tq)
            o = (gates[:, 3 * hh:3 * hh + 1] * ocmp_ref[hh] + gates[:, 3 * hh + 1:3 * hh + 2] * o_sel[r]
                 + gates[:, 3 * hh + 2:3 * hh + 3] * o_win[r])
            out_ref[0, :, hh * LANES:(hh + 1) * LANES] = o


def _nsa_sample(q, gates, pages, table, cmp, win, newkv, newwin):
    s, tq, _ = q.shape
    n_pages = table.shape[1]
    ncp = cmp.shape[1]
    wb = win.shape[2]
    assert (n_pages * PAGE_SIZE) // SEL_BLOCK == N_BLK_PAD and tq <= SEL_BLOCK and tq & (tq - 1) == 0
    assert n_pages % SEL_PAGES == 0
    rows = N_HEADS * tq
    per_seq = lambda r, w: pl.BlockSpec((1, r, w), lambda b, j, t: (b, 0, 0))
    page_spec = lambda k: pl.BlockSpec((None, 2, LANES, PAGE_SIZE),
                                       lambda b, j, t: (t[b, j * SEL_PAGES + k], 1, 0, 0))
    grid_spec = pltpu.PrefetchScalarGridSpec(
        num_scalar_prefetch=1,
        grid=(s, n_pages // SEL_PAGES),
        in_specs=[per_seq(tq, QPAD), per_seq(tq, LANES)] + [page_spec(k) for k in range(SEL_PAGES)] + [
            per_seq(ncp, 2 * LANES), per_seq(2 * LANES, wb),
            per_seq(2 * LANES, PAGE_SIZE), per_seq(2 * LANES, PAGE_SIZE),
            pl.BlockSpec((N_BLK_PAD, ncp), lambda b, j, t: (0, 0))],
        out_specs=per_seq(tq, QPAD),
        scratch_shapes=[
            pltpu.VMEM((rows, LANES), F32),
            pltpu.VMEM((rows, ncp), F32),
            pltpu.VMEM((N_KV, LANES, N_BLK_PAD), BF16),
            pltpu.VMEM((rows, LANES), F32),
            pltpu.VMEM((rows, LANES), F32),
            pltpu.VMEM((rows, LANES), F32),
            pltpu.VMEM((N_HEADS, tq, LANES), F32),
        ],
    )
    return pl.pallas_call(
        functools.partial(_nsa_sample_kernel, n_pages=n_pages, tq=tq),
        grid_spec=grid_spec,
        out_shape=jax.ShapeDtypeStruct((s, tq, QPAD), F32),
        compiler_params=_cparams("parallel", "arbitrary"),
        name="nsa_sample",
    )(table, q, gates, *([pages] * SEL_PAGES), cmp, win, newkv, newwin, _cover_t(ncp))


S5_LANE_BLOCK = 2048


def _s5_kernel(u_ref, z_ref, h0_ref, bbd_ref, lam_ref, cbd_ref, d_ref, y_ref, hl_ref,
               bu_ref, hs_ref, h_ref):
    c = pl.program_id(1)
    steps = u_ref.shape[1]

    @pl.when(c == 0)
    def _():
        h_ref[...] = h0_ref[0]

    u = u_ref[0]
    ub = u.astype(BF16)
    halves = [(slice(k * SSM_WIDTH // 2, (k + 1) * SSM_WIDTH // 2),
               [slice(p * N_STATE + k * N_STATE // 2, p * N_STATE + (k + 1) * N_STATE // 2) for p in range(2)])
              for k in range(2)]
    for ch, parts in halves:
        for st in parts:
            bu_ref[:, st] = _dot(ub[:, ch], bbd_ref[ch, st])
    for blk in range(N_STATE // S5_LANE_BLOCK):
        re = slice(blk * S5_LANE_BLOCK, (blk + 1) * S5_LANE_BLOCK)
        im = slice(N_STATE + blk * S5_LANE_BLOCK, N_STATE + (blk + 1) * S5_LANE_BLOCK)
        lr = lam_ref[:, re]
        li = lam_ref[:, im]

        def step(t, carry, re=re, im=im, lr=lr, li=li):
            hr, hi = carry
            row = pl.ds(t, 1)
            nr = lr * hr - li * hi + bu_ref[row, re]
            ni = lr * hi + li * hr + bu_ref[row, im]
            hs_ref[row, re] = nr
            hs_ref[row, im] = ni
            return nr, ni

        hr, hi = lax.fori_loop(0, steps, step, (h_ref[:, re], h_ref[:, im]),
                               unroll=min(8, steps))
        h_ref[:, re] = hr
        h_ref[:, im] = hi
    for ch, parts in halves:
        y = sum(_dot(hs_ref[:, st].astype(BF16), cbd_ref[st, ch]) for st in parts)
        y = y + d_ref[:, ch] * u[:, ch]
        y_ref[0, :, ch] = (jax.nn.gelu(y) * jax.nn.sigmoid(z_ref[0, :, ch])).astype(y_ref.dtype)

    @pl.when(c == pl.num_programs(1) - 1)
    def _():
        hl_ref[0] = h_ref[...]


def _s5_weights(lam_re, lam_im, log_dt, b_re, b_im, c_re, c_im, d_skip):
    lam = lax.complex(lam_re.astype(F32), lam_im.astype(F32))
    dt = jnp.exp(log_dt.astype(F32))[:, None]
    lam_bar = jnp.exp(lam * dt)
    b_bar = ((lam_bar - 1.0) / lam)[..., None] * lax.complex(b_re.astype(F32), b_im.astype(F32))
    eye = jnp.eye(N_SSM_GROUPS, dtype=F32)
    def in_bd(b):
        return jnp.einsum("gph,gk->ghkp", b, eye).reshape(SSM_WIDTH, N_STATE)
    bbd = jnp.concatenate([in_bd(b_bar.real), in_bd(b_bar.imag)], axis=1)
    def out_bd(cm):
        return jnp.einsum("ghp,gk->gpkh", cm, eye).reshape(N_STATE, SSM_WIDTH)
    cbd = jnp.concatenate([out_bd(c_re.astype(F32)), -out_bd(c_im.astype(F32))], axis=0)
    lam_row = jnp.concatenate([lam_bar.real.reshape(1, N_STATE), lam_bar.imag.reshape(1, N_STATE)], axis=1)
    return bbd.astype(BF16), lam_row, cbd.astype(BF16), d_skip.astype(F32).reshape(1, SSM_WIDTH)


def _s5(u, z, h0, s5w, chunk, out_dtype):
    bbd, lam_row, cbd, d_row = s5w
    s, t, _ = u.shape
    const = lambda shape: pl.BlockSpec(shape, lambda b, c: (0,) * len(shape))
    return pl.pallas_call(
        _s5_kernel,
        grid=(s, t // chunk),
        in_specs=[
            pl.BlockSpec((1, chunk, SSM_WIDTH), lambda b, c: (b, c, 0)),
            pl.BlockSpec((1, chunk, SSM_WIDTH), lambda b, c: (b, c, 0)),
            pl.BlockSpec((1, 1, 2 * N_STATE), lambda b, c: (b, 0, 0)),
            const((SSM_WIDTH, 2 * N_STATE)), const((1, 2 * N_STATE)),
            const((2 * N_STATE, SSM_WIDTH)), const((1, SSM_WIDTH)),
        ],
        out_specs=[
            pl.BlockSpec((1, chunk, SSM_WIDTH), lambda b, c: (b, c, 0)),
            pl.BlockSpec((1, 1, 2 * N_STATE), lambda b, c: (b, 0, 0)),
        ],
        out_shape=[jax.ShapeDtypeStruct((s, t, SSM_WIDTH), out_dtype),
                   jax.ShapeDtypeStruct((s, 1, 2 * N_STATE), F32)],
        scratch_shapes=[pltpu.VMEM((chunk, 2 * N_STATE), F32),
                        pltpu.VMEM((chunk, 2 * N_STATE), F32),
                        pltpu.VMEM((1, 2 * N_STATE), F32)],
        compiler_params=_cparams("parallel", "arbitrary"),
        name="s5",
    )(u, z, h0, bbd, lam_row, cbd, d_row)


def _outproj_kernel(x_ref, a_ref, s_ref, wa_ref, ws_ref, y_ref):
    y = x_ref[...]
    y = y + _dot(a_ref[...].astype(BF16), wa_ref[...])
    y = y + _dot(s_ref[...].astype(BF16), ws_ref[...])
    y_ref[...] = y


def _outproj_weights(w_out):
    wa = w_out[:NSA_WIDTH].reshape(N_KV, HPG, 1, HEAD_DIM, D_MODEL)
    slot = jnp.eye(N_KV, dtype=F32).reshape(N_KV, 1, N_KV, 1, 1)
    return (wa * slot).reshape(QPAD, D_MODEL).astype(BF16), w_out[NSA_WIDTH:].astype(BF16)


def _outproj(x2d, a_out, s_out, wa, ws, tm):
    n = x2d.shape[0]
    row = lambda w: pl.BlockSpec((tm, w), lambda i: (i, 0))
    return pl.pallas_call(
        _outproj_kernel,
        grid=(n // tm,),
        in_specs=[row(D_MODEL), row(QPAD), row(SSM_WIDTH),
                  pl.BlockSpec((QPAD, D_MODEL), lambda i: (0, 0)),
                  pl.BlockSpec((SSM_WIDTH, D_MODEL), lambda i: (0, 0))],
        out_specs=row(D_MODEL),
        out_shape=jax.ShapeDtypeStruct((n, D_MODEL), F32),
        compiler_params=_cparams("parallel"),
        name="outproj",
    )(x2d, a_out, s_out, wa, ws)


PEER_HALF = PEER_HEADS * PEER_DK // 2
_CAND_COUNTS = tuple(PEER_TOPK // (a + 1) for a in range(PEER_TOPK))
N_CAND = sum(_CAND_COUNTS)
N_CAND_PAD = -(-N_CAND // 8) * 8


def _peer_route_kernel(y_ref, g_ref, wq_ref, k1_ref, k2_ref, xn_ref, c1_ref, e1_ref, r2_ref, e2_ref,
                       s1_ref, s2_ref, v1_ref, v2_ref, cand_ref, rank1_ref, rank2_ref, rankc_ref):
    xn = _rmsnorm(y_ref[...], g_ref[...]).astype(BF16)
    xn_ref[...] = xn
    q = _dot(xn, wq_ref[...]).astype(BF16)
    s1_ref[...] = _nt_dot(k1_ref[...], q[:, :PEER_HALF])
    s2_ref[...] = _nt_dot(k2_ref[...], q[:, PEER_HALF:])
    t = y_ref.shape[0]
    krows = lambda h: slice(h * N_KEYS, (h + 1) * N_KEYS)
    crows = lambda h: slice(h * N_CAND_PAD, (h + 1) * N_CAND_PAD)

    def key_ranks(exact):
        worst = jnp.zeros((1, t), F32)
        for h in range(PEER_HEADS):
            for s_ref, rank_ref, v_ref in ((s1_ref, rank1_ref, v1_ref), (s2_ref, rank2_ref, v2_ref)):
                rank = _take_top(s_ref[krows(h), :], PEER_TOPK, v_ref, h * PEER_TOPK, exact)
                rank_ref[krows(h), :] = rank
                worst = jnp.maximum(worst, _n_taken(rank, PEER_TOPK))
        return worst

    worst = key_ranks(False)

    @pl.when(jnp.max(worst) > float(PEER_TOPK))
    def _():
        key_ranks(True)

    for h in range(PEER_HEADS):
        v1 = v1_ref[h * PEER_TOPK:(h + 1) * PEER_TOPK, :]
        v2 = v2_ref[h * PEER_TOPK:(h + 1) * PEER_TOPK, :]
        off = h * N_CAND_PAD
        for a, nb in enumerate(_CAND_COUNTS):
            cand_ref[off:off + nb, :] = v1[a:a + 1, :] + v2[0:nb, :]
            off += nb
        cand_ref[off:(h + 1) * N_CAND_PAD, :] = jnp.full(((h + 1) * N_CAND_PAD - off, t), LOWEST, F32)

    def cand_ranks(exact):
        worst = jnp.zeros((1, t), F32)
        for h in range(PEER_HEADS):
            rank = _take_top(cand_ref[crows(h), :], PEER_TOPK, exact=exact)
            rankc_ref[crows(h), :] = rank
            worst = jnp.maximum(worst, _n_taken(rank, PEER_TOPK))
        return worst

    worst = cand_ranks(False)

    @pl.when(jnp.max(worst) > float(PEER_TOPK))
    def _():
        cand_ranks(True)

    for h in range(PEER_HEADS):
        rows = krows(h)
        top = slice(h * PEER_TOPK, h * PEER_TOPK + 1)
        cand = cand_ref[crows(h), :]
        taken = jnp.where(rankc_ref[crows(h), :] < float(PEER_TOPK), 1.0, 0.0)
        z = jnp.sum(taken * jnp.exp(cand - cand[0:1, :]), axis=0, keepdims=True)
        rank1 = rank1_ref[rows, :]
        count = jnp.zeros((N_KEYS, t), F32)
        off = 0
        for a, nb in enumerate(_CAND_COUNTS):
            n_a = jnp.sum(taken[off:off + nb, :], axis=0, keepdims=True)
            count = jnp.where(rank1 == float(a), n_a, count)
            off += nb
        c1_ref[rows, :] = count
        e1_ref[rows, :] = jnp.exp(s1_ref[rows, :] - v1_ref[top, :]) / z
        r2_ref[rows, :] = rank2_ref[rows, :].astype(BF16)
        e2_ref[rows, :] = jnp.exp(s2_ref[rows, :] - v2_ref[top, :]).astype(BF16)


def _peer_weights(w_q, sub_k1, sub_k2):
    wq = w_q.reshape(D_MODEL, PEER_HEADS, 2, PEER_DK // 2).transpose(0, 2, 1, 3).reshape(D_MODEL, 2 * PEER_HALF)
    eye = jnp.eye(PEER_HEADS, dtype=F32)
    bd = lambda k: jnp.einsum("hkd,hj->hkjd", k, eye).reshape(PEER_HEADS * N_KEYS, PEER_HALF)
    return wq.astype(BF16), bd(sub_k1).astype(BF16), bd(sub_k2).astype(BF16)


def _peer_route(y2d, norm_g, wq, k1bd, k2bd, tm):
    n = y2d.shape[0]
    hk = PEER_HEADS * N_KEYS
    const = lambda a: pl.BlockSpec(a.shape, lambda i: (0, 0))
    col = lambda r: pl.BlockSpec((r, tm), lambda i: (0, i))
    tshape = lambda dt: jax.ShapeDtypeStruct((hk, n), dt)
    return pl.pallas_call(
        _peer_route_kernel,
        grid=(n // tm,),
        in_specs=[pl.BlockSpec((tm, D_MODEL), lambda i: (i, 0)),
                  pl.BlockSpec((1, D_MODEL), lambda i: (0, 0)), const(wq), const(k1bd), const(k2bd)],
        out_specs=[pl.BlockSpec((tm, D_MODEL), lambda i: (i, 0)), col(hk), col(hk), col(hk), col(hk)],
        out_shape=[jax.ShapeDtypeStruct((n, D_MODEL), BF16),
                   tshape(F32),
                   tshape(F32),
                   tshape(BF16),
                   tshape(BF16)],
        scratch_shapes=[pltpu.VMEM((hk, tm), F32), pltpu.VMEM((hk, tm), F32),
                        pltpu.VMEM((PEER_HEADS * PEER_TOPK, tm), F32),
                        pltpu.VMEM((PEER_HEADS * PEER_TOPK, tm), F32),
                        pltpu.VMEM((PEER_HEADS * N_CAND_PAD, tm), F32),
                        pltpu.VMEM((hk, tm), F32), pltpu.VMEM((hk, tm), F32),
                        pltpu.VMEM((PEER_HEADS * N_CAND_PAD, tm), F32)],
        compiler_params=_cparams("parallel"),
        name="peer_route",
    )(y2d, norm_g.reshape(1, D_MODEL), wq, k1bd, k2bd)


EXPERT_BLOCK = 1024
I1_PER_BLOCK = EXPERT_BLOCK // N_KEYS
N_EXPERT_BLOCKS = N_EXPERTS // EXPERT_BLOCK


def _peer_dense_kernel(y_ref, xn_ref, u_ref, vt_ref, c1_ref, e1_ref, r2_ref, e2_ref,
                       gf_ref, out_ref, acc_ref, act_ref, ga_ref):
    j = pl.program_id(1)
    t = xn_ref.shape[0]

    @pl.when(j == 0)
    def _():
        acc_ref[...] = jnp.zeros(acc_ref.shape, F32)
        act_ref[...] = jnp.zeros(act_ref.shape, BF16)
        ga_ref[...] = jnp.zeros(ga_ref.shape, BF16)

    def gate_stage(prev):
        jb = j - 1
        for ii in range(I1_PER_BLOCK):
            erows = slice(ii * N_KEYS, (ii + 1) * N_KEYS)
            for tc in range(t // LANES):
                cols = slice(tc * LANES, (tc + 1) * LANES)
                gate = jnp.zeros((N_KEYS, LANES), BF16)
                for h in range(PEER_HEADS):
                    grp = pl.ds(pl.multiple_of(h * N_KEYS + jb * I1_PER_BLOCK, I1_PER_BLOCK), I1_PER_BLOCK)
                    krows = slice(h * N_KEYS, (h + 1) * N_KEYS)
                    count = c1_ref[grp, cols][ii:ii + 1, :].astype(BF16)
                    e1 = e1_ref[grp, cols][ii:ii + 1, :].astype(BF16)
                    gate = gate + jnp.where(r2_ref[krows, cols] < count, e2_ref[krows, cols] * e1,
                                            jnp.zeros((), BF16))
                ga_ref[prev, erows, cols] = gate * act_ref[prev, erows, cols]

    def stages(cur, prev):
        pl.when((j >= 1) & (j <= N_EXPERT_BLOCKS))(functools.partial(gate_stage, prev))

        acc_ref[...] += _dot(vt_ref[...], ga_ref[cur])

        act_ref[cur] = _gelu_tanh(_nt_dot(u_ref[...], xn_ref[...])).astype(BF16)

    for parity in range(2):
        pl.when(j % 2 == parity)(functools.partial(stages, parity, 1 - parity))

    @pl.when(j == pl.num_programs(1) - 1)
    def _():
        y = y_ref[...] + acc_ref[...].T
        out_ref[...] = _rmsnorm(y, gf_ref[...])


def _peer_dense(y2d, xn, u_bf, vt_bf, c1, e1, r2, e2, norm_f, tm):
    n = y2d.shape[0]
    hk = PEER_HEADS * N_KEYS
    tok = lambda w: pl.BlockSpec((tm, w), lambda i, j: (i, 0))
    col = lambda r: pl.BlockSpec((r, tm), lambda i, j: (0, i))
    last = N_EXPERT_BLOCKS - 1
    return pl.pallas_call(
        _peer_dense_kernel,
        grid=(n // tm, N_EXPERT_BLOCKS + 2),
        in_specs=[tok(D_MODEL), tok(D_MODEL),
                  pl.BlockSpec((EXPERT_BLOCK, D_MODEL), lambda i, j: (jnp.minimum(j, last), 0)),
                  pl.BlockSpec((D_MODEL, EXPERT_BLOCK), lambda i, j: (0, jnp.maximum(j - 2, 0))),
                  col(hk), col(hk), col(hk), col(hk),
                  pl.BlockSpec((1, D_MODEL), lambda i, j: (0, 0))],
        out_specs=tok(D_MODEL),
        out_shape=jax.ShapeDtypeStruct((n, D_MODEL), F32),
        scratch_shapes=[pltpu.VMEM((D_MODEL, tm), F32),
                        pltpu.VMEM((2, EXPERT_BLOCK, tm), BF16),
                        pltpu.VMEM((2, EXPERT_BLOCK, tm), BF16)],
        compiler_params=_cparams("parallel", "arbitrary"),
        name="peer_dense",
    )(y2d, xn, u_bf, vt_bf, c1, e1, r2, e2, norm_f.reshape(1, D_MODEL))


PROJ_TILE = 512
PEER_TILE = 256
PEER_DENSE_TILE = 512
S5_CHUNK = 256


def _state_to_rows(st):
    s = st.shape[0]
    return jnp.concatenate([st[..., 0].reshape(s, 1, N_STATE), st[..., 1].reshape(s, 1, N_STATE)], axis=-1)


def _rows_to_state(h):
    s = h.shape[0]
    shape = (s, N_SSM_GROUPS, SSM_STATE)
    return jnp.stack([h[:, 0, :N_STATE].reshape(shape), h[:, 0, N_STATE:].reshape(shape)], axis=-1)


def _feat_to_tokens(a, n_types):
    s, _, t = a.shape
    return a.reshape(s, n_types, N_KV, HEAD_DIM, t).transpose(0, 4, 1, 2, 3)


def _peer_block(y2d, norm_g, peer_w, norm_f, tm):
    wq, k1bd, k2bd, u_bf, vt_bf = peer_w
    xn, c1, e1, r2, e2 = _peer_route(y2d, norm_g, wq, k1bd, k2bd, tm)
    return _peer_dense(y2d, xn, u_bf, vt_bf, c1, e1, r2, e2, norm_f, min(PEER_DENSE_TILE, y2d.shape[0]))


def kernel(x_prompt, x_sample, cache_kv, cache_win, state_ssm, page_table, norm_mix, w_in, w_cmp1, w_cmp2, pe_cmp, lam_re, lam_im, log_dt, b_re, b_im, c_re, c_im, d_skip, w_out, norm_ffn, w_q_peer, sub_k1, sub_k2, u_tab, v_tab, norm_final):
    b, t, d = x_prompt.shape
    db, ts, _ = x_sample.shape
    assert w_in.shape[0] == DEPTH == 1 and d == D_MODEL
    l = 0
    n_pool = cache_kv.shape[1]
    wb = cache_win.shape[2]

    w_tok, w_feat = _proj_weights(w_in[l])
    cmp_w = _compress_weights(w_cmp1[l], w_cmp2[l], pe_cmp[l])
    s5_w = _s5_weights(lam_re[l], lam_im[l], log_dt[l], b_re[l], b_im[l], c_re[l], c_im[l], d_skip[l])
    wa, ws = _outproj_weights(w_out[l])
    peer_w = _peer_weights(w_q_peer[l], sub_k1[l], sub_k2[l]) + (
        u_tab[l].astype(BF16), v_tab[l].T.astype(BF16))

    q, gate, u, z, kswk, q_t, kvsel_t, ksvs_t, kvwin_t, kwvw_t = _project(
        x_prompt, norm_mix[l], w_tok, w_feat, PROJ_TILE, BF16)
    n_pg = t // PAGE_SIZE
    table_p = jnp.broadcast_to(jnp.arange(n_pg, dtype=jnp.int32), (b, n_pg))
    prompt_page = lambda k: pl.BlockSpec((None, 2, LANES, PAGE_SIZE),
                                         lambda bb, j, tb: (bb, 0, 0, tb[bb, j * CMP_PAGES + k]))
    cmp_p = _compress(kvsel_t.reshape(b, 4, LANES, t), prompt_page, table_p, *cmp_w)
    a_p = _nsa_prompt_t(q.reshape(b, t, QPAD), q_t, gate.reshape(b, t, LANES),
                        kswk.reshape(b, t, 2 * KV_WIDTH), ksvs_t, kwvw_t, cmp_p)
    s_p, h_p = _s5(u.reshape(b, t, SSM_WIDTH), z.reshape(b, t, SSM_WIDTH),
                   jnp.zeros((b, 1, 2 * N_STATE), F32), s5_w, S5_CHUNK, BF16)
    y1p = _outproj(x_prompt.reshape(b * t, d), a_p.reshape(b * t, QPAD), s_p.reshape(b * t, SSM_WIDTH),
                   wa, ws, PROJ_TILE)
    y_prompt = _peer_block(y1p, norm_ffn[l], peer_w, norm_final, PEER_TILE).reshape(b, t, d)
    kv_prompt = _feat_to_tokens(kvsel_t, 4)
    win_prompt = _feat_to_tokens(kvwin_t[:, :, t - min(WINDOW, t):], 2)
    ssm_prompt = _rows_to_state(h_p)

    qs, gate_s, u_s, z_s, _, _, kvsel_st, ksvs_st, kvwin_st, kwvw_st = _project(
        x_sample.reshape(1, db * ts, d), norm_mix[l], w_tok, w_feat, db * ts, F32)
    pages_s = cache_kv[l].transpose(0, 2, 3, 4, 1).reshape(n_pool, 4, LANES, PAGE_SIZE)
    sample_page = lambda k: pl.BlockSpec((None, 2, LANES, PAGE_SIZE),
                                         lambda bb, j, tb: (tb[bb, j * CMP_PAGES + k], 0, 0, 0))
    cmp_s = _compress(pages_s, sample_page, page_table, *cmp_w)
    new_page = lambda a: jnp.pad(a[0].reshape(2 * LANES, db, ts).transpose(1, 0, 2),
                                 ((0, 0), (0, 0), (0, PAGE_SIZE - ts)))
    win_t = cache_win[l].transpose(0, 2, 3, 4, 1).reshape(db, 2 * LANES, wb)
    a_s = _nsa_sample(qs.reshape(db, ts, QPAD), gate_s.reshape(db, ts, LANES), pages_s, page_table, cmp_s,
                      win_t, new_page(ksvs_st), new_page(kwvw_st))
    s_s, h_s = _s5(u_s.reshape(db, ts, SSM_WIDTH), z_s.reshape(db, ts, SSM_WIDTH),
                   _state_to_rows(state_ssm[l].astype(F32)), s5_w, ts, F32)
    y1s = _outproj(x_sample.reshape(db * ts, d), a_s.reshape(db * ts, QPAD), s_s.reshape(db * ts, SSM_WIDTH),
                   wa, ws, db * ts)
    y_sample = _peer_block(y1s, norm_ffn[l], peer_w, norm_final, PEER_TILE).reshape(db, ts, d)
    per_tok = lambda a, n_types: a[0].reshape(n_types, N_KV, HEAD_DIM, db, ts).transpose(3, 4, 0, 1, 2)
    kv_sample = per_tok(kvsel_st, 4)
    win_new = per_tok(kvwin_st, 2).astype(cache_win.dtype)
    win_sample = jnp.concatenate([cache_win[l], win_new], axis=1)[:, ts:]
    ssm_sample = _rows_to_state(h_s)

    return (y_prompt, y_sample, kv_prompt[None], kv_sample[None], win_prompt[None], win_sample[None],
            ssm_prompt[None], ssm_sample[None])
```

```python
import functools
import math

import jax
import jax.numpy as jnp
from jax import lax
from jax.experimental import pallas as pl
from jax.experimental.pallas import tpu as pltpu

D_MODEL = 1024
DEPTH = 1
PAGE_SIZE = 128
N_HEADS = 8
N_KV = 2
HEAD_DIM = 64
HPG = N_HEADS // N_KV
CMP_LEN = 32
CMP_STRIDE = 16
SEL_BLOCK = 64
N_SEL = 16
WINDOW = 512
SSM_WIDTH = 512
SSM_GROUP = 16
N_SSM_GROUPS = SSM_WIDTH // SSM_GROUP
SSM_STATE = 64
N_KEYS = 128
N_EXPERTS = N_KEYS * N_KEYS
PEER_HEADS = 8
PEER_DK = 128
PEER_TOPK = 16
NSA_WIDTH = N_HEADS * HEAD_DIM
KV_WIDTH = N_KV * HEAD_DIM
GATE_WIDTH = 3 * N_HEADS
EPS = 1e-6
NEG = -1e30
FORCE = 1e4
LOG2_E = 1.0 / math.log(2.0)
LOWEST = -3.0e38

LANES = 128
BF16_ROWS = 16
VMEM_LIMIT = 56 * 1024 * 1024

N_STATE = N_SSM_GROUPS * SSM_STATE
QPAD = N_HEADS * LANES
F32 = jnp.float32
BF16 = jnp.bfloat16

assert KV_WIDTH == LANES and PAGE_SIZE == LANES


def _cparams(*sem):
    return pltpu.CompilerParams(dimension_semantics=sem, vmem_limit_bytes=VMEM_LIMIT)


def _nt_dot(a, b):
    return lax.dot_general(a, b, (((1,), (1,)), ((), ())), preferred_element_type=F32)


def _dot(a, b):
    return jnp.dot(a, b, preferred_element_type=F32)


def _rmsnorm(x, g):
    return x * lax.rsqrt(jnp.mean(x * x, axis=-1, keepdims=True) + EPS) * g


_GELU_C1 = 2.0 * math.sqrt(2.0 / math.pi) * LOG2_E
_GELU_C3 = _GELU_C1 * 0.044715


def _gelu_tanh(x):
    return x / (1.0 + jnp.exp2(x * (-_GELU_C1 - _GELU_C3 * (x * x))))


_PROJ_COLS = (("q", QPAD), ("gate", LANES), ("u", SSM_WIDTH), ("z", SSM_WIDTH), ("kswk", 2 * KV_WIDTH))
KVT_ROWS = 6 * KV_WIDTH


def _proj_kernel(x_ref, g_ref, w_ref, wt_ref, q_ref, gate_ref, u_ref, z_ref, kswk_ref,
                 qt_ref, kvsel_ref, ksvs_ref, kvwin_ref, kwvw_ref):
    xn = _rmsnorm(x_ref[...], g_ref[...]).astype(BF16)
    off = 0
    for (name, width), ref in zip(_PROJ_COLS, (q_ref, gate_ref, u_ref, z_ref, kswk_ref)):
        ref[...] = _dot(xn, w_ref[:, off:off + width]).astype(ref.dtype)
        off += width
    qt_ref[0] = _nt_dot(wt_ref[0:QPAD, :], xn).astype(BF16)
    kvt = _nt_dot(wt_ref[QPAD:, :], xn)
    kvsel_ref[0] = kvt[0:4 * KV_WIDTH]
    ksvs_ref[0] = kvt[2 * KV_WIDTH:4 * KV_WIDTH].astype(BF16)
    kvwin_ref[0] = kvt[4 * KV_WIDTH:]
    kwvw_ref[0] = kvt[4 * KV_WIDTH:].astype(BF16)


def _proj_weights(w_in):
    c0 = NSA_WIDTH
    c1 = c0 + 6 * KV_WIDTH
    c2 = c1 + GATE_WIDTH
    wq = w_in[:, :c0].reshape(D_MODEL, N_KV, HPG, 1, HEAD_DIM) * (HEAD_DIM ** -0.5 * LOG2_E)
    slot = jnp.eye(N_KV, dtype=F32).reshape(1, N_KV, 1, N_KV, 1)
    wq_pad = (wq * slot).reshape(D_MODEL, QPAD)
    wg = jnp.pad(w_in[:, c1:c2], ((0, 0), (0, LANES - GATE_WIDTH)))
    ks = w_in[:, c0 + 2 * KV_WIDTH:c0 + 3 * KV_WIDTH]
    kw = w_in[:, c0 + 4 * KV_WIDTH:c0 + 5 * KV_WIDTH]
    w = jnp.concatenate([wq_pad, wg, w_in[:, c2:], ks, kw], axis=1)
    wt = jnp.concatenate([wq_pad, w_in[:, c0:c1]], axis=1).T
    return w.astype(BF16), wt.astype(BF16)


def _project(x3d, norm_g, w_tok, w_feat, tm, q_dtype):
    s, t, _ = x3d.shape
    nt = t // tm
    n = s * t
    row = lambda w: pl.BlockSpec((tm, w), lambda b, i: (b * nt + i, 0))
    feat = lambda r: pl.BlockSpec((1, r, tm), lambda b, i: (b, 0, i))
    const = lambda a: pl.BlockSpec(a.shape, lambda b, i: (0, 0))
    shapes = [
        jax.ShapeDtypeStruct((n, QPAD), q_dtype),
        jax.ShapeDtypeStruct((n, LANES), F32),
        jax.ShapeDtypeStruct((n, SSM_WIDTH), F32),
        jax.ShapeDtypeStruct((n, SSM_WIDTH), F32),
        jax.ShapeDtypeStruct((n, 2 * KV_WIDTH), BF16),
        jax.ShapeDtypeStruct((s, QPAD, t), BF16),
        jax.ShapeDtypeStruct((s, 4 * KV_WIDTH, t), F32),
        jax.ShapeDtypeStruct((s, 2 * KV_WIDTH, t), BF16),
        jax.ShapeDtypeStruct((s, 2 * KV_WIDTH, t), F32),
        jax.ShapeDtypeStruct((s, 2 * KV_WIDTH, t), BF16),
    ]
    g2 = norm_g.reshape(1, D_MODEL)
    return pl.pallas_call(
        _proj_kernel,
        grid=(s, nt),
        in_specs=[row(D_MODEL), const(g2), const(w_tok), const(w_feat)],
        out_specs=[row(sh.shape[1]) for sh in shapes[:5]] + [feat(sh.shape[1]) for sh in shapes[5:]],
        out_shape=shapes,
        compiler_params=_cparams("parallel", "parallel"),
        name="proj",
    )(x3d.reshape(n, D_MODEL), g2, w_tok, w_feat)


CMP_PAGES = 16
CMP_PAIRS = CMP_STRIDE // 2


def _compress_kernel(tbl_ref, *refs, n_pages):
    del tbl_ref
    page_refs = refs[:CMP_PAGES]
    w1_ref, pe_ref, w2_ref, out_ref, slabk_ref, slabv_ref = refs[CMP_PAGES:]
    j = pl.program_id(1)
    seq = n_pages * PAGE_SIZE
    slabs = (slabk_ref, slabv_ref)

    for k, page in enumerate(page_refs):
        rows = pl.ds(pl.multiple_of((j * CMP_PAGES + k) * PAGE_SIZE, PAGE_SIZE), PAGE_SIZE)
        for c, slab in enumerate(slabs):
            slab[rows, :] = page[c].T

    @pl.when(j == n_pages // CMP_PAGES - 1)
    def _():
        nb = seq // CMP_STRIDE
        for c, slab in enumerate(slabs):
            head = jnp.zeros((nb, LANES), F32)
            tail = jnp.zeros((nb, LANES), F32)
            for m in range(CMP_PAIRS):
                x = jnp.concatenate([slab[pl.ds(m, nb, stride=CMP_STRIDE), :],
                                     slab[pl.ds(m + CMP_PAIRS, nb, stride=CMP_STRIDE), :]], axis=1)
                head = head + _dot((x + pe_ref[c, m:m + 1, :]).astype(BF16), w1_ref[c, m])
                m2 = CMP_PAIRS + m
                tail = tail + _dot((x + pe_ref[c, m2:m2 + 1, :]).astype(BF16), w1_ref[c, m2])
            hid = head + pltpu.roll(tail, nb - 1, 0)
            out = _dot(jax.nn.gelu(hid).astype(BF16), w2_ref[c])
            out_ref[0, :, c * LANES:(c + 1) * LANES] = out.astype(BF16)


def _compress_weights(w1, w2, pe):
    eye = jnp.eye(N_KV, dtype=F32)
    bd = lambda m: jnp.einsum("...de,gk->...gdke", m, eye).reshape(m.shape[:-2] + (LANES, LANES))
    pair = lambda a: a.reshape(2, 2, 2, CMP_PAIRS, *a.shape[2:]).swapaxes(2, 3)
    w1p = pair(bd(w1)).reshape(2, 2 * CMP_PAIRS, 2 * LANES, LANES)
    pe2 = jnp.concatenate([pe, pe], axis=-1)
    pep = pair(pe2).reshape(2, 2 * CMP_PAIRS, 2 * LANES)
    return w1p.astype(BF16), pep, bd(w2).astype(BF16)


def _compress(pages, page_spec, table, w1bd, pe2, w2bd):
    s, p = table.shape
    assert CMP_LEN == 2 * CMP_STRIDE and p % CMP_PAGES == 0
    seq = p * PAGE_SIZE
    const = lambda a: pl.BlockSpec(a.shape, lambda b, j, t: (0,) * a.ndim)
    grid_spec = pltpu.PrefetchScalarGridSpec(
        num_scalar_prefetch=1,
        grid=(s, p // CMP_PAGES),
        in_specs=[page_spec(k) for k in range(CMP_PAGES)] + [const(w1bd), const(pe2), const(w2bd)],
        out_specs=pl.BlockSpec((1, seq // CMP_STRIDE, 2 * LANES), lambda b, j, t: (b, 0, 0)),
        scratch_shapes=[pltpu.VMEM((seq, LANES), F32), pltpu.VMEM((seq, LANES), F32)],
    )
    return pl.pallas_call(
        functools.partial(_compress_kernel, n_pages=p),
        grid_spec=grid_spec,
        out_shape=jax.ShapeDtypeStruct((s, seq // CMP_STRIDE, 2 * LANES), BF16),
        compiler_params=_cparams("parallel", "arbitrary"),
        name="compress",
    )(table, *([pages] * CMP_PAGES), w1bd, pe2, w2bd)


Q_TILE = 128
KEY_CHUNK = 1024
N_BLK_PAD = 128
LOG2_SEL_BLOCK = int(math.log2(SEL_BLOCK))
WIN_KEYS = WINDOW + Q_TILE


def _cover_t(n_cmp_pad):
    n = jnp.arange(n_cmp_pad, dtype=jnp.int32)[None, :] * CMP_STRIDE
    s = jnp.arange(N_BLK_PAD, dtype=jnp.int32)[:, None] * SEL_BLOCK
    return ((n < s + SEL_BLOCK) & (n + CMP_LEN > s)).astype(BF16)


def _split3_nt(w, x):
    hi = x.astype(BF16)
    r1 = x - hi.astype(F32)
    mid = r1.astype(BF16)
    lo = (r1 - mid.astype(F32)).astype(BF16)
    return _nt_dot(w, hi) + _nt_dot(w, mid) + _nt_dot(w, lo)


def _take_top(s, n, out_ref=None, out_row=0, exact=True):
    kidx = lax.broadcasted_iota(jnp.int32, s.shape, 0).astype(F32)
    rem = s
    rank = jnp.full(s.shape, float(n), F32)
    for a in range(n):
        mx = jnp.max(rem, axis=0, keepdims=True)
        if exact:
            first = jnp.min(jnp.where(rem == mx, kidx, float(s.shape[0])), axis=0, keepdims=True)
            taken = kidx == first
        else:
            taken = rem == mx
        if out_ref is not None:
            out_ref[out_row + a:out_row + a + 1, :] = mx
        rank = jnp.where(taken, float(a), rank)
        rem = jnp.where(taken, LOWEST, rem)
    return rank


def _n_taken(rank, n):
    return jnp.sum(jnp.where(rank < float(n), 1.0, 0.0), axis=0, keepdims=True)


def _topk_block_mask(imp_t, n_keep):
    return jnp.where(_take_top(imp_t, n_keep) < float(n_keep), 0.0, NEG)


def _cmp_branch(q, s_ref, cmp_ref, ocmp_ref, t_pos, rows):
    ncp = cmp_ref.shape[1]
    kc = cmp_ref[0, :, 0:LANES]
    vc = cmp_ref[0, :, LANES:2 * LANES]
    s_ref[:, 0:ncp] = _nt_dot(q, kc)
    n_idx = lax.broadcasted_iota(jnp.int32, (rows, ncp), 1)
    ok = (n_idx * CMP_STRIDE + (CMP_LEN - 1) <= t_pos) & (n_idx < ncp - 1)
    imps = []
    for g in range(N_KV):
        psum = jnp.zeros((rows, ncp), F32)
        for h in range(HPG):
            hh = g * HPG + h
            s = jnp.where(ok, s_ref[hh * rows:(hh + 1) * rows, 0:ncp], NEG)
            m = jnp.max(s, axis=-1, keepdims=True)
            e = jnp.where(ok, jnp.exp2(s - m), 0.0)
            l = jnp.sum(e, axis=-1, keepdims=True)
            p = e * jnp.where(l > 0.0, 1.0 / l, 0.0)
            ocmp_ref[hh] = _dot(p.astype(BF16), vc)
            psum = psum + p
        imps.append(psum)
    return imps


def _with_ones_rows(vt, g):
    row = lax.broadcasted_iota(jnp.int32, vt.shape, 0)
    own = (row >= g * HEAD_DIM) & (row < (g + 1) * HEAD_DIM)
    return jnp.where(own, vt, jnp.ones((), vt.dtype))


def _block_expand(t):
    key_blk = jnp.arange(t, dtype=jnp.int32)[None, :] // SEL_BLOCK
    return (jnp.arange(N_BLK_PAD, dtype=jnp.int32)[:, None] == key_blk).astype(BF16)


def _nsa_prompt_t_kernel(q_ref, qt_ref, gate_ref, ktok_ref, vs_ref, vw_ref, cmp_ref, covt_ref, expt_ref,
                         out_ref, qs_ref, qst_ref, sc_ref, s_ref, p_ref, m_ref, acc_ref, ocmp_ref):
    i = pl.program_id(1)
    t0 = i * Q_TILE
    group_cols = HPG * Q_TILE
    for h in range(N_HEADS):
        cols = slice(h * Q_TILE, (h + 1) * Q_TILE)
        qs_ref[cols, :] = q_ref[0, :, h * LANES:(h + 1) * LANES]
        qst_ref[0:LANES, cols] = qt_ref[0, h * LANES:(h + 1) * LANES, :]
    t_col = t0 + lax.broadcasted_iota(jnp.int32, (Q_TILE, 1), 0)

    psums = _cmp_branch(qs_ref[...], sc_ref, cmp_ref, ocmp_ref, t_col, Q_TILE)
    s_idx = lax.broadcasted_iota(jnp.int32, (N_BLK_PAD, Q_TILE), 0)
    t_row = t0 + lax.broadcasted_iota(jnp.int32, (N_BLK_PAD, Q_TILE), 1)
    cur = jnp.right_shift(t_row, LOG2_SEL_BLOCK)
    forced = (s_idx == 0) | (s_idx == cur) | (s_idx == cur - 1)
    for g in range(N_KV):
        imp_t = _split3_nt(covt_ref[...], psums[g])
        imp_t = jnp.where(forced, FORCE, imp_t)
        imp_t = jnp.where(s_idx * SEL_BLOCK <= t_row, imp_t, NEG)
        mask_t = _topk_block_mask(imp_t, N_SEL).astype(BF16)
        for h in range(HPG):
            hh = g * HPG + h
            qst_ref[LANES:2 * LANES, hh * Q_TILE:(hh + 1) * Q_TILE] = mask_t

    def attend(n_keys, v_aug, bias_t):
        for g in range(N_KV):
            alphas = []
            for h in range(HPG):
                hh = g * HPG + h
                s = s_ref[0:n_keys, hh * Q_TILE:(hh + 1) * Q_TILE]
                if bias_t is not None:
                    s = s + bias_t
                m_old = m_ref[hh:hh + 1, :]
                m_new = jnp.maximum(m_old, jnp.max(s, axis=0, keepdims=True))
                p_ref[0:n_keys, h * Q_TILE:(h + 1) * Q_TILE] = jnp.exp2(s - m_new).astype(BF16)
                alphas.append(jnp.exp2(m_old - m_new))
                m_ref[hh:hh + 1, :] = m_new
            alpha = jnp.concatenate(alphas, axis=1)
            acc_ref[g] = alpha * acc_ref[g] + _dot(v_aug[g], p_ref[0:n_keys, :])

    def reset():
        m_ref[...] = jnp.full(m_ref.shape, NEG, F32)
        acc_ref[...] = jnp.zeros(acc_ref.shape, F32)

    def outputs():
        outs = []
        for g in range(N_KV):
            lsum = (1 - g) * HEAD_DIM
            for h in range(HPG):
                a = acc_ref[g, :, h * Q_TILE:(h + 1) * Q_TILE]
                outs.append(a / a[lsum:lsum + 1, :])
        return outs

    reset()
    key_sub = lax.broadcasted_iota(jnp.int32, (KEY_CHUNK, Q_TILE), 0)
    t_lane = t0 + lax.broadcasted_iota(jnp.int32, (KEY_CHUNK, Q_TILE), 1)

    def chunk(c, diagonal):
        k0 = pl.multiple_of(c * KEY_CHUNK, KEY_CHUNK)
        keys = pl.ds(k0, KEY_CHUNK)
        lhs = jnp.concatenate([ktok_ref[0, keys, 0:LANES], expt_ref[keys, :]], axis=1)
        s_ref[0:KEY_CHUNK, :] = _dot(lhs, qst_ref[...])
        vt = vs_ref[0, LANES:2 * LANES, keys]
        causal = jnp.where(k0 + key_sub <= t_lane, 0.0, NEG) if diagonal else None
        attend(KEY_CHUNK, [_with_ones_rows(vt, g) for g in range(N_KV)], causal)

    n_full = t0 // KEY_CHUNK

    def full_chunk(c, carry):
        chunk(c, False)
        return carry

    lax.fori_loop(0, n_full, full_chunk, 0)
    chunk(n_full, True)
    o_sel = outputs()

    reset()
    start = pl.multiple_of(jnp.maximum(t0 - WINDOW, 0), Q_TILE)
    wkeys = pl.ds(start, WIN_KEYS)
    s_ref[0:WIN_KEYS, :] = _dot(ktok_ref[0, wkeys, LANES:2 * LANES], qst_ref[0:LANES, :])
    vwt = vw_ref[0, LANES:2 * LANES, wkeys]
    dist = (t0 + lax.broadcasted_iota(jnp.int32, (WIN_KEYS, Q_TILE), 1)
            - (start + lax.broadcasted_iota(jnp.int32, (WIN_KEYS, Q_TILE), 0)))
    bias_w = jnp.where((dist >= 0) & (dist <= WINDOW), 0.0, NEG)
    attend(WIN_KEYS, [_with_ones_rows(vwt, g) for g in range(N_KV)], bias_w)
    o_win = outputs()

    gates = jax.nn.sigmoid(gate_ref[0])
    gates_t = gates.T
    for hh in range(N_HEADS):
        o_t = gates_t[3 * hh + 1:3 * hh + 2, :] * o_sel[hh] + gates_t[3 * hh + 2:3 * hh + 3, :] * o_win[hh]
        o = gates[:, 3 * hh:3 * hh + 1] * ocmp_ref[hh] + o_t.T
        out_ref[0, :, hh * LANES:(hh + 1) * LANES] = o.astype(out_ref.dtype)


def _nsa_prompt_t(q, qt, gates, kswk, ksvs_t, kwvw_t, cmp):
    b, t, _ = q.shape
    ncp = cmp.shape[1]
    tile = lambda w: pl.BlockSpec((1, Q_TILE, w), lambda bb, i: (bb, i, 0))
    whole = lambda r, w: pl.BlockSpec((1, r, w), lambda bb, i: (bb, 0, 0))
    const = lambda r, w: pl.BlockSpec((r, w), lambda bb, i: (0, 0))
    rows = N_HEADS * Q_TILE
    expand_t = _block_expand(t).T
    return pl.pallas_call(
        _nsa_prompt_t_kernel,
        grid=(b, t // Q_TILE),
        in_specs=[tile(QPAD), pl.BlockSpec((1, QPAD, Q_TILE), lambda bb, i: (bb, 0, i)), tile(LANES),
                  whole(t, 2 * LANES), whole(2 * LANES, t), whole(2 * LANES, t),
                  whole(ncp, 2 * LANES), const(N_BLK_PAD, ncp), const(t, N_BLK_PAD)],
        out_specs=tile(QPAD),
        out_shape=jax.ShapeDtypeStruct((b, t, QPAD), BF16),
        scratch_shapes=[
            pltpu.VMEM((rows, LANES), BF16),
            pltpu.VMEM((2 * LANES, rows), BF16),
            pltpu.VMEM((rows, ncp), F32),
            pltpu.VMEM((max(WIN_KEYS, KEY_CHUNK), rows), F32),
            pltpu.VMEM((max(WIN_KEYS, KEY_CHUNK), HPG * Q_TILE), BF16),
            pltpu.VMEM((N_HEADS, Q_TILE), F32),
            pltpu.VMEM((N_KV, LANES, HPG * Q_TILE), F32),
            pltpu.VMEM((N_HEADS, Q_TILE, LANES), F32),
        ],
        compiler_params=_cparams("parallel", "arbitrary"),
        name="nsa_prompt",
    )(q, qt, gates, kswk, ksvs_t, kwvw_t, cmp, _cover_t(ncp), expand_t)


SEL_PAGES = 32


def _flash_update(s, vt, m_ref, l_ref, acc_ref):
    m_old = m_ref[...]
    m_new = jnp.maximum(m_old, jnp.max(s, axis=-1, keepdims=True))
    alpha = jnp.exp2(m_old - m_new)
    p = jnp.exp2(s - m_new[:, 0:1])
    l_ref[...] = alpha * l_ref[...] + jnp.sum(p, axis=-1, keepdims=True)
    acc_ref[...] = alpha * acc_ref[...] + _nt_dot(p.astype(BF16), vt)
    m_ref[...] = m_new


def _nsa_sample_kernel(tbl_ref, *refs, n_pages, tq):
    del tbl_ref
    q_ref, gate_ref = refs[:2]
    page_refs = refs[2:2 + SEL_PAGES]
    (cmp_ref, win_ref, newkv_ref, newwin_ref, covt_ref, expand_ref, out_ref, qs_ref, s_ref, msel_ref,
     m_ref, l_ref, acc_ref, ocmp_ref) = refs[2 + SEL_PAGES:]
    j = pl.program_id(1)
    past_len = n_pages * PAGE_SIZE
    rows = N_HEADS * tq
    step_keys = SEL_PAGES * PAGE_SIZE
    i_col = lax.broadcasted_iota(jnp.int32, (rows, 1), 0) & (tq - 1)

    @pl.when(j == 0)
    def _():
        for h in range(N_HEADS):
            qs_ref[h * tq:(h + 1) * tq, :] = q_ref[0, :, h * LANES:(h + 1) * LANES]
        t_col = past_len + lax.broadcasted_iota(jnp.int32, (tq, 1), 0)
        psums = _cmp_branch(qs_ref[...].astype(BF16), s_ref, cmp_ref, ocmp_ref, t_col, tq)
        s_idx = lax.broadcasted_iota(jnp.int32, (N_BLK_PAD, LANES), 0)
        t_row = past_len + lax.broadcasted_iota(jnp.int32, (N_BLK_PAD, LANES), 1)
        cur = jnp.right_shift(t_row, LOG2_SEL_BLOCK)
        forced = (s_idx == 0) | (s_idx == cur) | (s_idx == cur - 1)
        ncp = cmp_ref.shape[1]
        for g in range(N_KV):
            psum = jnp.concatenate([psums[g], jnp.zeros((LANES - tq, ncp), F32)], axis=0)
            imp_t = jnp.where(forced, FORCE, _split3_nt(covt_ref[...], psum))
            mask_t = _topk_block_mask(imp_t, N_SEL - 1)
            msel_ref[g] = mask_t.T.astype(BF16)
        m_ref[...] = jnp.full(m_ref.shape, NEG, F32)
        l_ref[...] = jnp.zeros(l_ref.shape, F32)
        acc_ref[...] = jnp.zeros(acc_ref.shape, F32)

    q = qs_ref[...].astype(BF16)
    kt = jnp.concatenate([page[0].astype(BF16) for page in page_refs], axis=1)
    vt = jnp.concatenate([page[1].astype(BF16) for page in page_refs], axis=1)
    bias_g = [_dot(msel_ref[g, 0:BF16_ROWS, :], expand_ref[...])[0:tq, :] for g in range(N_KV)]
    bias = jnp.concatenate([bias_g[hh // HPG] for hh in range(N_HEADS)], axis=0)
    _flash_update(_dot(q, kt) + bias, vt, m_ref, l_ref, acc_ref)

    @pl.when(j == n_pages // SEL_PAGES - 1)
    def _():
        j_lane = lax.broadcasted_iota(jnp.int32, (rows, PAGE_SIZE), 1)
        new_bias = jnp.where(j_lane <= i_col, 0.0, NEG)
        _flash_update(_dot(q, newkv_ref[0, 0:LANES, :]) + new_bias, newkv_ref[0, LANES:2 * LANES, :],
                      m_ref, l_ref, acc_ref)
        wb = win_ref.shape[2]
        dist = wb + i_col - lax.broadcasted_iota(jnp.int32, (rows, wb), 1)
        s1 = _dot(q, win_ref[0, 0:LANES, :].astype(BF16)) + jnp.where((dist >= 0) & (dist <= WINDOW), 0.0, NEG)
        s2 = _dot(q, newwin_ref[0, 0:LANES, :]) + new_bias
        m = jnp.maximum(jnp.max(s1, axis=-1, keepdims=True), jnp.max(s2, axis=-1, keepdims=True))
        p1 = jnp.exp2(s1 - m)
        p2 = jnp.exp2(s2 - m)
        l = jnp.sum(p1, axis=-1, keepdims=True) + jnp.sum(p2, axis=-1, keepdims=True)
        o_win = (_nt_dot(p1.astype(BF16), win_ref[0, LANES:2 * LANES, :].astype(BF16))
                 + _nt_dot(p2.astype(BF16), newwin_ref[0, LANES:2 * LANES, :])) / l
        o_sel = acc_ref[...] / l_ref[...]
        gates = jax.nn.sigmoid(gate_ref[0])
        for hh in range(N_HEADS):
            r = slice(hh * tq, (hh + 1) * tq)
            o = (gates[:, 3 * hh:3 * hh + 1] * ocmp_ref[hh] + gates[:, 3 * hh + 1:3 * hh + 2] * o_sel[r]
                 + gates[:, 3 * hh + 2:3 * hh + 3] * o_win[r])
            out_ref[0, :, hh * LANES:(hh + 1) * LANES] = o


def _nsa_sample(q, gates, pages, table, cmp, win, newkv, newwin):
    s, tq, _ = q.shape
    n_pages = table.shape[1]
    ncp = cmp.shape[1]
    wb = win.shape[2]
    assert (n_pages * PAGE_SIZE) // SEL_BLOCK == N_BLK_PAD and tq <= SEL_BLOCK and tq & (tq - 1) == 0
    assert n_pages % SEL_PAGES == 0
    rows = N_HEADS * tq
    per_seq = lambda r, w: pl.BlockSpec((1, r, w), lambda b, j, t: (b, 0, 0))
    page_spec = lambda k: pl.BlockSpec((None, 2, LANES, PAGE_SIZE),
                                       lambda b, j, t: (t[b, j * SEL_PAGES + k], 1, 0, 0))
    grid_spec = pltpu.PrefetchScalarGridSpec(
        num_scalar_prefetch=1,
        grid=(s, n_pages // SEL_PAGES),
        in_specs=[per_seq(tq, QPAD), per_seq(tq, LANES)] + [page_spec(k) for k in range(SEL_PAGES)] + [
            per_seq(ncp, 2 * LANES), per_seq(2 * LANES, wb),
            per_seq(2 * LANES, PAGE_SIZE), per_seq(2 * LANES, PAGE_SIZE),
            pl.BlockSpec((N_BLK_PAD, ncp), lambda b, j, t: (0, 0)),
            pl.BlockSpec((N_BLK_PAD, SEL_PAGES * PAGE_SIZE), lambda b, j, t: (0, j))],
        out_specs=per_seq(tq, QPAD),
        scratch_shapes=[
            pltpu.VMEM((rows, LANES), F32),
            pltpu.VMEM((rows, ncp), F32),
            pltpu.VMEM((N_KV, LANES, N_BLK_PAD), BF16),
            pltpu.VMEM((rows, LANES), F32),
            pltpu.VMEM((rows, LANES), F32),
            pltpu.VMEM((rows, LANES), F32),
            pltpu.VMEM((N_HEADS, tq, LANES), F32),
        ],
    )
    return pl.pallas_call(
        functools.partial(_nsa_sample_kernel, n_pages=n_pages, tq=tq),
        grid_spec=grid_spec,
        out_shape=jax.ShapeDtypeStruct((s, tq, QPAD), F32),
        compiler_params=_cparams("parallel", "arbitrary"),
        name="nsa_sample",
    )(table, q, gates, *([pages] * SEL_PAGES), cmp, win, newkv, newwin, _cover_t(ncp),
      _block_expand(n_pages * PAGE_SIZE))


S5_LANE_BLOCK = 2048


def _s5_kernel(u_ref, z_ref, h0_ref, bbd_ref, lam_ref, cbd_ref, d_ref, y_ref, hl_ref,
               bu_ref, hs_ref, h_ref):
    c = pl.program_id(1)
    steps = u_ref.shape[1]

    @pl.when(c == 0)
    def _():
        h_ref[...] = h0_ref[0]

    u = u_ref[0]
    ub = u.astype(BF16)
    halves = [(slice(k * SSM_WIDTH // 2, (k + 1) * SSM_WIDTH // 2),
               [slice(p * N_STATE + k * N_STATE // 2, p * N_STATE + (k + 1) * N_STATE // 2) for p in range(2)])
              for k in range(2)]
    for ch, parts in halves:
        for st in parts:
            bu_ref[:, st] = _dot(ub[:, ch], bbd_ref[ch, st])
    for blk in range(N_STATE // S5_LANE_BLOCK):
        re = slice(blk * S5_LANE_BLOCK, (blk + 1) * S5_LANE_BLOCK)
        im = slice(N_STATE + blk * S5_LANE_BLOCK, N_STATE + (blk + 1) * S5_LANE_BLOCK)
        lr = lam_ref[:, re]
        li = lam_ref[:, im]

        def step(t, carry, re=re, im=im, lr=lr, li=li):
            hr, hi = carry
            row = pl.ds(t, 1)
            nr = lr * hr - li * hi + bu_ref[row, re]
            ni = lr * hi + li * hr + bu_ref[row, im]
            hs_ref[row, re] = nr
            hs_ref[row, im] = ni
            return nr, ni

        hr, hi = lax.fori_loop(0, steps, step, (h_ref[:, re], h_ref[:, im]),
                               unroll=min(8, steps))
        h_ref[:, re] = hr
        h_ref[:, im] = hi
    for ch, parts in halves:
        y = sum(_dot(hs_ref[:, st].astype(BF16), cbd_ref[st, ch]) for st in parts)
        y = y + d_ref[:, ch] * u[:, ch]
        y_ref[0, :, ch] = (jax.nn.gelu(y) * jax.nn.sigmoid(z_ref[0, :, ch])).astype(y_ref.dtype)

    @pl.when(c == pl.num_programs(1) - 1)
    def _():
        hl_ref[0] = h_ref[...]


def _s5_weights(lam_re, lam_im, log_dt, b_re, b_im, c_re, c_im, d_skip):
    lam = lax.complex(lam_re.astype(F32), lam_im.astype(F32))
    dt = jnp.exp(log_dt.astype(F32))[:, None]
    lam_bar = jnp.exp(lam * dt)
    b_bar = ((lam_bar - 1.0) / lam)[..., None] * lax.complex(b_re.astype(F32), b_im.astype(F32))
    eye = jnp.eye(N_SSM_GROUPS, dtype=F32)
    def in_bd(b):
        return jnp.einsum("gph,gk->ghkp", b, eye).reshape(SSM_WIDTH, N_STATE)
    bbd = jnp.concatenate([in_bd(b_bar.real), in_bd(b_bar.imag)], axis=1)
    def out_bd(cm):
        return jnp.einsum("ghp,gk->gpkh", cm, eye).reshape(N_STATE, SSM_WIDTH)
    cbd = jnp.concatenate([out_bd(c_re.astype(F32)), -out_bd(c_im.astype(F32))], axis=0)
    lam_row = jnp.concatenate([lam_bar.real.reshape(1, N_STATE), lam_bar.imag.reshape(1, N_STATE)], axis=1)
    return bbd.astype(BF16), lam_row, cbd.astype(BF16), d_skip.astype(F32).reshape(1, SSM_WIDTH)


def _s5(u, z, h0, s5w, chunk, out_dtype):
    bbd, lam_row, cbd, d_row = s5w
    s, t, _ = u.shape
    const = lambda shape: pl.BlockSpec(shape, lambda b, c: (0,) * len(shape))
    return pl.pallas_call(
        _s5_kernel,
        grid=(s, t // chunk),
        in_specs=[
            pl.BlockSpec((1, chunk, SSM_WIDTH), lambda b, c: (b, c, 0)),
            pl.BlockSpec((1, chunk, SSM_WIDTH), lambda b, c: (b, c, 0)),
            pl.BlockSpec((1, 1, 2 * N_STATE), lambda b, c: (b, 0, 0)),
            const((SSM_WIDTH, 2 * N_STATE)), const((1, 2 * N_STATE)),
            const((2 * N_STATE, SSM_WIDTH)), const((1, SSM_WIDTH)),
        ],
        out_specs=[
            pl.BlockSpec((1, chunk, SSM_WIDTH), lambda b, c: (b, c, 0)),
            pl.BlockSpec((1, 1, 2 * N_STATE), lambda b, c: (b, 0, 0)),
        ],
        out_shape=[jax.ShapeDtypeStruct((s, t, SSM_WIDTH), out_dtype),
                   jax.ShapeDtypeStruct((s, 1, 2 * N_STATE), F32)],
        scratch_shapes=[pltpu.VMEM((chunk, 2 * N_STATE), F32),
                        pltpu.VMEM((chunk, 2 * N_STATE), F32),
                        pltpu.VMEM((1, 2 * N_STATE), F32)],
        compiler_params=_cparams("parallel", "arbitrary"),
        name="s5",
    )(u, z, h0, bbd, lam_row, cbd, d_row)


def _outproj_kernel(x_ref, a_ref, s_ref, wa_ref, ws_ref, y_ref):
    y = x_ref[...]
    y = y + _dot(a_ref[...].astype(BF16), wa_ref[...])
    y = y + _dot(s_ref[...].astype(BF16), ws_ref[...])
    y_ref[...] = y


def _outproj_weights(w_out):
    wa = w_out[:NSA_WIDTH].reshape(N_KV, HPG, 1, HEAD_DIM, D_MODEL)
    slot = jnp.eye(N_KV, dtype=F32).reshape(N_KV, 1, N_KV, 1, 1)
    return (wa * slot).reshape(QPAD, D_MODEL).astype(BF16), w_out[NSA_WIDTH:].astype(BF16)


def _outproj(x2d, a_out, s_out, wa, ws, tm):
    n = x2d.shape[0]
    row = lambda w: pl.BlockSpec((tm, w), lambda i: (i, 0))
    return pl.pallas_call(
        _outproj_kernel,
        grid=(n // tm,),
        in_specs=[row(D_MODEL), row(QPAD), row(SSM_WIDTH),
                  pl.BlockSpec((QPAD, D_MODEL), lambda i: (0, 0)),
                  pl.BlockSpec((SSM_WIDTH, D_MODEL), lambda i: (0, 0))],
        out_specs=row(D_MODEL),
        out_shape=jax.ShapeDtypeStruct((n, D_MODEL), F32),
        compiler_params=_cparams("parallel"),
        name="outproj",
    )(x2d, a_out, s_out, wa, ws)


PEER_HALF = PEER_HEADS * PEER_DK // 2
_CAND_COUNTS = tuple(PEER_TOPK // (a + 1) for a in range(PEER_TOPK))
N_CAND = sum(_CAND_COUNTS)
N_CAND_PAD = -(-N_CAND // 8) * 8


def _peer_route_kernel(y_ref, g_ref, wq_ref, k1_ref, k2_ref, xn_ref, c1_ref, e1_ref, r2_ref, e2_ref,
                       s1_ref, s2_ref, v1_ref, v2_ref, cand_ref, rank1_ref, rank2_ref, rankc_ref):
    xn = _rmsnorm(y_ref[...], g_ref[...]).astype(BF16)
    xn_ref[...] = xn
    q = _dot(xn, wq_ref[...]).astype(BF16)
    s1_ref[...] = _nt_dot(k1_ref[...], q[:, :PEER_HALF])
    s2_ref[...] = _nt_dot(k2_ref[...], q[:, PEER_HALF:])
    t = y_ref.shape[0]
    krows = lambda h: slice(h * N_KEYS, (h + 1) * N_KEYS)
    crows = lambda h: slice(h * N_CAND_PAD, (h + 1) * N_CAND_PAD)

    def key_ranks(exact):
        worst = jnp.zeros((1, t), F32)
        for h in range(PEER_HEADS):
            for s_ref, rank_ref, v_ref in ((s1_ref, rank1_ref, v1_ref), (s2_ref, rank2_ref, v2_ref)):
                rank = _take_top(s_ref[krows(h), :], PEER_TOPK, v_ref, h * PEER_TOPK, exact)
                rank_ref[krows(h), :] = rank
                worst = jnp.maximum(worst, _n_taken(rank, PEER_TOPK))
        return worst

    worst = key_ranks(False)

    @pl.when(jnp.max(worst) > float(PEER_TOPK))
    def _():
        key_ranks(True)

    for h in range(PEER_HEADS):
        v1 = v1_ref[h * PEER_TOPK:(h + 1) * PEER_TOPK, :]
        v2 = v2_ref[h * PEER_TOPK:(h + 1) * PEER_TOPK, :]
        off = h * N_CAND_PAD
        for a, nb in enumerate(_CAND_COUNTS):
            cand_ref[off:off + nb, :] = v1[a:a + 1, :] + v2[0:nb, :]
            off += nb
        cand_ref[off:(h + 1) * N_CAND_PAD, :] = jnp.full(((h + 1) * N_CAND_PAD - off, t), LOWEST, F32)

    def cand_ranks(exact):
        worst = jnp.zeros((1, t), F32)
        for h in range(PEER_HEADS):
            rank = _take_top(cand_ref[crows(h), :], PEER_TOPK, exact=exact)
            rankc_ref[crows(h), :] = rank
            worst = jnp.maximum(worst, _n_taken(rank, PEER_TOPK))
        return worst

    worst = cand_ranks(False)

    @pl.when(jnp.max(worst) > float(PEER_TOPK))
    def _():
        cand_ranks(True)

    for h in range(PEER_HEADS):
        rows = krows(h)
        top = slice(h * PEER_TOPK, h * PEER_TOPK + 1)
        cand = cand_ref[crows(h), :]
        taken = jnp.where(rankc_ref[crows(h), :] < float(PEER_TOPK), 1.0, 0.0)
        z = jnp.sum(taken * jnp.exp(cand - cand[0:1, :]), axis=0, keepdims=True)
        rank1 = rank1_ref[rows, :]
        count = jnp.zeros((N_KEYS, t), F32)
        off = 0
        for a, nb in enumerate(_CAND_COUNTS):
            n_a = jnp.sum(taken[off:off + nb, :], axis=0, keepdims=True)
            count = jnp.where(rank1 == float(a), n_a, count)
            off += nb
        c1_ref[rows, :] = count
        e1_ref[rows, :] = jnp.exp(s1_ref[rows, :] - v1_ref[top, :]) / z
        r2_ref[rows, :] = rank2_ref[rows, :].astype(BF16)
        e2_ref[rows, :] = jnp.exp(s2_ref[rows, :] - v2_ref[top, :]).astype(BF16)


def _peer_weights(w_q, sub_k1, sub_k2):
    wq = w_q.reshape(D_MODEL, PEER_HEADS, 2, PEER_DK // 2).transpose(0, 2, 1, 3).reshape(D_MODEL, 2 * PEER_HALF)
    eye = jnp.eye(PEER_HEADS, dtype=F32)
    bd = lambda k: jnp.einsum("hkd,hj->hkjd", k, eye).reshape(PEER_HEADS * N_KEYS, PEER_HALF)
    return wq.astype(BF16), bd(sub_k1).astype(BF16), bd(sub_k2).astype(BF16)


def _peer_route(y2d, norm_g, wq, k1bd, k2bd, tm):
    n = y2d.shape[0]
    hk = PEER_HEADS * N_KEYS
    const = lambda a: pl.BlockSpec(a.shape, lambda i: (0, 0))
    col = lambda r: pl.BlockSpec((r, tm), lambda i: (0, i))
    tshape = lambda dt: jax.ShapeDtypeStruct((hk, n), dt)
    return pl.pallas_call(
        _peer_route_kernel,
        grid=(n // tm,),
        in_specs=[pl.BlockSpec((tm, D_MODEL), lambda i: (i, 0)),
                  pl.BlockSpec((1, D_MODEL), lambda i: (0, 0)), const(wq), const(k1bd), const(k2bd)],
        out_specs=[pl.BlockSpec((tm, D_MODEL), lambda i: (i, 0)), col(hk), col(hk), col(hk), col(hk)],
        out_shape=[jax.ShapeDtypeStruct((n, D_MODEL), BF16),
                   tshape(F32),
                   tshape(F32),
                   tshape(BF16),
                   tshape(BF16)],
        scratch_shapes=[pltpu.VMEM((hk, tm), F32), pltpu.VMEM((hk, tm), F32),
                        pltpu.VMEM((PEER_HEADS * PEER_TOPK, tm), F32),
                        pltpu.VMEM((PEER_HEADS * PEER_TOPK, tm), F32),
                        pltpu.VMEM((PEER_HEADS * N_CAND_PAD, tm), F32),
                        pltpu.VMEM((hk, tm), F32), pltpu.VMEM((hk, tm), F32),
                        pltpu.VMEM((PEER_HEADS * N_CAND_PAD, tm), F32)],
        compiler_params=_cparams("parallel"),
        name="peer_route",
    )(y2d, norm_g.reshape(1, D_MODEL), wq, k1bd, k2bd)


EXPERT_BLOCK = 1024
I1_PER_BLOCK = EXPERT_BLOCK // N_KEYS
N_EXPERT_BLOCKS = N_EXPERTS // EXPERT_BLOCK


def _peer_dense_kernel(y_ref, xn_ref, u_ref, vt_ref, c1_ref, e1_ref, r2_ref, e2_ref,
                       gf_ref, out_ref, acc_ref, act_ref, ga_ref):
    j = pl.program_id(1)
    t = xn_ref.shape[0]

    @pl.when(j == 0)
    def _():
        acc_ref[...] = jnp.zeros(acc_ref.shape, F32)
        act_ref[...] = jnp.zeros(act_ref.shape, BF16)
        ga_ref[...] = jnp.zeros(ga_ref.shape, BF16)

    def gate_stage(prev):
        jb = j - 1
        for ii in range(I1_PER_BLOCK):
            erows = slice(ii * N_KEYS, (ii + 1) * N_KEYS)
            for tc in range(t // LANES):
                cols = slice(tc * LANES, (tc + 1) * LANES)
                gate = jnp.zeros((N_KEYS, LANES), BF16)
                for h in range(PEER_HEADS):
                    grp = pl.ds(pl.multiple_of(h * N_KEYS + jb * I1_PER_BLOCK, I1_PER_BLOCK), I1_PER_BLOCK)
                    krows = slice(h * N_KEYS, (h + 1) * N_KEYS)
                    count = c1_ref[grp, cols][ii:ii + 1, :].astype(BF16)
                    e1 = e1_ref[grp, cols][ii:ii + 1, :].astype(BF16)
                    gate = gate + jnp.where(r2_ref[krows, cols] < count, e2_ref[krows, cols] * e1,
                                            jnp.zeros((), BF16))
                ga_ref[prev, erows, cols] = gate * act_ref[prev, erows, cols]

    def stages(cur, prev):
        pl.when((j >= 1) & (j <= N_EXPERT_BLOCKS))(functools.partial(gate_stage, prev))

        acc_ref[...] += _dot(vt_ref[...], ga_ref[cur])

        act_ref[cur] = _gelu_tanh(_nt_dot(u_ref[...], xn_ref[...])).astype(BF16)

    for parity in range(2):
        pl.when(j % 2 == parity)(functools.partial(stages, parity, 1 - parity))

    @pl.when(j == pl.num_programs(1) - 1)
    def _():
        y = y_ref[...] + acc_ref[...].T
        out_ref[...] = _rmsnorm(y, gf_ref[...])


def _peer_dense(y2d, xn, u_bf, vt_bf, c1, e1, r2, e2, norm_f, tm):
    n = y2d.shape[0]
    hk = PEER_HEADS * N_KEYS
    tok = lambda w: pl.BlockSpec((tm, w), lambda i, j: (i, 0))
    col = lambda r: pl.BlockSpec((r, tm), lambda i, j: (0, i))
    last = N_EXPERT_BLOCKS - 1
    return pl.pallas_call(
        _peer_dense_kernel,
        grid=(n // tm, N_EXPERT_BLOCKS + 2),
        in_specs=[tok(D_MODEL), tok(D_MODEL),
                  pl.BlockSpec((EXPERT_BLOCK, D_MODEL), lambda i, j: (jnp.minimum(j, last), 0)),
                  pl.BlockSpec((D_MODEL, EXPERT_BLOCK), lambda i, j: (0, jnp.maximum(j - 2, 0))),
                  col(hk), col(hk), col(hk), col(hk),
                  pl.BlockSpec((1, D_MODEL), lambda i, j: (0, 0))],
        out_specs=tok(D_MODEL),
        out_shape=jax.ShapeDtypeStruct((n, D_MODEL), F32),
        scratch_shapes=[pltpu.VMEM((D_MODEL, tm), F32),
                        pltpu.VMEM((2, EXPERT_BLOCK, tm), BF16),
                        pltpu.VMEM((2, EXPERT_BLOCK, tm), BF16)],
        compiler_params=_cparams("parallel", "arbitrary"),
        name="peer_dense",
    )(y2d, xn, u_bf, vt_bf, c1, e1, r2, e2, norm_f.reshape(1, D_MODEL))


PROJ_TILE = 512
PEER_TILE = 256
PEER_DENSE_TILE = 512
S5_CHUNK = 256


def _state_to_rows(st):
    s = st.shape[0]
    return jnp.concatenate([st[..., 0].reshape(s, 1, N_STATE), st[..., 1].reshape(s, 1, N_STATE)], axis=-1)


def _rows_to_state(h):
    s = h.shape[0]
    shape = (s, N_SSM_GROUPS, SSM_STATE)
    return jnp.stack([h[:, 0, :N_STATE].reshape(shape), h[:, 0, N_STATE:].reshape(shape)], axis=-1)


def _feat_to_tokens(a, n_types):
    s, _, t = a.shape
    return a.reshape(s, n_types, N_KV, HEAD_DIM, t).transpose(0, 4, 1, 2, 3)


def _peer_block(y2d, norm_g, peer_w, norm_f, tm):
    wq, k1bd, k2bd, u_bf, vt_bf = peer_w
    xn, c1, e1, r2, e2 = _peer_route(y2d, norm_g, wq, k1bd, k2bd, tm)
    return _peer_dense(y2d, xn, u_bf, vt_bf, c1, e1, r2, e2, norm_f, min(PEER_DENSE_TILE, y2d.shape[0]))


def kernel(x_prompt, x_sample, cache_kv, cache_win, state_ssm, page_table, norm_mix, w_in, w_cmp1, w_cmp2, pe_cmp, lam_re, lam_im, log_dt, b_re, b_im, c_re, c_im, d_skip, w_out, norm_ffn, w_q_peer, sub_k1, sub_k2, u_tab, v_tab, norm_final):
    b, t, d = x_prompt.shape
    db, ts, _ = x_sample.shape
    assert w_in.shape[0] == DEPTH == 1 and d == D_MODEL
    l = 0
    n_pool = cache_kv.shape[1]
    wb = cache_win.shape[2]

    w_tok, w_feat = _proj_weights(w_in[l])
    cmp_w = _compress_weights(w_cmp1[l], w_cmp2[l], pe_cmp[l])
    s5_w = _s5_weights(lam_re[l], lam_im[l], log_dt[l], b_re[l], b_im[l], c_re[l], c_im[l], d_skip[l])
    wa, ws = _outproj_weights(w_out[l])
    peer_w = _peer_weights(w_q_peer[l], sub_k1[l], sub_k2[l]) + (
        u_tab[l].astype(BF16), v_tab[l].T.astype(BF16))

    q, gate, u, z, kswk, q_t, kvsel_t, ksvs_t, kvwin_t, kwvw_t = _project(
        x_prompt, norm_mix[l], w_tok, w_feat, PROJ_TILE, BF16)
    n_pg = t // PAGE_SIZE
    table_p = jnp.broadcast_to(jnp.arange(n_pg, dtype=jnp.int32), (b, n_pg))
    prompt_page = lambda k: pl.BlockSpec((None, 2, LANES, PAGE_SIZE),
                                         lambda bb, j, tb: (bb, 0, 0, tb[bb, j * CMP_PAGES + k]))
    cmp_p = _compress(kvsel_t.reshape(b, 4, LANES, t), prompt_page, table_p, *cmp_w)
    a_p = _nsa_prompt_t(q.reshape(b, t, QPAD), q_t, gate.reshape(b, t, LANES),
                        kswk.reshape(b, t, 2 * KV_WIDTH), ksvs_t, kwvw_t, cmp_p)
    s_p, h_p = _s5(u.reshape(b, t, SSM_WIDTH), z.reshape(b, t, SSM_WIDTH),
                   jnp.zeros((b, 1, 2 * N_STATE), F32), s5_w, S5_CHUNK, BF16)
    y1p = _outproj(x_prompt.reshape(b * t, d), a_p.reshape(b * t, QPAD), s_p.reshape(b * t, SSM_WIDTH),
                   wa, ws, PROJ_TILE)
    y_prompt = _peer_block(y1p, norm_ffn[l], peer_w, norm_final, PEER_TILE).reshape(b, t, d)
    kv_prompt = _feat_to_tokens(kvsel_t, 4)
    win_prompt = _feat_to_tokens(kvwin_t[:, :, t - min(WINDOW, t):], 2)
    ssm_prompt = _rows_to_state(h_p)

    qs, gate_s, u_s, z_s, _, _, kvsel_st, ksvs_st, kvwin_st, kwvw_st = _project(
        x_sample.reshape(1, db * ts, d), norm_mix[l], w_tok, w_feat, db * ts, F32)
    pages_s = cache_kv[l].transpose(0, 2, 3, 4, 1).reshape(n_pool, 4, LANES, PAGE_SIZE)
    sample_page = lambda k: pl.BlockSpec((None, 2, LANES, PAGE_SIZE),
                                         lambda bb, j, tb: (tb[bb, j * CMP_PAGES + k], 0, 0, 0))
    cmp_s = _compress(pages_s, sample_page, page_table, *cmp_w)
    new_page = lambda a: jnp.pad(a[0].reshape(2 * LANES, db, ts).transpose(1, 0, 2),
                                 ((0, 0), (0, 0), (0, PAGE_SIZE - ts)))
    win_t = cache_win[l].transpose(0, 2, 3, 4, 1).reshape(db, 2 * LANES, wb)
    a_s = _nsa_sample(qs.reshape(db, ts, QPAD), gate_s.reshape(db, ts, LANES), pages_s, page_table, cmp_s,
                      win_t, new_page(ksvs_st), new_page(kwvw_st))
    s_s, h_s = _s5(u_s.reshape(db, ts, SSM_WIDTH), z_s.reshape(db, ts, SSM_WIDTH),
                   _state_to_rows(state_ssm[l].astype(F32)), s5_w, ts, F32)
    y1s = _outproj(x_sample.reshape(db * ts, d), a_s.reshape(db * ts, QPAD), s_s.reshape(db * ts, SSM_WIDTH),
                   wa, ws, db * ts)
    y_sample = _peer_block(y1s, norm_ffn[l], peer_w, norm_final, PEER_TILE).reshape(db, ts, d)
    per_tok = lambda a, n_types: a[0].reshape(n_types, N_KV, HEAD_DIM, db, ts).transpose(3, 4, 0, 1, 2)
    kv_sample = per_tok(kvsel_st, 4)
    win_new = per_tok(kvwin_st, 2).astype(cache_win.dtype)
    win_sample = jnp.concatenate([cache_win[l], win_new], axis=1)[:, ts:]
    ssm_sample = _rows_to_state(h_s)

    return (y_prompt, y_sample, kv_prompt[None], kv_sample[None], win_prompt[None], win_sample[None],
            ssm_prompt[None], ssm_sample[None])
```

```python
import functools
import math

import jax
import jax.numpy as jnp
from jax import lax
from jax.experimental import pallas as pl
from jax.experimental.pallas import tpu as pltpu

D_MODEL = 1024
DEPTH = 1
PAGE_SIZE = 128
N_HEADS = 8
N_KV = 2
HEAD_DIM = 64
HPG = N_HEADS // N_KV
CMP_LEN = 32
CMP_STRIDE = 16
SEL_BLOCK = 64
N_SEL = 16
WINDOW = 512
SSM_WIDTH = 512
SSM_GROUP = 16
N_SSM_GROUPS = SSM_WIDTH // SSM_GROUP
SSM_STATE = 64
N_KEYS = 128
N_EXPERTS = N_KEYS * N_KEYS
PEER_HEADS = 8
PEER_DK = 128
PEER_TOPK = 16
NSA_WIDTH = N_HEADS * HEAD_DIM
KV_WIDTH = N_KV * HEAD_DIM
GATE_WIDTH = 3 * N_HEADS
EPS = 1e-6
NEG = -1e30
FORCE = 1e4
LOG2_E = 1.0 / math.log(2.0)
LOWEST = -3.0e38

LANES = 128
BF16_ROWS = 16
VMEM_LIMIT = 56 * 1024 * 1024

N_STATE = N_SSM_GROUPS * SSM_STATE
QPAD = N_HEADS * LANES
F32 = jnp.float32
BF16 = jnp.bfloat16

assert KV_WIDTH == LANES and PAGE_SIZE == LANES


def _cparams(*sem):
    return pltpu.CompilerParams(dimension_semantics=sem, vmem_limit_bytes=VMEM_LIMIT)


def _nt_dot(a, b):
    return lax.dot_general(a, b, (((1,), (1,)), ((), ())), preferred_element_type=F32)


def _dot(a, b):
    return jnp.dot(a, b, preferred_element_type=F32)


def _rmsnorm(x, g):
    return x * lax.rsqrt(jnp.mean(x * x, axis=-1, keepdims=True) + EPS) * g


_GELU_C1 = 2.0 * math.sqrt(2.0 / math.pi) * LOG2_E
_GELU_C3 = _GELU_C1 * 0.044715


def _gelu_tanh(x):
    return x / (1.0 + jnp.exp2(x * (-_GELU_C1 - _GELU_C3 * (x * x))))


_PROJ_COLS = (("q", QPAD), ("gate", LANES), ("u", SSM_WIDTH), ("z", SSM_WIDTH), ("kswk", 2 * KV_WIDTH))
KVT_ROWS = 6 * KV_WIDTH


def _proj_kernel(x_ref, g_ref, w_ref, wt_ref, q_ref, gate_ref, u_ref, z_ref, kswk_ref,
                 qt_ref, kvsel_ref, ksvs_ref, kvwin_ref, kwvw_ref):
    xn = _rmsnorm(x_ref[...], g_ref[...]).astype(BF16)
    off = 0
    for (name, width), ref in zip(_PROJ_COLS, (q_ref, gate_ref, u_ref, z_ref, kswk_ref)):
        ref[...] = _dot(xn, w_ref[:, off:off + width]).astype(ref.dtype)
        off += width
    qt_ref[0] = _nt_dot(wt_ref[0:QPAD, :], xn).astype(BF16)
    kvt = _nt_dot(wt_ref[QPAD:, :], xn)
    kvsel_ref[0] = kvt[0:4 * KV_WIDTH]
    ksvs_ref[0] = kvt[2 * KV_WIDTH:4 * KV_WIDTH].astype(BF16)
    kvwin_ref[0] = kvt[4 * KV_WIDTH:]
    kwvw_ref[0] = kvt[4 * KV_WIDTH:].astype(BF16)


def _proj_weights(w_in):
    c0 = NSA_WIDTH
    c1 = c0 + 6 * KV_WIDTH
    c2 = c1 + GATE_WIDTH
    wq = w_in[:, :c0].reshape(D_MODEL, N_KV, HPG, 1, HEAD_DIM) * (HEAD_DIM ** -0.5 * LOG2_E)
    slot = jnp.eye(N_KV, dtype=F32).reshape(1, N_KV, 1, N_KV, 1)
    wq_pad = (wq * slot).reshape(D_MODEL, QPAD)
    wg = jnp.pad(w_in[:, c1:c2], ((0, 0), (0, LANES - GATE_WIDTH)))
    ks = w_in[:, c0 + 2 * KV_WIDTH:c0 + 3 * KV_WIDTH]
    kw = w_in[:, c0 + 4 * KV_WIDTH:c0 + 5 * KV_WIDTH]
    w = jnp.concatenate([wq_pad, wg, w_in[:, c2:], ks, kw], axis=1)
    wt = jnp.concatenate([wq_pad, w_in[:, c0:c1]], axis=1).T
    return w.astype(BF16), wt.astype(BF16)


def _project(x3d, norm_g, w_tok, w_feat, tm, q_dtype):
    s, t, _ = x3d.shape
    nt = t // tm
    n = s * t
    row = lambda w: pl.BlockSpec((tm, w), lambda b, i: (b * nt + i, 0))
    feat = lambda r: pl.BlockSpec((1, r, tm), lambda b, i: (b, 0, i))
    const = lambda a: pl.BlockSpec(a.shape, lambda b, i: (0, 0))
    shapes = [
        jax.ShapeDtypeStruct((n, QPAD), q_dtype),
        jax.ShapeDtypeStruct((n, LANES), F32),
        jax.ShapeDtypeStruct((n, SSM_WIDTH), F32),
        jax.ShapeDtypeStruct((n, SSM_WIDTH), F32),
        jax.ShapeDtypeStruct((n, 2 * KV_WIDTH), BF16),
        jax.ShapeDtypeStruct((s, QPAD, t), BF16),
        jax.ShapeDtypeStruct((s, 4 * KV_WIDTH, t), F32),
        jax.ShapeDtypeStruct((s, 2 * KV_WIDTH, t), BF16),
        jax.ShapeDtypeStruct((s, 2 * KV_WIDTH, t), F32),
        jax.ShapeDtypeStruct((s, 2 * KV_WIDTH, t), BF16),
    ]
    g2 = norm_g.reshape(1, D_MODEL)
    return pl.pallas_call(
        _proj_kernel,
        grid=(s, nt),
        in_specs=[row(D_MODEL), const(g2), const(w_tok), const(w_feat)],
        out_specs=[row(sh.shape[1]) for sh in shapes[:5]] + [feat(sh.shape[1]) for sh in shapes[5:]],
        out_shape=shapes,
        compiler_params=_cparams("parallel", "parallel"),
        name="proj",
    )(x3d.reshape(n, D_MODEL), g2, w_tok, w_feat)


CMP_PAGES = 16
CMP_PAIRS = CMP_STRIDE // 2


def _compress_kernel(tbl_ref, *refs, n_pages):
    del tbl_ref
    page_refs = refs[:CMP_PAGES]
    w1_ref, pe_ref, w2_ref, out_ref, slabk_ref, slabv_ref = refs[CMP_PAGES:]
    j = pl.program_id(1)
    seq = n_pages * PAGE_SIZE
    slabs = (slabk_ref, slabv_ref)

    for k, page in enumerate(page_refs):
        rows = pl.ds(pl.multiple_of((j * CMP_PAGES + k) * PAGE_SIZE, PAGE_SIZE), PAGE_SIZE)
        for c, slab in enumerate(slabs):
            slab[rows, :] = page[c].T

    @pl.when(j == n_pages // CMP_PAGES - 1)
    def _():
        nb = seq // CMP_STRIDE
        for c, slab in enumerate(slabs):
            head = jnp.zeros((nb, LANES), F32)
            tail = jnp.zeros((nb, LANES), F32)
            for m in range(CMP_PAIRS):
                x = jnp.concatenate([slab[pl.ds(m, nb, stride=CMP_STRIDE), :],
                                     slab[pl.ds(m + CMP_PAIRS, nb, stride=CMP_STRIDE), :]], axis=1)
                head = head + _dot((x + pe_ref[c, m:m + 1, :]).astype(BF16), w1_ref[c, m])
                m2 = CMP_PAIRS + m
                tail = tail + _dot((x + pe_ref[c, m2:m2 + 1, :]).astype(BF16), w1_ref[c, m2])
            hid = head + pltpu.roll(tail, nb - 1, 0)
            out = _dot(jax.nn.gelu(hid).astype(BF16), w2_ref[c])
            out_ref[0, :, c * LANES:(c + 1) * LANES] = out.astype(BF16)


def _compress_weights(w1, w2, pe):
    eye = jnp.eye(N_KV, dtype=F32)
    bd = lambda m: jnp.einsum("...de,gk->...gdke", m, eye).reshape(m.shape[:-2] + (LANES, LANES))
    pair = lambda a: a.reshape(2, 2, 2, CMP_PAIRS, *a.shape[2:]).swapaxes(2, 3)
    w1p = pair(bd(w1)).reshape(2, 2 * CMP_PAIRS, 2 * LANES, LANES)
    pe2 = jnp.concatenate([pe, pe], axis=-1)
    pep = pair(pe2).reshape(2, 2 * CMP_PAIRS, 2 * LANES)
    return w1p.astype(BF16), pep, bd(w2).astype(BF16)


def _compress(pages, page_spec, table, w1bd, pe2, w2bd):
    s, p = table.shape
    assert CMP_LEN == 2 * CMP_STRIDE and p % CMP_PAGES == 0
    seq = p * PAGE_SIZE
    const = lambda a: pl.BlockSpec(a.shape, lambda b, j, t: (0,) * a.ndim)
    grid_spec = pltpu.PrefetchScalarGridSpec(
        num_scalar_prefetch=1,
        grid=(s, p // CMP_PAGES),
        in_specs=[page_spec(k) for k in range(CMP_PAGES)] + [const(w1bd), const(pe2), const(w2bd)],
        out_specs=pl.BlockSpec((1, seq // CMP_STRIDE, 2 * LANES), lambda b, j, t: (b, 0, 0)),
        scratch_shapes=[pltpu.VMEM((seq, LANES), F32), pltpu.VMEM((seq, LANES), F32)],
    )
    return pl.pallas_call(
        functools.partial(_compress_kernel, n_pages=p),
        grid_spec=grid_spec,
        out_shape=jax.ShapeDtypeStruct((s, seq // CMP_STRIDE, 2 * LANES), BF16),
        compiler_params=_cparams("parallel", "arbitrary"),
        name="compress",
    )(table, *([pages] * CMP_PAGES), w1bd, pe2, w2bd)


Q_TILE = 128
KEY_CHUNK = 1024
N_BLK_PAD = 128
LOG2_SEL_BLOCK = int(math.log2(SEL_BLOCK))
WIN_KEYS = WINDOW + Q_TILE


def _cover_t(n_cmp_pad):
    n = jnp.arange(n_cmp_pad, dtype=jnp.int32)[None, :] * CMP_STRIDE
    s = jnp.arange(N_BLK_PAD, dtype=jnp.int32)[:, None] * SEL_BLOCK
    return ((n < s + SEL_BLOCK) & (n + CMP_LEN > s)).astype(BF16)


def _split3_nt(w, x):
    hi = x.astype(BF16)
    r1 = x - hi.astype(F32)
    mid = r1.astype(BF16)
    lo = (r1 - mid.astype(F32)).astype(BF16)
    return _nt_dot(w, hi) + _nt_dot(w, mid) + _nt_dot(w, lo)


def _take_top(s, n, out_ref=None, out_row=0, exact=True):
    kidx = lax.broadcasted_iota(jnp.int32, s.shape, 0).astype(F32)
    rem = s
    rank = jnp.full(s.shape, float(n), F32)
    for a in range(n):
        mx = jnp.max(rem, axis=0, keepdims=True)
        if exact:
            first = jnp.min(jnp.where(rem == mx, kidx, float(s.shape[0])), axis=0, keepdims=True)
            taken = kidx == first
        else:
            taken = rem == mx
        if out_ref is not None:
            out_ref[out_row + a:out_row + a + 1, :] = mx
        rank = jnp.where(taken, float(a), rank)
        rem = jnp.where(taken, LOWEST, rem)
    return rank


def _n_taken(rank, n):
    return jnp.sum(jnp.where(rank < float(n), 1.0, 0.0), axis=0, keepdims=True)


def _topk_block_mask(imp_t, n_keep):
    return jnp.where(_take_top(imp_t, n_keep) < float(n_keep), 0.0, NEG)


def _cmp_branch(q, s_ref, cmp_ref, ocmp_ref, t_pos, rows):
    ncp = cmp_ref.shape[1]
    kc = cmp_ref[0, :, 0:LANES]
    vc = cmp_ref[0, :, LANES:2 * LANES]
    s_ref[:, 0:ncp] = _nt_dot(q, kc)
    n_idx = lax.broadcasted_iota(jnp.int32, (rows, ncp), 1)
    ok = (n_idx * CMP_STRIDE + (CMP_LEN - 1) <= t_pos) & (n_idx < ncp - 1)
    imps = []
    for g in range(N_KV):
        psum = jnp.zeros((rows, ncp), F32)
        for h in range(HPG):
            hh = g * HPG + h
            s = jnp.where(ok, s_ref[hh * rows:(hh + 1) * rows, 0:ncp], NEG)
            m = jnp.max(s, axis=-1, keepdims=True)
            e = jnp.where(ok, jnp.exp2(s - m), 0.0)
            l = jnp.sum(e, axis=-1, keepdims=True)
            p = e * jnp.where(l > 0.0, 1.0 / l, 0.0)
            ocmp_ref[hh] = _dot(p.astype(BF16), vc)
            psum = psum + p
        imps.append(psum)
    return imps


def _with_ones_rows(vt, g):
    row = lax.broadcasted_iota(jnp.int32, vt.shape, 0)
    own = (row >= g * HEAD_DIM) & (row < (g + 1) * HEAD_DIM)
    return jnp.where(own, vt, jnp.ones((), vt.dtype))


def _block_expand(t):
    key_blk = jnp.arange(t, dtype=jnp.int32)[None, :] // SEL_BLOCK
    return (jnp.arange(N_BLK_PAD, dtype=jnp.int32)[:, None] == key_blk).astype(BF16)


def _nsa_prompt_t_kernel(q_ref, qt_ref, gate_ref, ktok_ref, vs_ref, vw_ref, cmp_ref, covt_ref, expt_ref,
                         out_ref, qs_ref, qst_ref, sc_ref, s_ref, p_ref, m_ref, acc_ref, ocmp_ref):
    i = pl.program_id(1)
    t0 = i * Q_TILE
    group_cols = HPG * Q_TILE
    for h in range(N_HEADS):
        cols = slice(h * Q_TILE, (h + 1) * Q_TILE)
        qs_ref[cols, :] = q_ref[0, :, h * LANES:(h + 1) * LANES]
        qst_ref[0:LANES, cols] = qt_ref[0, h * LANES:(h + 1) * LANES, :]
    t_col = t0 + lax.broadcasted_iota(jnp.int32, (Q_TILE, 1), 0)

    psums = _cmp_branch(qs_ref[...], sc_ref, cmp_ref, ocmp_ref, t_col, Q_TILE)
    s_idx = lax.broadcasted_iota(jnp.int32, (N_BLK_PAD, Q_TILE), 0)
    t_row = t0 + lax.broadcasted_iota(jnp.int32, (N_BLK_PAD, Q_TILE), 1)
    cur = jnp.right_shift(t_row, LOG2_SEL_BLOCK)
    forced = (s_idx == 0) | (s_idx == cur) | (s_idx == cur - 1)
    for g in range(N_KV):
        imp_t = _split3_nt(covt_ref[...], psums[g])
        imp_t = jnp.where(forced, FORCE, imp_t)
        imp_t = jnp.where(s_idx * SEL_BLOCK <= t_row, imp_t, NEG)
        mask_t = _topk_block_mask(imp_t, N_SEL).astype(BF16)
        for h in range(HPG):
            hh = g * HPG + h
            qst_ref[LANES:2 * LANES, hh * Q_TILE:(hh + 1) * Q_TILE] = mask_t

    def attend(n_keys, v_aug, bias_t):
        for g in range(N_KV):
            alphas = []
            for h in range(HPG):
                hh = g * HPG + h
                s = s_ref[0:n_keys, hh * Q_TILE:(hh + 1) * Q_TILE]
                if bias_t is not None:
                    s = s + bias_t
                m_old = m_ref[hh:hh + 1, :]
                m_new = jnp.maximum(m_old, jnp.max(s, axis=0, keepdims=True))
                p_ref[0:n_keys, h * Q_TILE:(h + 1) * Q_TILE] = jnp.exp2(s - m_new).astype(BF16)
                alphas.append(jnp.exp2(m_old - m_new))
                m_ref[hh:hh + 1, :] = m_new
            alpha = jnp.concatenate(alphas, axis=1)
            acc_ref[g] = alpha * acc_ref[g] + _dot(v_aug[g], p_ref[0:n_keys, :])

    def reset():
        m_ref[...] = jnp.full(m_ref.shape, NEG, F32)
        acc_ref[...] = jnp.zeros(acc_ref.shape, F32)

    def outputs():
        outs = []
        for g in range(N_KV):
            lsum = (1 - g) * HEAD_DIM
            for h in range(HPG):
                a = acc_ref[g, :, h * Q_TILE:(h + 1) * Q_TILE]
                outs.append(a / a[lsum:lsum + 1, :])
        return outs

    reset()

    def chunk(c, diagonal, n_keys=KEY_CHUNK):
        k0 = pl.multiple_of(c * KEY_CHUNK, KEY_CHUNK)
        keys = pl.ds(k0, n_keys)
        lhs = jnp.concatenate([ktok_ref[0, keys, 0:LANES], expt_ref[keys, :]], axis=1)
        s_ref[0:n_keys, :] = _dot(lhs, qst_ref[...])
        vt = vs_ref[0, LANES:2 * LANES, keys]
        causal = None
        if diagonal:
            key_pos = k0 + lax.broadcasted_iota(jnp.int32, (n_keys, Q_TILE), 0)
            causal = jnp.where(key_pos <= t0 + lax.broadcasted_iota(jnp.int32, (n_keys, Q_TILE), 1), 0.0, NEG)
        attend(n_keys, [_with_ones_rows(vt, g) for g in range(N_KV)], causal)

    n_full = t0 // KEY_CHUNK

    def full_chunk(c, carry):
        chunk(c, False)
        return carry

    lax.fori_loop(0, n_full, full_chunk, 0)
    first_half = t0 - n_full * KEY_CHUNK < KEY_CHUNK // 2
    pl.when(first_half)(functools.partial(chunk, n_full, True, KEY_CHUNK // 2))
    pl.when(jnp.logical_not(first_half))(functools.partial(chunk, n_full, True))
    o_sel = outputs()

    reset()
    start = pl.multiple_of(jnp.maximum(t0 - WINDOW, 0), Q_TILE)
    wkeys = pl.ds(start, WIN_KEYS)
    s_ref[0:WIN_KEYS, :] = _dot(ktok_ref[0, wkeys, LANES:2 * LANES], qst_ref[0:LANES, :])
    vwt = vw_ref[0, LANES:2 * LANES, wkeys]
    dist = (t0 + lax.broadcasted_iota(jnp.int32, (WIN_KEYS, Q_TILE), 1)
            - (start + lax.broadcasted_iota(jnp.int32, (WIN_KEYS, Q_TILE), 0)))
    bias_w = jnp.where((dist >= 0) & (dist <= WINDOW), 0.0, NEG)
    attend(WIN_KEYS, [_with_ones_rows(vwt, g) for g in range(N_KV)], bias_w)
    o_win = outputs()

    gates = jax.nn.sigmoid(gate_ref[0])
    gates_t = gates.T
    for hh in range(N_HEADS):
        o_t = gates_t[3 * hh + 1:3 * hh + 2, :] * o_sel[hh] + gates_t[3 * hh + 2:3 * hh + 3, :] * o_win[hh]
        o = gates[:, 3 * hh:3 * hh + 1] * ocmp_ref[hh] + o_t.T
        out_ref[0, :, hh * LANES:(hh + 1) * LANES] = o.astype(out_ref.dtype)


def _nsa_prompt_t(q, qt, gates, kswk, ksvs_t, kwvw_t, cmp):
    b, t, _ = q.shape
    ncp = cmp.shape[1]
    tile = lambda w: pl.BlockSpec((1, Q_TILE, w), lambda bb, i: (bb, i, 0))
    whole = lambda r, w: pl.BlockSpec((1, r, w), lambda bb, i: (bb, 0, 0))
    const = lambda r, w: pl.BlockSpec((r, w), lambda bb, i: (0, 0))
    rows = N_HEADS * Q_TILE
    expand_t = _block_expand(t).T
    return pl.pallas_call(
        _nsa_prompt_t_kernel,
        grid=(b, t // Q_TILE),
        in_specs=[tile(QPAD), pl.BlockSpec((1, QPAD, Q_TILE), lambda bb, i: (bb, 0, i)), tile(LANES),
                  whole(t, 2 * LANES), whole(2 * LANES, t), whole(2 * LANES, t),
                  whole(ncp, 2 * LANES), const(N_BLK_PAD, ncp), const(t, N_BLK_PAD)],
        out_specs=tile(QPAD),
        out_shape=jax.ShapeDtypeStruct((b, t, QPAD), BF16),
        scratch_shapes=[
            pltpu.VMEM((rows, LANES), BF16),
            pltpu.VMEM((2 * LANES, rows), BF16),
            pltpu.VMEM((rows, ncp), F32),
            pltpu.VMEM((max(WIN_KEYS, KEY_CHUNK), rows), F32),
            pltpu.VMEM((max(WIN_KEYS, KEY_CHUNK), HPG * Q_TILE), BF16),
            pltpu.VMEM((N_HEADS, Q_TILE), F32),
            pltpu.VMEM((N_KV, LANES, HPG * Q_TILE), F32),
            pltpu.VMEM((N_HEADS, Q_TILE, LANES), F32),
        ],
        compiler_params=_cparams("parallel", "arbitrary"),
        name="nsa_prompt",
    )(q, qt, gates, kswk, ksvs_t, kwvw_t, cmp, _cover_t(ncp), expand_t)


SEL_PAGES = 32


def _flash_update(s, vt, m_ref, l_ref, acc_ref):
    m_old = m_ref[...]
    m_new = jnp.maximum(m_old, jnp.max(s, axis=-1, keepdims=True))
    alpha = jnp.exp2(m_old - m_new)
    p = jnp.exp2(s - m_new[:, 0:1])
    l_ref[...] = alpha * l_ref[...] + jnp.sum(p, axis=-1, keepdims=True)
    acc_ref[...] = alpha * acc_ref[...] + _nt_dot(p.astype(BF16), vt)
    m_ref[...] = m_new


def _nsa_sample_kernel(tbl_ref, *refs, n_pages, tq):
    del tbl_ref
    q_ref, gate_ref = refs[:2]
    page_refs = refs[2:2 + SEL_PAGES]
    (cmp_ref, win_ref, newkv_ref, newwin_ref, covt_ref, expand_ref, out_ref, qs_ref, s_ref, msel_ref,
     m_ref, l_ref, acc_ref, ocmp_ref) = refs[2 + SEL_PAGES:]
    j = pl.program_id(1)
    past_len = n_pages * PAGE_SIZE
    rows = N_HEADS * tq
    step_keys = SEL_PAGES * PAGE_SIZE
    i_col = lax.broadcasted_iota(jnp.int32, (rows, 1), 0) & (tq - 1)

    @pl.when(j == 0)
    def _():
        for h in range(N_HEADS):
            qs_ref[h * tq:(h + 1) * tq, :] = q_ref[0, :, h * LANES:(h + 1) * LANES]
        t_col = past_len + lax.broadcasted_iota(jnp.int32, (tq, 1), 0)
        psums = _cmp_branch(qs_ref[...].astype(BF16), s_ref, cmp_ref, ocmp_ref, t_col, tq)
        s_idx = lax.broadcasted_iota(jnp.int32, (N_BLK_PAD, LANES), 0)
        t_row = past_len + lax.broadcasted_iota(jnp.int32, (N_BLK_PAD, LANES), 1)
        cur = jnp.right_shift(t_row, LOG2_SEL_BLOCK)
        forced = (s_idx == 0) | (s_idx == cur) | (s_idx == cur - 1)
        ncp = cmp_ref.shape[1]
        for g in range(N_KV):
            psum = jnp.concatenate([psums[g], jnp.zeros((LANES - tq, ncp), F32)], axis=0)
            imp_t = jnp.where(forced, FORCE, _split3_nt(covt_ref[...], psum))
            mask_t = _topk_block_mask(imp_t, N_SEL - 1)
            msel_ref[g] = mask_t.T.astype(BF16)
        m_ref[...] = jnp.full(m_ref.shape, NEG, F32)
        l_ref[...] = jnp.zeros(l_ref.shape, F32)
        acc_ref[...] = jnp.zeros(acc_ref.shape, F32)

    q = qs_ref[...].astype(BF16)
    kt = jnp.concatenate([page[0].astype(BF16) for page in page_refs], axis=1)
    vt = jnp.concatenate([page[1].astype(BF16) for page in page_refs], axis=1)
    bias_g = [_dot(msel_ref[g, 0:BF16_ROWS, :], expand_ref[...])[0:tq, :] for g in range(N_KV)]
    bias = jnp.concatenate([bias_g[hh // HPG] for hh in range(N_HEADS)], axis=0)
    _flash_update(_dot(q, kt) + bias, vt, m_ref, l_ref, acc_ref)

    @pl.when(j == n_pages // SEL_PAGES - 1)
    def _():
        j_lane = lax.broadcasted_iota(jnp.int32, (rows, PAGE_SIZE), 1)
        new_bias = jnp.where(j_lane <= i_col, 0.0, NEG)
        _flash_update(_dot(q, newkv_ref[0, 0:LANES, :]) + new_bias, newkv_ref[0, LANES:2 * LANES, :],
                      m_ref, l_ref, acc_ref)
        wb = win_ref.shape[2]
        dist = wb + i_col - lax.broadcasted_iota(jnp.int32, (rows, wb), 1)
        s1 = _dot(q, win_ref[0, 0:LANES, :].astype(BF16)) + jnp.where((dist >= 0) & (dist <= WINDOW), 0.0, NEG)
        s2 = _dot(q, newwin_ref[0, 0:LANES, :]) + new_bias
        m = jnp.maximum(jnp.max(s1, axis=-1, keepdims=True), jnp.max(s2, axis=-1, keepdims=True))
        p1 = jnp.exp2(s1 - m)
        p2 = jnp.exp2(s2 - m)
        l = jnp.sum(p1, axis=-1, keepdims=True) + jnp.sum(p2, axis=-1, keepdims=True)
        o_win = (_nt_dot(p1.astype(BF16), win_ref[0, LANES:2 * LANES, :].astype(BF16))
                 + _nt_dot(p2.astype(BF16), newwin_ref[0, LANES:2 * LANES, :])) / l
        o_sel = acc_ref[...] / l_ref[...]
        gates = jax.nn.sigmoid(gate_ref[0])
        for hh in range(N_HEADS):
            r = slice(hh * tq, (hh + 1) * tq)
            o = (gates[:, 3 * hh:3 * hh + 1] * ocmp_ref[hh] + gates[:, 3 * hh + 1:3 * hh + 2] * o_sel[r]
                 + gates[:, 3 * hh + 2:3 * hh + 3] * o_win[r])
            out_ref[0, :, hh * LANES:(hh + 1) * LANES] = o


def _nsa_sample(q, gates, pages, table, cmp, win, newkv, newwin):
    s, tq, _ = q.shape
    n_pages = table.shape[1]
    ncp = cmp.shape[1]
    wb = win.shape[2]
    assert (n_pages * PAGE_SIZE) // SEL_BLOCK == N_BLK_PAD and tq <= SEL_BLOCK and tq & (tq - 1) == 0
    assert n_pages % SEL_PAGES == 0
    rows = N_HEADS * tq
    per_seq = lambda r, w: pl.BlockSpec((1, r, w), lambda b, j, t: (b, 0, 0))
    page_spec = lambda k: pl.BlockSpec((None, 2, LANES, PAGE_SIZE),
                                       lambda b, j, t: (t[b, j * SEL_PAGES + k], 1, 0, 0))
    grid_spec = pltpu.PrefetchScalarGridSpec(
        num_scalar_prefetch=1,
        grid=(s, n_pages // SEL_PAGES),
        in_specs=[per_seq(tq, QPAD), per_seq(tq, LANES)] + [page_spec(k) for k in range(SEL_PAGES)] + [
            per_seq(ncp, 2 * LANES), per_seq(2 * LANES, wb),
            per_seq(2 * LANES, PAGE_SIZE), per_seq(2 * LANES, PAGE_SIZE),
            pl.BlockSpec((N_BLK_PAD, ncp), lambda b, j, t: (0, 0)),
            pl.BlockSpec((N_BLK_PAD, SEL_PAGES * PAGE_SIZE), lambda b, j, t: (0, j))],
        out_specs=per_seq(tq, QPAD),
        scratch_shapes=[
            pltpu.VMEM((rows, LANES), F32),
            pltpu.VMEM((rows, ncp), F32),
            pltpu.VMEM((N_KV, LANES, N_BLK_PAD), BF16),
            pltpu.VMEM((rows, LANES), F32),
            pltpu.VMEM((rows, LANES), F32),
            pltpu.VMEM((rows, LANES), F32),
            pltpu.VMEM((N_HEADS, tq, LANES), F32),
        ],
    )
    return pl.pallas_call(
        functools.partial(_nsa_sample_kernel, n_pages=n_pages, tq=tq),
        grid_spec=grid_spec,
        out_shape=jax.ShapeDtypeStruct((s, tq, QPAD), F32),
        compiler_params=_cparams("parallel", "arbitrary"),
        name="nsa_sample",
    )(table, q, gates, *([pages] * SEL_PAGES), cmp, win, newkv, newwin, _cover_t(ncp),
      _block_expand(n_pages * PAGE_SIZE))


S5_LANE_BLOCK = 2048


def _s5_kernel(u_ref, z_ref, h0_ref, bbd_ref, lam_ref, cbd_ref, d_ref, y_ref, hl_ref,
               bu_ref, hs_ref, h_ref):
    c = pl.program_id(1)
    steps = u_ref.shape[1]

    @pl.when(c == 0)
    def _():
        h_ref[...] = h0_ref[0]

    u = u_ref[0]
    ub = u.astype(BF16)
    halves = [(slice(k * SSM_WIDTH // 2, (k + 1) * SSM_WIDTH // 2),
               [slice(p * N_STATE + k * N_STATE // 2, p * N_STATE + (k + 1) * N_STATE // 2) for p in range(2)])
              for k in range(2)]
    for ch, parts in halves:
        for st in parts:
            bu_ref[:, st] = _dot(ub[:, ch], bbd_ref[ch, st])
    for blk in range(N_STATE // S5_LANE_BLOCK):
        re = slice(blk * S5_LANE_BLOCK, (blk + 1) * S5_LANE_BLOCK)
        im = slice(N_STATE + blk * S5_LANE_BLOCK, N_STATE + (blk + 1) * S5_LANE_BLOCK)
        lr = lam_ref[:, re]
        li = lam_ref[:, im]

        def step(t, carry, re=re, im=im, lr=lr, li=li):
            hr, hi = carry
            row = pl.ds(t, 1)
            nr = lr * hr - li * hi + bu_ref[row, re]
            ni = lr * hi + li * hr + bu_ref[row, im]
            hs_ref[row, re] = nr
            hs_ref[row, im] = ni
            return nr, ni

        hr, hi = lax.fori_loop(0, steps, step, (h_ref[:, re], h_ref[:, im]),
                               unroll=min(8, steps))
        h_ref[:, re] = hr
        h_ref[:, im] = hi
    for ch, parts in halves:
        y = sum(_dot(hs_ref[:, st].astype(BF16), cbd_ref[st, ch]) for st in parts)
        y = y + d_ref[:, ch] * u[:, ch]
        y_ref[0, :, ch] = (jax.nn.gelu(y) * jax.nn.sigmoid(z_ref[0, :, ch])).astype(y_ref.dtype)

    @pl.when(c == pl.num_programs(1) - 1)
    def _():
        hl_ref[0] = h_ref[...]


def _s5_weights(lam_re, lam_im, log_dt, b_re, b_im, c_re, c_im, d_skip):
    lam = lax.complex(lam_re.astype(F32), lam_im.astype(F32))
    dt = jnp.exp(log_dt.astype(F32))[:, None]
    lam_bar = jnp.exp(lam * dt)
    b_bar = ((lam_bar - 1.0) / lam)[..., None] * lax.complex(b_re.astype(F32), b_im.astype(F32))
    eye = jnp.eye(N_SSM_GROUPS, dtype=F32)
    def in_bd(b):
        return jnp.einsum("gph,gk->ghkp", b, eye).reshape(SSM_WIDTH, N_STATE)
    bbd = jnp.concatenate([in_bd(b_bar.real), in_bd(b_bar.imag)], axis=1)
    def out_bd(cm):
        return jnp.einsum("ghp,gk->gpkh", cm, eye).reshape(N_STATE, SSM_WIDTH)
    cbd = jnp.concatenate([out_bd(c_re.astype(F32)), -out_bd(c_im.astype(F32))], axis=0)
    lam_row = jnp.concatenate([lam_bar.real.reshape(1, N_STATE), lam_bar.imag.reshape(1, N_STATE)], axis=1)
    return bbd.astype(BF16), lam_row, cbd.astype(BF16), d_skip.astype(F32).reshape(1, SSM_WIDTH)


def _s5(u, z, h0, s5w, chunk, out_dtype):
    bbd, lam_row, cbd, d_row = s5w
    s, t, _ = u.shape
    const = lambda shape: pl.BlockSpec(shape, lambda b, c: (0,) * len(shape))
    return pl.pallas_call(
        _s5_kernel,
        grid=(s, t // chunk),
        in_specs=[
            pl.BlockSpec((1, chunk, SSM_WIDTH), lambda b, c: (b, c, 0)),
            pl.BlockSpec((1, chunk, SSM_WIDTH), lambda b, c: (b, c, 0)),
            pl.BlockSpec((1, 1, 2 * N_STATE), lambda b, c: (b, 0, 0)),
            const((SSM_WIDTH, 2 * N_STATE)), const((1, 2 * N_STATE)),
            const((2 * N_STATE, SSM_WIDTH)), const((1, SSM_WIDTH)),
        ],
        out_specs=[
            pl.BlockSpec((1, chunk, SSM_WIDTH), lambda b, c: (b, c, 0)),
            pl.BlockSpec((1, 1, 2 * N_STATE), lambda b, c: (b, 0, 0)),
        ],
        out_shape=[jax.ShapeDtypeStruct((s, t, SSM_WIDTH), out_dtype),
                   jax.ShapeDtypeStruct((s, 1, 2 * N_STATE), F32)],
        scratch_shapes=[pltpu.VMEM((chunk, 2 * N_STATE), F32),
                        pltpu.VMEM((chunk, 2 * N_STATE), F32),
                        pltpu.VMEM((1, 2 * N_STATE), F32)],
        compiler_params=_cparams("parallel", "arbitrary"),
        name="s5",
    )(u, z, h0, bbd, lam_row, cbd, d_row)


def _outproj_kernel(x_ref, a_ref, s_ref, wa_ref, ws_ref, y_ref):
    y = x_ref[...]
    y = y + _dot(a_ref[...].astype(BF16), wa_ref[...])
    y = y + _dot(s_ref[...].astype(BF16), ws_ref[...])
    y_ref[...] = y


def _outproj_weights(w_out):
    wa = w_out[:NSA_WIDTH].reshape(N_KV, HPG, 1, HEAD_DIM, D_MODEL)
    slot = jnp.eye(N_KV, dtype=F32).reshape(N_KV, 1, N_KV, 1, 1)
    return (wa * slot).reshape(QPAD, D_MODEL).astype(BF16), w_out[NSA_WIDTH:].astype(BF16)


def _outproj(x2d, a_out, s_out, wa, ws, tm):
    n = x2d.shape[0]
    row = lambda w: pl.BlockSpec((tm, w), lambda i: (i, 0))
    return pl.pallas_call(
        _outproj_kernel,
        grid=(n // tm,),
        in_specs=[row(D_MODEL), row(QPAD), row(SSM_WIDTH),
                  pl.BlockSpec((QPAD, D_MODEL), lambda i: (0, 0)),
                  pl.BlockSpec((SSM_WIDTH, D_MODEL), lambda i: (0, 0))],
        out_specs=row(D_MODEL),
        out_shape=jax.ShapeDtypeStruct((n, D_MODEL), F32),
        compiler_params=_cparams("parallel"),
        name="outproj",
    )(x2d, a_out, s_out, wa, ws)


PEER_HALF = PEER_HEADS * PEER_DK // 2
_CAND_COUNTS = tuple(PEER_TOPK // (a + 1) for a in range(PEER_TOPK))
N_CAND = sum(_CAND_COUNTS)
N_CAND_PAD = -(-N_CAND // 8) * 8


def _peer_route_kernel(y_ref, g_ref, wq_ref, k1_ref, k2_ref, xn_ref, c1_ref, e1_ref, r2_ref, e2_ref,
                       s1_ref, s2_ref, v1_ref, v2_ref, cand_ref, rank1_ref, rank2_ref, rankc_ref):
    xn = _rmsnorm(y_ref[...], g_ref[...]).astype(BF16)
    xn_ref[...] = xn
    q = _dot(xn, wq_ref[...]).astype(BF16)
    s1_ref[...] = _nt_dot(k1_ref[...], q[:, :PEER_HALF])
    s2_ref[...] = _nt_dot(k2_ref[...], q[:, PEER_HALF:])
    t = y_ref.shape[0]
    krows = lambda h: slice(h * N_KEYS, (h + 1) * N_KEYS)
    crows = lambda h: slice(h * N_CAND_PAD, (h + 1) * N_CAND_PAD)

    def key_ranks(exact):
        worst = jnp.zeros((1, t), F32)
        for h in range(PEER_HEADS):
            for s_ref, rank_ref, v_ref in ((s1_ref, rank1_ref, v1_ref), (s2_ref, rank2_ref, v2_ref)):
                rank = _take_top(s_ref[krows(h), :], PEER_TOPK, v_ref, h * PEER_TOPK, exact)
                rank_ref[krows(h), :] = rank
                worst = jnp.maximum(worst, _n_taken(rank, PEER_TOPK))
        return worst

    worst = key_ranks(False)

    @pl.when(jnp.max(worst) > float(PEER_TOPK))
    def _():
        key_ranks(True)

    for h in range(PEER_HEADS):
        v1 = v1_ref[h * PEER_TOPK:(h + 1) * PEER_TOPK, :]
        v2 = v2_ref[h * PEER_TOPK:(h + 1) * PEER_TOPK, :]
        off = h * N_CAND_PAD
        for a, nb in enumerate(_CAND_COUNTS):
            cand_ref[off:off + nb, :] = v1[a:a + 1, :] + v2[0:nb, :]
            off += nb
        cand_ref[off:(h + 1) * N_CAND_PAD, :] = jnp.full(((h + 1) * N_CAND_PAD - off, t), LOWEST, F32)

    def cand_ranks(exact):
        worst = jnp.zeros((1, t), F32)
        for h in range(PEER_HEADS):
            rank = _take_top(cand_ref[crows(h), :], PEER_TOPK, exact=exact)
            rankc_ref[crows(h), :] = rank
            worst = jnp.maximum(worst, _n_taken(rank, PEER_TOPK))
        return worst

    worst = cand_ranks(False)

    @pl.when(jnp.max(worst) > float(PEER_TOPK))
    def _():
        cand_ranks(True)

    for h in range(PEER_HEADS):
        rows = krows(h)
        top = slice(h * PEER_TOPK, h * PEER_TOPK + 1)
        cand = cand_ref[crows(h), :]
        taken = jnp.where(rankc_ref[crows(h), :] < float(PEER_TOPK), 1.0, 0.0)
        z = jnp.sum(taken * jnp.exp(cand - cand[0:1, :]), axis=0, keepdims=True)
        rank1 = rank1_ref[rows, :]
        count = jnp.zeros((N_KEYS, t), F32)
        off = 0
        for a, nb in enumerate(_CAND_COUNTS):
            n_a = jnp.sum(taken[off:off + nb, :], axis=0, keepdims=True)
            count = jnp.where(rank1 == float(a), n_a, count)
            off += nb
        c1_ref[rows, :] = count
        e1_ref[rows, :] = jnp.exp(s1_ref[rows, :] - v1_ref[top, :]) / z
        r2_ref[rows, :] = rank2_ref[rows, :].astype(BF16)
        e2_ref[rows, :] = jnp.exp(s2_ref[rows, :] - v2_ref[top, :]).astype(BF16)


def _peer_weights(w_q, sub_k1, sub_k2):
    wq = w_q.reshape(D_MODEL, PEER_HEADS, 2, PEER_DK // 2).transpose(0, 2, 1, 3).reshape(D_MODEL, 2 * PEER_HALF)
    eye = jnp.eye(PEER_HEADS, dtype=F32)
    bd = lambda k: jnp.einsum("hkd,hj->hkjd", k, eye).reshape(PEER_HEADS * N_KEYS, PEER_HALF)
    return wq.astype(BF16), bd(sub_k1).astype(BF16), bd(sub_k2).astype(BF16)


def _peer_route(y2d, norm_g, wq, k1bd, k2bd, tm):
    n = y2d.shape[0]
    hk = PEER_HEADS * N_KEYS
    const = lambda a: pl.BlockSpec(a.shape, lambda i: (0, 0))
    col = lambda r: pl.BlockSpec((r, tm), lambda i: (0, i))
    tshape = lambda dt: jax.ShapeDtypeStruct((hk, n), dt)
    return pl.pallas_call(
        _peer_route_kernel,
        grid=(n // tm,),
        in_specs=[pl.BlockSpec((tm, D_MODEL), lambda i: (i, 0)),
                  pl.BlockSpec((1, D_MODEL), lambda i: (0, 0)), const(wq), const(k1bd), const(k2bd)],
        out_specs=[pl.BlockSpec((tm, D_MODEL), lambda i: (i, 0)), col(hk), col(hk), col(hk), col(hk)],
        out_shape=[jax.ShapeDtypeStruct((n, D_MODEL), BF16),
                   tshape(F32),
                   tshape(F32),
                   tshape(BF16),
                   tshape(BF16)],
        scratch_shapes=[pltpu.VMEM((hk, tm), F32), pltpu.VMEM((hk, tm), F32),
                        pltpu.VMEM((PEER_HEADS * PEER_TOPK, tm), F32),
                        pltpu.VMEM((PEER_HEADS * PEER_TOPK, tm), F32),
                        pltpu.VMEM((PEER_HEADS * N_CAND_PAD, tm), F32),
                        pltpu.VMEM((hk, tm), F32), pltpu.VMEM((hk, tm), F32),
                        pltpu.VMEM((PEER_HEADS * N_CAND_PAD, tm), F32)],
        compiler_params=_cparams("parallel"),
        name="peer_route",
    )(y2d, norm_g.reshape(1, D_MODEL), wq, k1bd, k2bd)


EXPERT_BLOCK = 1024
I1_PER_BLOCK = EXPERT_BLOCK // N_KEYS
N_EXPERT_BLOCKS = N_EXPERTS // EXPERT_BLOCK


def _peer_dense_kernel(y_ref, xn_ref, u_ref, vt_ref, c1_ref, e1_ref, r2_ref, e2_ref,
                       gf_ref, out_ref, acc_ref, act_ref, ga_ref):
    j = pl.program_id(1)
    t = xn_ref.shape[0]

    @pl.when(j == 0)
    def _():
        acc_ref[...] = jnp.zeros(acc_ref.shape, F32)
        act_ref[...] = jnp.zeros(act_ref.shape, BF16)
        ga_ref[...] = jnp.zeros(ga_ref.shape, BF16)

    def gate_stage(prev):
        jb = j - 1
        for ii in range(I1_PER_BLOCK):
            erows = slice(ii * N_KEYS, (ii + 1) * N_KEYS)
            for tc in range(t // LANES):
                cols = slice(tc * LANES, (tc + 1) * LANES)
                gate = jnp.zeros((N_KEYS, LANES), BF16)
                for h in range(PEER_HEADS):
                    grp = pl.ds(pl.multiple_of(h * N_KEYS + jb * I1_PER_BLOCK, I1_PER_BLOCK), I1_PER_BLOCK)
                    krows = slice(h * N_KEYS, (h + 1) * N_KEYS)
                    count = c1_ref[grp, cols][ii:ii + 1, :].astype(BF16)
                    e1 = e1_ref[grp, cols][ii:ii + 1, :].astype(BF16)
                    gate = gate + jnp.where(r2_ref[krows, cols] < count, e2_ref[krows, cols] * e1,
                                            jnp.zeros((), BF16))
                ga_ref[prev, erows, cols] = gate * act_ref[prev, erows, cols]

    def stages(cur, prev):
        pl.when((j >= 1) & (j <= N_EXPERT_BLOCKS))(functools.partial(gate_stage, prev))

        acc_ref[...] += _dot(vt_ref[...], ga_ref[cur])

        act_ref[cur] = _gelu_tanh(_nt_dot(u_ref[...], xn_ref[...])).astype(BF16)

    for parity in range(2):
        pl.when(j % 2 == parity)(functools.partial(stages, parity, 1 - parity))

    @pl.when(j == pl.num_programs(1) - 1)
    def _():
        y = y_ref[...] + acc_ref[...].T
        out_ref[...] = _rmsnorm(y, gf_ref[...])


def _peer_dense(y2d, xn, u_bf, vt_bf, c1, e1, r2, e2, norm_f, tm):
    n = y2d.shape[0]
    hk = PEER_HEADS * N_KEYS
    tok = lambda w: pl.BlockSpec((tm, w), lambda i, j: (i, 0))
    col = lambda r: pl.BlockSpec((r, tm), lambda i, j: (0, i))
    last = N_EXPERT_BLOCKS - 1
    return pl.pallas_call(
        _peer_dense_kernel,
        grid=(n // tm, N_EXPERT_BLOCKS + 2),
        in_specs=[tok(D_MODEL), tok(D_MODEL),
                  pl.BlockSpec((EXPERT_BLOCK, D_MODEL), lambda i, j: (jnp.minimum(j, last), 0)),
                  pl.BlockSpec((D_MODEL, EXPERT_BLOCK), lambda i, j: (0, jnp.maximum(j - 2, 0))),
                  col(hk), col(hk), col(hk), col(hk),
                  pl.BlockSpec((1, D_MODEL), lambda i, j: (0, 0))],
        out_specs=tok(D_MODEL),
        out_shape=jax.ShapeDtypeStruct((n, D_MODEL), F32),
        scratch_shapes=[pltpu.VMEM((D_MODEL, tm), F32),
                        pltpu.VMEM((2, EXPERT_BLOCK, tm), BF16),
                        pltpu.VMEM((2, EXPERT_BLOCK, tm), BF16)],
        compiler_params=_cparams("parallel", "arbitrary"),
        name="peer_dense",
    )(y2d, xn, u_bf, vt_bf, c1, e1, r2, e2, norm_f.reshape(1, D_MODEL))


PROJ_TILE = 512
PEER_TILE = 256
PEER_DENSE_TILE = 512
S5_CHUNK = 256


def _state_to_rows(st):
    s = st.shape[0]
    return jnp.concatenate([st[..., 0].reshape(s, 1, N_STATE), st[..., 1].reshape(s, 1, N_STATE)], axis=-1)


def _rows_to_state(h):
    s = h.shape[0]
    shape = (s, N_SSM_GROUPS, SSM_STATE)
    return jnp.stack([h[:, 0, :N_STATE].reshape(shape), h[:, 0, N_STATE:].reshape(shape)], axis=-1)


def _feat_to_tokens(a, n_types):
    s, _, t = a.shape
    return a.reshape(s, n_types, N_KV, HEAD_DIM, t).transpose(0, 4, 1, 2, 3)


def _peer_block(y2d, norm_g, peer_w, norm_f, tm):
    wq, k1bd, k2bd, u_bf, vt_bf = peer_w
    xn, c1, e1, r2, e2 = _peer_route(y2d, norm_g, wq, k1bd, k2bd, tm)
    return _peer_dense(y2d, xn, u_bf, vt_bf, c1, e1, r2, e2, norm_f, min(PEER_DENSE_TILE, y2d.shape[0]))


def kernel(x_prompt, x_sample, cache_kv, cache_win, state_ssm, page_table, norm_mix, w_in, w_cmp1, w_cmp2, pe_cmp, lam_re, lam_im, log_dt, b_re, b_im, c_re, c_im, d_skip, w_out, norm_ffn, w_q_peer, sub_k1, sub_k2, u_tab, v_tab, norm_final):
    b, t, d = x_prompt.shape
    db, ts, _ = x_sample.shape
    assert w_in.shape[0] == DEPTH == 1 and d == D_MODEL
    l = 0
    n_pool = cache_kv.shape[1]
    wb = cache_win.shape[2]

    w_tok, w_feat = _proj_weights(w_in[l])
    cmp_w = _compress_weights(w_cmp1[l], w_cmp2[l], pe_cmp[l])
    s5_w = _s5_weights(lam_re[l], lam_im[l], log_dt[l], b_re[l], b_im[l], c_re[l], c_im[l], d_skip[l])
    wa, ws = _outproj_weights(w_out[l])
    peer_w = _peer_weights(w_q_peer[l], sub_k1[l], sub_k2[l]) + (
        u_tab[l].astype(BF16), v_tab[l].T.astype(BF16))

    q, gate, u, z, kswk, q_t, kvsel_t, ksvs_t, kvwin_t, kwvw_t = _project(
        x_prompt, norm_mix[l], w_tok, w_feat, PROJ_TILE, BF16)
    n_pg = t // PAGE_SIZE
    table_p = jnp.broadcast_to(jnp.arange(n_pg, dtype=jnp.int32), (b, n_pg))
    prompt_page = lambda k: pl.BlockSpec((None, 2, LANES, PAGE_SIZE),
                                         lambda bb, j, tb: (bb, 0, 0, tb[bb, j * CMP_PAGES + k]))
    cmp_p = _compress(kvsel_t.reshape(b, 4, LANES, t), prompt_page, table_p, *cmp_w)
    a_p = _nsa_prompt_t(q.reshape(b, t, QPAD), q_t, gate.reshape(b, t, LANES),
                        kswk.reshape(b, t, 2 * KV_WIDTH), ksvs_t, kwvw_t, cmp_p)
    s_p, h_p = _s5(u.reshape(b, t, SSM_WIDTH), z.reshape(b, t, SSM_WIDTH),
                   jnp.zeros((b, 1, 2 * N_STATE), F32), s5_w, S5_CHUNK, BF16)
    y1p = _outproj(x_prompt.reshape(b * t, d), a_p.reshape(b * t, QPAD), s_p.reshape(b * t, SSM_WIDTH),
                   wa, ws, PROJ_TILE)
    y_prompt = _peer_block(y1p, norm_ffn[l], peer_w, norm_final, PEER_TILE).reshape(b, t, d)
    kv_prompt = _feat_to_tokens(kvsel_t, 4)
    win_prompt = _feat_to_tokens(kvwin_t[:, :, t - min(WINDOW, t):], 2)
    ssm_prompt = _rows_to_state(h_p)

    qs, gate_s, u_s, z_s, _, _, kvsel_st, ksvs_st, kvwin_st, kwvw_st = _project(
        x_sample.reshape(1, db * ts, d), norm_mix[l], w_tok, w_feat, db * ts, F32)
    pages_s = cache_kv[l].transpose(0, 2, 3, 4, 1).reshape(n_pool, 4, LANES, PAGE_SIZE)
    sample_page = lambda k: pl.BlockSpec((None, 2, LANES, PAGE_SIZE),
                                         lambda bb, j, tb: (tb[bb, j * CMP_PAGES + k], 0, 0, 0))
    cmp_s = _compress(pages_s, sample_page, page_table, *cmp_w)
    new_page = lambda a: jnp.pad(a[0].reshape(2 * LANES, db, ts).transpose(1, 0, 2),
                                 ((0, 0), (0, 0), (0, PAGE_SIZE - ts)))
    win_t = cache_win[l].transpose(0, 2, 3, 4, 1).reshape(db, 2 * LANES, wb)
    a_s = _nsa_sample(qs.reshape(db, ts, QPAD), gate_s.reshape(db, ts, LANES), pages_s, page_table, cmp_s,
                      win_t, new_page(ksvs_st), new_page(kwvw_st))
    s_s, h_s = _s5(u_s.reshape(db, ts, SSM_WIDTH), z_s.reshape(db, ts, SSM_WIDTH),
                   _state_to_rows(state_ssm[l].astype(F32)), s5_w, ts, F32)
    y1s = _outproj(x_sample.reshape(db * ts, d), a_s.reshape(db * ts, QPAD), s_s.reshape(db * ts, SSM_WIDTH),
                   wa, ws, db * ts)
    y_sample = _peer_block(y1s, norm_ffn[l], peer_w, norm_final, PEER_TILE).reshape(db, ts, d)
    per_tok = lambda a, n_types: a[0].reshape(n_types, N_KV, HEAD_DIM, db, ts).transpose(3, 4, 0, 1, 2)
    kv_sample = per_tok(kvsel_st, 4)
    win_new = per_tok(kvwin_st, 2).astype(cache_win.dtype)
    win_sample = jnp.concatenate([cache_win[l], win_new], axis=1)[:, ts:]
    ssm_sample = _rows_to_state(h_s)

    return (y_prompt, y_sample, kv_prompt[None], kv_sample[None], win_prompt[None], win_sample[None],
            ssm_prompt[None], ssm_sample[None])
```

```python
import functools
import math

import jax
import jax.numpy as jnp
from jax import lax
from jax.experimental import pallas as pl
from jax.experimental.pallas import tpu as pltpu

D_MODEL = 1024
DEPTH = 1
PAGE_SIZE = 128
N_HEADS = 8
N_KV = 2
HEAD_DIM = 64
HPG = N_HEADS // N_KV
CMP_LEN = 32
CMP_STRIDE = 16
SEL_BLOCK = 64
N_SEL = 16
WINDOW = 512
SSM_WIDTH = 512
SSM_GROUP = 16
N_SSM_GROUPS = SSM_WIDTH // SSM_GROUP
SSM_STATE = 64
N_KEYS = 128
N_EXPERTS = N_KEYS * N_KEYS
PEER_HEADS = 8
PEER_DK = 128
PEER_TOPK = 16
NSA_WIDTH = N_HEADS * HEAD_DIM
KV_WIDTH = N_KV * HEAD_DIM
GATE_WIDTH = 3 * N_HEADS
EPS = 1e-6
NEG = -1e30
FORCE = 1e4
LOG2_E = 1.0 / math.log(2.0)
LOWEST = -3.0e38

LANES = 128
BF16_ROWS = 16
VMEM_LIMIT = 56 * 1024 * 1024

N_STATE = N_SSM_GROUPS * SSM_STATE
QPAD = N_HEADS * LANES
F32 = jnp.float32
BF16 = jnp.bfloat16

assert KV_WIDTH == LANES and PAGE_SIZE == LANES


def _cparams(*sem):
    return pltpu.CompilerParams(dimension_semantics=sem, vmem_limit_bytes=VMEM_LIMIT)


def _nt_dot(a, b):
    return lax.dot_general(a, b, (((1,), (1,)), ((), ())), preferred_element_type=F32)


def _dot(a, b):
    return jnp.dot(a, b, preferred_element_type=F32)


def _rmsnorm(x, g):
    return x * lax.rsqrt(jnp.mean(x * x, axis=-1, keepdims=True) + EPS) * g


_GELU_C1 = 2.0 * math.sqrt(2.0 / math.pi) * LOG2_E
_GELU_C3 = _GELU_C1 * 0.044715


def _gelu_tanh(x):
    return x / (1.0 + jnp.exp2(x * (-_GELU_C1 - _GELU_C3 * (x * x))))


_PROJ_COLS = (("q", QPAD), ("gate", LANES), ("u", SSM_WIDTH), ("z", SSM_WIDTH), ("kswk", 2 * KV_WIDTH))
KVT_ROWS = 6 * KV_WIDTH


def _proj_kernel(x_ref, g_ref, w_ref, wt_ref, q_ref, gate_ref, u_ref, z_ref, kswk_ref,
                 qt_ref, kvsel_ref, ksvs_ref, kvwin_ref, kwvw_ref):
    xn = _rmsnorm(x_ref[...], g_ref[...]).astype(BF16)
    off = 0
    for (name, width), ref in zip(_PROJ_COLS, (q_ref, gate_ref, u_ref, z_ref, kswk_ref)):
        ref[...] = _dot(xn, w_ref[:, off:off + width]).astype(ref.dtype)
        off += width
    qt_ref[0] = _nt_dot(wt_ref[0:QPAD, :], xn).astype(BF16)
    kvt = _nt_dot(wt_ref[QPAD:, :], xn)
    kvsel_ref[0] = kvt[0:4 * KV_WIDTH]
    ksvs_ref[0] = kvt[2 * KV_WIDTH:4 * KV_WIDTH].astype(BF16)
    kvwin_ref[0] = kvt[4 * KV_WIDTH:]
    kwvw_ref[0] = kvt[4 * KV_WIDTH:].astype(BF16)


def _proj_weights(w_in):
    c0 = NSA_WIDTH
    c1 = c0 + 6 * KV_WIDTH
    c2 = c1 + GATE_WIDTH
    wq = w_in[:, :c0].reshape(D_MODEL, N_KV, HPG, 1, HEAD_DIM) * (HEAD_DIM ** -0.5 * LOG2_E)
    slot = jnp.eye(N_KV, dtype=F32).reshape(1, N_KV, 1, N_KV, 1)
    wq_pad = (wq * slot).reshape(D_MODEL, QPAD)
    wg = jnp.pad(w_in[:, c1:c2], ((0, 0), (0, LANES - GATE_WIDTH)))
    ks = w_in[:, c0 + 2 * KV_WIDTH:c0 + 3 * KV_WIDTH]
    kw = w_in[:, c0 + 4 * KV_WIDTH:c0 + 5 * KV_WIDTH]
    w = jnp.concatenate([wq_pad, wg, w_in[:, c2:], ks, kw], axis=1)
    wt = jnp.concatenate([wq_pad, w_in[:, c0:c1]], axis=1).T
    return w.astype(BF16), wt.astype(BF16)


def _project(x3d, norm_g, w_tok, w_feat, tm, q_dtype):
    s, t, _ = x3d.shape
    nt = t // tm
    n = s * t
    row = lambda w: pl.BlockSpec((tm, w), lambda b, i: (b * nt + i, 0))
    feat = lambda r: pl.BlockSpec((1, r, tm), lambda b, i: (b, 0, i))
    const = lambda a: pl.BlockSpec(a.shape, lambda b, i: (0, 0))
    shapes = [
        jax.ShapeDtypeStruct((n, QPAD), q_dtype),
        jax.ShapeDtypeStruct((n, LANES), F32),
        jax.ShapeDtypeStruct((n, SSM_WIDTH), F32),
        jax.ShapeDtypeStruct((n, SSM_WIDTH), F32),
        jax.ShapeDtypeStruct((n, 2 * KV_WIDTH), BF16),
        jax.ShapeDtypeStruct((s, QPAD, t), BF16),
        jax.ShapeDtypeStruct((s, 4 * KV_WIDTH, t), F32),
        jax.ShapeDtypeStruct((s, 2 * KV_WIDTH, t), BF16),
        jax.ShapeDtypeStruct((s, 2 * KV_WIDTH, t), F32),
        jax.ShapeDtypeStruct((s, 2 * KV_WIDTH, t), BF16),
    ]
    g2 = norm_g.reshape(1, D_MODEL)
    return pl.pallas_call(
        _proj_kernel,
        grid=(s, nt),
        in_specs=[row(D_MODEL), const(g2), const(w_tok), const(w_feat)],
        out_specs=[row(sh.shape[1]) for sh in shapes[:5]] + [feat(sh.shape[1]) for sh in shapes[5:]],
        out_shape=shapes,
        compiler_params=_cparams("parallel", "parallel"),
        name="proj",
    )(x3d.reshape(n, D_MODEL), g2, w_tok, w_feat)


CMP_PAGES = 16
CMP_PAIRS = CMP_STRIDE // 2


def _compress_kernel(tbl_ref, *refs, n_pages):
    del tbl_ref
    page_refs = refs[:CMP_PAGES]
    w1_ref, pe_ref, w2_ref, out_ref, slabk_ref, slabv_ref = refs[CMP_PAGES:]
    j = pl.program_id(1)
    seq = n_pages * PAGE_SIZE
    slabs = (slabk_ref, slabv_ref)

    for k, page in enumerate(page_refs):
        rows = pl.ds(pl.multiple_of((j * CMP_PAGES + k) * PAGE_SIZE, PAGE_SIZE), PAGE_SIZE)
        for c, slab in enumerate(slabs):
            slab[rows, :] = page[c].T

    @pl.when(j == n_pages // CMP_PAGES - 1)
    def _():
        nb = seq // CMP_STRIDE
        for c, slab in enumerate(slabs):
            head = jnp.zeros((nb, LANES), F32)
            tail = jnp.zeros((nb, LANES), F32)
            for m in range(CMP_PAIRS):
                x = jnp.concatenate([slab[pl.ds(m, nb, stride=CMP_STRIDE), :],
                                     slab[pl.ds(m + CMP_PAIRS, nb, stride=CMP_STRIDE), :]], axis=1)
                head = head + _dot((x + pe_ref[c, m:m + 1, :]).astype(BF16), w1_ref[c, m])
                m2 = CMP_PAIRS + m
                tail = tail + _dot((x + pe_ref[c, m2:m2 + 1, :]).astype(BF16), w1_ref[c, m2])
            hid = head + pltpu.roll(tail, nb - 1, 0)
            out = _dot(jax.nn.gelu(hid).astype(BF16), w2_ref[c])
            out_ref[0, :, c * LANES:(c + 1) * LANES] = out.astype(BF16)


def _compress_weights(w1, w2, pe):
    eye = jnp.eye(N_KV, dtype=F32)
    bd = lambda m: jnp.einsum("...de,gk->...gdke", m, eye).reshape(m.shape[:-2] + (LANES, LANES))
    pair = lambda a: a.reshape(2, 2, 2, CMP_PAIRS, *a.shape[2:]).swapaxes(2, 3)
    w1p = pair(bd(w1)).reshape(2, 2 * CMP_PAIRS, 2 * LANES, LANES)
    pe2 = jnp.concatenate([pe, pe], axis=-1)
    pep = pair(pe2).reshape(2, 2 * CMP_PAIRS, 2 * LANES)
    return w1p.astype(BF16), pep, bd(w2).astype(BF16)


def _compress(pages, page_spec, table, w1bd, pe2, w2bd):
    s, p = table.shape
    assert CMP_LEN == 2 * CMP_STRIDE and p % CMP_PAGES == 0
    seq = p * PAGE_SIZE
    const = lambda a: pl.BlockSpec(a.shape, lambda b, j, t: (0,) * a.ndim)
    grid_spec = pltpu.PrefetchScalarGridSpec(
        num_scalar_prefetch=1,
        grid=(s, p // CMP_PAGES),
        in_specs=[page_spec(k) for k in range(CMP_PAGES)] + [const(w1bd), const(pe2), const(w2bd)],
        out_specs=pl.BlockSpec((1, seq // CMP_STRIDE, 2 * LANES), lambda b, j, t: (b, 0, 0)),
        scratch_shapes=[pltpu.VMEM((seq, LANES), F32), pltpu.VMEM((seq, LANES), F32)],
    )
    return pl.pallas_call(
        functools.partial(_compress_kernel, n_pages=p),
        grid_spec=grid_spec,
        out_shape=jax.ShapeDtypeStruct((s, seq // CMP_STRIDE, 2 * LANES), BF16),
        compiler_params=_cparams("parallel", "arbitrary"),
        name="compress",
    )(table, *([pages] * CMP_PAGES), w1bd, pe2, w2bd)


Q_TILE = 128
KEY_CHUNK = 1024
N_BLK_PAD = 128
LOG2_SEL_BLOCK = int(math.log2(SEL_BLOCK))
WIN_KEYS = WINDOW + Q_TILE


def _cover_t(n_cmp_pad):
    n = jnp.arange(n_cmp_pad, dtype=jnp.int32)[None, :] * CMP_STRIDE
    s = jnp.arange(N_BLK_PAD, dtype=jnp.int32)[:, None] * SEL_BLOCK
    return ((n < s + SEL_BLOCK) & (n + CMP_LEN > s)).astype(BF16)


def _split3_nt(w, x):
    hi = x.astype(BF16)
    r1 = x - hi.astype(F32)
    mid = r1.astype(BF16)
    lo = (r1 - mid.astype(F32)).astype(BF16)
    return _nt_dot(w, hi) + _nt_dot(w, mid) + _nt_dot(w, lo)


def _take_top(s, n, out_ref=None, out_row=0, exact=True):
    kidx = lax.broadcasted_iota(jnp.int32, s.shape, 0).astype(F32)
    rem = s
    rank = jnp.full(s.shape, float(n), F32)
    for a in range(n):
        mx = jnp.max(rem, axis=0, keepdims=True)
        if exact:
            first = jnp.min(jnp.where(rem == mx, kidx, float(s.shape[0])), axis=0, keepdims=True)
            taken = kidx == first
        else:
            taken = rem == mx
        if out_ref is not None:
            out_ref[out_row + a:out_row + a + 1, :] = mx
        rank = jnp.where(taken, float(a), rank)
        rem = jnp.where(taken, LOWEST, rem)
    return rank


def _n_taken(rank, n):
    return jnp.sum(jnp.where(rank < float(n), 1.0, 0.0), axis=0, keepdims=True)


def _topk_block_mask(imp_t, n_keep):
    return jnp.where(_take_top(imp_t, n_keep) < float(n_keep), 0.0, NEG)


def _cmp_branch(q, s_ref, cmp_ref, ocmp_ref, t_pos, rows):
    ncp = cmp_ref.shape[1]
    kc = cmp_ref[0, :, 0:LANES]
    vc = cmp_ref[0, :, LANES:2 * LANES]
    s_ref[:, 0:ncp] = _nt_dot(q, kc)
    n_idx = lax.broadcasted_iota(jnp.int32, (rows, ncp), 1)
    ok = (n_idx * CMP_STRIDE + (CMP_LEN - 1) <= t_pos) & (n_idx < ncp - 1)
    imps = []
    for g in range(N_KV):
        psum = jnp.zeros((rows, ncp), F32)
        for h in range(HPG):
            hh = g * HPG + h
            s = jnp.where(ok, s_ref[hh * rows:(hh + 1) * rows, 0:ncp], NEG)
            m = jnp.max(s, axis=-1, keepdims=True)
            e = jnp.where(ok, jnp.exp2(s - m), 0.0)
            l = jnp.sum(e, axis=-1, keepdims=True)
            p = e * jnp.where(l > 0.0, 1.0 / l, 0.0)
            ocmp_ref[hh] = _dot(p.astype(BF16), vc)
            psum = psum + p
        imps.append(psum)
    return imps


def _with_ones_rows(vt, g):
    row = lax.broadcasted_iota(jnp.int32, vt.shape, 0)
    own = (row >= g * HEAD_DIM) & (row < (g + 1) * HEAD_DIM)
    return jnp.where(own, vt, jnp.ones((), vt.dtype))


def _block_expand(t):
    key_blk = jnp.arange(t, dtype=jnp.int32)[None, :] // SEL_BLOCK
    return (jnp.arange(N_BLK_PAD, dtype=jnp.int32)[:, None] == key_blk).astype(BF16)


def _nsa_prompt_t_kernel(q_ref, qt_ref, gate_ref, ktok_ref, vs_ref, vw_ref, cmp_ref, covt_ref, expt_ref,
                         out_ref, qs_ref, qst_ref, sc_ref, s_ref, p_ref, m_ref, acc_ref, ocmp_ref):
    i = pl.program_id(1)
    t0 = i * Q_TILE
    group_cols = HPG * Q_TILE
    for h in range(N_HEADS):
        cols = slice(h * Q_TILE, (h + 1) * Q_TILE)
        qs_ref[cols, :] = q_ref[0, :, h * LANES:(h + 1) * LANES]
        qst_ref[0:LANES, cols] = qt_ref[0, h * LANES:(h + 1) * LANES, :]
    t_col = t0 + lax.broadcasted_iota(jnp.int32, (Q_TILE, 1), 0)

    psums = _cmp_branch(qs_ref[...], sc_ref, cmp_ref, ocmp_ref, t_col, Q_TILE)
    s_idx = lax.broadcasted_iota(jnp.int32, (N_BLK_PAD, Q_TILE), 0)
    t_row = t0 + lax.broadcasted_iota(jnp.int32, (N_BLK_PAD, Q_TILE), 1)
    cur = jnp.right_shift(t_row, LOG2_SEL_BLOCK)
    forced = (s_idx == 0) | (s_idx == cur) | (s_idx == cur - 1)
    for g in range(N_KV):
        imp_t = _split3_nt(covt_ref[...], psums[g])
        imp_t = jnp.where(forced, FORCE, imp_t)
        imp_t = jnp.where(s_idx * SEL_BLOCK <= t_row, imp_t, NEG)
        mask_t = _topk_block_mask(imp_t, N_SEL).astype(BF16)
        for h in range(HPG):
            hh = g * HPG + h
            qst_ref[LANES:2 * LANES, hh * Q_TILE:(hh + 1) * Q_TILE] = mask_t

    def attend(n_keys, v_aug, bias_t):
        for g in range(N_KV):
            alphas = []
            for h in range(HPG):
                hh = g * HPG + h
                s = s_ref[0:n_keys, hh * Q_TILE:(hh + 1) * Q_TILE]
                if bias_t is not None:
                    s = s + bias_t
                m_old = m_ref[hh:hh + 1, :]
                m_new = jnp.maximum(m_old, jnp.max(s, axis=0, keepdims=True))
                p_ref[0:n_keys, h * Q_TILE:(h + 1) * Q_TILE] = jnp.exp2(s - m_new).astype(BF16)
                alphas.append(jnp.exp2(m_old - m_new))
                m_ref[hh:hh + 1, :] = m_new
            alpha = jnp.concatenate(alphas, axis=1)
            acc_ref[g] = alpha * acc_ref[g] + _dot(v_aug[g], p_ref[0:n_keys, :])

    def reset():
        m_ref[...] = jnp.full(m_ref.shape, NEG, F32)
        acc_ref[...] = jnp.zeros(acc_ref.shape, F32)

    def outputs():
        outs = []
        for g in range(N_KV):
            lsum = (1 - g) * HEAD_DIM
            for h in range(HPG):
                a = acc_ref[g, :, h * Q_TILE:(h + 1) * Q_TILE]
                outs.append(a / a[lsum:lsum + 1, :])
        return outs

    reset()

    def chunk(c, diagonal, n_keys=KEY_CHUNK):
        k0 = pl.multiple_of(c * KEY_CHUNK, KEY_CHUNK)
        keys = pl.ds(k0, n_keys)
        lhs = jnp.concatenate([ktok_ref[0, keys, 0:LANES], expt_ref[keys, :]], axis=1)
        s_ref[0:n_keys, :] = _dot(lhs, qst_ref[...])
        vt = vs_ref[0, LANES:2 * LANES, keys]
        causal = None
        if diagonal:
            key_pos = k0 + lax.broadcasted_iota(jnp.int32, (n_keys, Q_TILE), 0)
            causal = jnp.where(key_pos <= t0 + lax.broadcasted_iota(jnp.int32, (n_keys, Q_TILE), 1), 0.0, NEG)
        attend(n_keys, [_with_ones_rows(vt, g) for g in range(N_KV)], causal)

    n_full = t0 // KEY_CHUNK

    def full_chunk(c, carry):
        chunk(c, False)
        return carry

    lax.fori_loop(0, n_full, full_chunk, 0)
    first_half = t0 - n_full * KEY_CHUNK < KEY_CHUNK // 2
    pl.when(first_half)(functools.partial(chunk, n_full, True, KEY_CHUNK // 2))
    pl.when(jnp.logical_not(first_half))(functools.partial(chunk, n_full, True))
    o_sel = outputs()

    reset()
    start = pl.multiple_of(jnp.maximum(t0 - WINDOW, 0), Q_TILE)
    wkeys = pl.ds(start, WIN_KEYS)
    s_ref[0:WIN_KEYS, :] = _dot(ktok_ref[0, wkeys, LANES:2 * LANES], qst_ref[0:LANES, :])
    vwt = vw_ref[0, LANES:2 * LANES, wkeys]
    dist = (t0 + lax.broadcasted_iota(jnp.int32, (WIN_KEYS, Q_TILE), 1)
            - (start + lax.broadcasted_iota(jnp.int32, (WIN_KEYS, Q_TILE), 0)))
    bias_w = jnp.where((dist >= 0) & (dist <= WINDOW), 0.0, NEG)
    attend(WIN_KEYS, [_with_ones_rows(vwt, g) for g in range(N_KV)], bias_w)
    o_win = outputs()

    gates = jax.nn.sigmoid(gate_ref[0])
    gates_t = gates.T
    for hh in range(N_HEADS):
        o_t = gates_t[3 * hh + 1:3 * hh + 2, :] * o_sel[hh] + gates_t[3 * hh + 2:3 * hh + 3, :] * o_win[hh]
        o = gates[:, 3 * hh:3 * hh + 1] * ocmp_ref[hh] + o_t.T
        out_ref[0, :, hh * LANES:(hh + 1) * LANES] = o.astype(out_ref.dtype)


def _nsa_prompt_t(q, qt, gates, kswk, ksvs_t, kwvw_t, cmp):
    b, t, _ = q.shape
    ncp = cmp.shape[1]
    tile = lambda w: pl.BlockSpec((1, Q_TILE, w), lambda bb, i: (bb, i, 0))
    whole = lambda r, w: pl.BlockSpec((1, r, w), lambda bb, i: (bb, 0, 0))
    const = lambda r, w: pl.BlockSpec((r, w), lambda bb, i: (0, 0))
    rows = N_HEADS * Q_TILE
    expand_t = _block_expand(t).T
    return pl.pallas_call(
        _nsa_prompt_t_kernel,
        grid=(b, t // Q_TILE),
        in_specs=[tile(QPAD), pl.BlockSpec((1, QPAD, Q_TILE), lambda bb, i: (bb, 0, i)), tile(LANES),
                  whole(t, 2 * LANES), whole(2 * LANES, t), whole(2 * LANES, t),
                  whole(ncp, 2 * LANES), const(N_BLK_PAD, ncp), const(t, N_BLK_PAD)],
        out_specs=tile(QPAD),
        out_shape=jax.ShapeDtypeStruct((b, t, QPAD), BF16),
        scratch_shapes=[
            pltpu.VMEM((rows, LANES), BF16),
            pltpu.VMEM((2 * LANES, rows), BF16),
            pltpu.VMEM((rows, ncp), F32),
            pltpu.VMEM((max(WIN_KEYS, KEY_CHUNK), rows), F32),
            pltpu.VMEM((max(WIN_KEYS, KEY_CHUNK), HPG * Q_TILE), BF16),
            pltpu.VMEM((N_HEADS, Q_TILE), F32),
            pltpu.VMEM((N_KV, LANES, HPG * Q_TILE), F32),
            pltpu.VMEM((N_HEADS, Q_TILE, LANES), F32),
        ],
        compiler_params=_cparams("parallel", "arbitrary"),
        name="nsa_prompt",
    )(q, qt, gates, kswk, ksvs_t, kwvw_t, cmp, _cover_t(ncp), expand_t)


SEL_PAGES = 32


def _flash_update(s, vt, m_ref, l_ref, acc_ref):
    m_old = m_ref[...]
    m_new = jnp.maximum(m_old, jnp.max(s, axis=-1, keepdims=True))
    alpha = jnp.exp2(m_old - m_new)
    p = jnp.exp2(s - m_new[:, 0:1])
    l_ref[...] = alpha * l_ref[...] + jnp.sum(p, axis=-1, keepdims=True)
    acc_ref[...] = alpha * acc_ref[...] + _nt_dot(p.astype(BF16), vt)
    m_ref[...] = m_new


def _nsa_sample_kernel(tbl_ref, *refs, n_pages, tq):
    del tbl_ref
    q_ref, gate_ref = refs[:2]
    page_refs = refs[2:2 + SEL_PAGES]
    (cmp_ref, win_ref, newkv_ref, newwin_ref, covt_ref, expand_ref, out_ref, qs_ref, s_ref, msel_ref,
     m_ref, l_ref, acc_ref, ocmp_ref) = refs[2 + SEL_PAGES:]
    j = pl.program_id(1)
    past_len = n_pages * PAGE_SIZE
    rows = N_HEADS * tq
    step_keys = SEL_PAGES * PAGE_SIZE
    i_col = lax.broadcasted_iota(jnp.int32, (rows, 1), 0) & (tq - 1)

    @pl.when(j == 0)
    def _():
        for h in range(N_HEADS):
            qs_ref[h * tq:(h + 1) * tq, :] = q_ref[0, :, h * LANES:(h + 1) * LANES]
        t_col = past_len + lax.broadcasted_iota(jnp.int32, (tq, 1), 0)
        psums = _cmp_branch(qs_ref[...].astype(BF16), s_ref, cmp_ref, ocmp_ref, t_col, tq)
        s_idx = lax.broadcasted_iota(jnp.int32, (N_BLK_PAD, LANES), 0)
        t_row = past_len + lax.broadcasted_iota(jnp.int32, (N_BLK_PAD, LANES), 1)
        cur = jnp.right_shift(t_row, LOG2_SEL_BLOCK)
        forced = (s_idx == 0) | (s_idx == cur) | (s_idx == cur - 1)
        ncp = cmp_ref.shape[1]
        for g in range(N_KV):
            psum = jnp.concatenate([psums[g], jnp.zeros((LANES - tq, ncp), F32)], axis=0)
            imp_t = jnp.where(forced, FORCE, _split3_nt(covt_ref[...], psum))
            mask_t = _topk_block_mask(imp_t, N_SEL - 1)
            msel_ref[g] = mask_t.T.astype(BF16)
        m_ref[...] = jnp.full(m_ref.shape, NEG, F32)
        l_ref[...] = jnp.zeros(l_ref.shape, F32)
        acc_ref[...] = jnp.zeros(acc_ref.shape, F32)

    q = qs_ref[...].astype(BF16)
    kt = jnp.concatenate([page[0].astype(BF16) for page in page_refs], axis=1)
    vt = jnp.concatenate([page[1].astype(BF16) for page in page_refs], axis=1)
    bias_g = [_dot(msel_ref[g, 0:BF16_ROWS, :], expand_ref[...])[0:tq, :] for g in range(N_KV)]
    bias = jnp.concatenate([bias_g[hh // HPG] for hh in range(N_HEADS)], axis=0)
    _flash_update(_dot(q, kt) + bias, vt, m_ref, l_ref, acc_ref)

    @pl.when(j == n_pages // SEL_PAGES - 1)
    def _():
        j_lane = lax.broadcasted_iota(jnp.int32, (rows, PAGE_SIZE), 1)
        new_bias = jnp.where(j_lane <= i_col, 0.0, NEG)
        _flash_update(_dot(q, newkv_ref[0, 0:LANES, :]) + new_bias, newkv_ref[0, LANES:2 * LANES, :],
                      m_ref, l_ref, acc_ref)
        wb = win_ref.shape[2]
        dist = wb + i_col - lax.broadcasted_iota(jnp.int32, (rows, wb), 1)
        s1 = _dot(q, win_ref[0, 0:LANES, :].astype(BF16)) + jnp.where((dist >= 0) & (dist <= WINDOW), 0.0, NEG)
        s2 = _dot(q, newwin_ref[0, 0:LANES, :]) + new_bias
        m = jnp.maximum(jnp.max(s1, axis=-1, keepdims=True), jnp.max(s2, axis=-1, keepdims=True))
        p1 = jnp.exp2(s1 - m)
        p2 = jnp.exp2(s2 - m)
        l = jnp.sum(p1, axis=-1, keepdims=True) + jnp.sum(p2, axis=-1, keepdims=True)
        o_win = (_nt_dot(p1.astype(BF16), win_ref[0, LANES:2 * LANES, :].astype(BF16))
                 + _nt_dot(p2.astype(BF16), newwin_ref[0, LANES:2 * LANES, :])) / l
        o_sel = acc_ref[...] / l_ref[...]
        gates = jax.nn.sigmoid(gate_ref[0])
        for hh in range(N_HEADS):
            r = slice(hh * tq, (hh + 1) * tq)
            o = (gates[:, 3 * hh:3 * hh + 1] * ocmp_ref[hh] + gates[:, 3 * hh + 1:3 * hh + 2] * o_sel[r]
                 + gates[:, 3 * hh + 2:3 * hh + 3] * o_win[r])
            out_ref[0, :, hh * LANES:(hh + 1) * LANES] = o


def _nsa_sample(q, gates, pages, table, cmp, win, newkv, newwin):
    s, tq, _ = q.shape
    n_pages = table.shape[1]
    ncp = cmp.shape[1]
    wb = win.shape[2]
    assert (n_pages * PAGE_SIZE) // SEL_BLOCK == N_BLK_PAD and tq <= SEL_BLOCK and tq & (tq - 1) == 0
    assert n_pages % SEL_PAGES == 0
    rows = N_HEADS * tq
    per_seq = lambda r, w: pl.BlockSpec((1, r, w), lambda b, j, t: (b, 0, 0))
    page_spec = lambda k: pl.BlockSpec((None, 2, LANES, PAGE_SIZE),
                                       lambda b, j, t: (t[b, j * SEL_PAGES + k], 1, 0, 0))
    grid_spec = pltpu.PrefetchScalarGridSpec(
        num_scalar_prefetch=1,
        grid=(s, n_pages // SEL_PAGES),
        in_specs=[per_seq(tq, QPAD), per_seq(tq, LANES)] + [page_spec(k) for k in range(SEL_PAGES)] + [
            per_seq(ncp, 2 * LANES), per_seq(2 * LANES, wb),
            per_seq(2 * LANES, PAGE_SIZE), per_seq(2 * LANES, PAGE_SIZE),
            pl.BlockSpec((N_BLK_PAD, ncp), lambda b, j, t: (0, 0)),
            pl.BlockSpec((N_BLK_PAD, SEL_PAGES * PAGE_SIZE), lambda b, j, t: (0, j))],
        out_specs=per_seq(tq, QPAD),
        scratch_shapes=[
            pltpu.VMEM((rows, LANES), F32),
            pltpu.VMEM((rows, ncp), F32),
            pltpu.VMEM((N_KV, LANES, N_BLK_PAD), BF16),
            pltpu.VMEM((rows, LANES), F32),
            pltpu.VMEM((rows, LANES), F32),
            pltpu.VMEM((rows, LANES), F32),
            pltpu.VMEM((N_HEADS, tq, LANES), F32),
        ],
    )
    return pl.pallas_call(
        functools.partial(_nsa_sample_kernel, n_pages=n_pages, tq=tq),
        grid_spec=grid_spec,
        out_shape=jax.ShapeDtypeStruct((s, tq, QPAD), F32),
        compiler_params=_cparams("parallel", "arbitrary"),
        name="nsa_sample",
    )(table, q, gates, *([pages] * SEL_PAGES), cmp, win, newkv, newwin, _cover_t(ncp),
      _block_expand(n_pages * PAGE_SIZE))


S5_LANE_BLOCK = 2048


def _s5_kernel(u_ref, z_ref, h0_ref, bbd_ref, lam_ref, cbd_ref, d_ref, y_ref, hl_ref,
               bu_ref, hs_ref, h_ref):
    c = pl.program_id(1)
    steps = u_ref.shape[1]

    @pl.when(c == 0)
    def _():
        h_ref[...] = h0_ref[0]

    u = u_ref[0]
    ub = u.astype(BF16)
    halves = [(slice(k * SSM_WIDTH // 2, (k + 1) * SSM_WIDTH // 2),
               [slice(p * N_STATE + k * N_STATE // 2, p * N_STATE + (k + 1) * N_STATE // 2) for p in range(2)])
              for k in range(2)]
    for ch, parts in halves:
        for st in parts:
            bu_ref[:, st] = _dot(ub[:, ch], bbd_ref[ch, st])
    for blk in range(N_STATE // S5_LANE_BLOCK):
        re = slice(blk * S5_LANE_BLOCK, (blk + 1) * S5_LANE_BLOCK)
        im = slice(N_STATE + blk * S5_LANE_BLOCK, N_STATE + (blk + 1) * S5_LANE_BLOCK)
        lr = lam_ref[:, re]
        li = lam_ref[:, im]

        def step(t, carry, re=re, im=im, lr=lr, li=li):
            hr, hi = carry
            row = pl.ds(t, 1)
            nr = lr * hr - li * hi + bu_ref[row, re]
            ni = lr * hi + li * hr + bu_ref[row, im]
            hs_ref[row, re] = nr
            hs_ref[row, im] = ni
            return nr, ni

        hr, hi = lax.fori_loop(0, steps, step, (h_ref[:, re], h_ref[:, im]),
                               unroll=min(8, steps))
        h_ref[:, re] = hr
        h_ref[:, im] = hi
    for ch, parts in halves:
        y = sum(_dot(hs_ref[:, st].astype(BF16), cbd_ref[st, ch]) for st in parts)
        y = y + d_ref[:, ch] * u[:, ch]
        y_ref[0, :, ch] = (jax.nn.gelu(y) * jax.nn.sigmoid(z_ref[0, :, ch])).astype(y_ref.dtype)

    @pl.when(c == pl.num_programs(1) - 1)
    def _():
        hl_ref[0] = h_ref[...]


def _s5_weights(lam_re, lam_im, log_dt, b_re, b_im, c_re, c_im, d_skip):
    lam = lax.complex(lam_re.astype(F32), lam_im.astype(F32))
    dt = jnp.exp(log_dt.astype(F32))[:, None]
    lam_bar = jnp.exp(lam * dt)
    b_bar = ((lam_bar - 1.0) / lam)[..., None] * lax.complex(b_re.astype(F32), b_im.astype(F32))
    eye = jnp.eye(N_SSM_GROUPS, dtype=F32)
    def in_bd(b):
        return jnp.einsum("gph,gk->ghkp", b, eye).reshape(SSM_WIDTH, N_STATE)
    bbd = jnp.concatenate([in_bd(b_bar.real), in_bd(b_bar.imag)], axis=1)
    def out_bd(cm):
        return jnp.einsum("ghp,gk->gpkh", cm, eye).reshape(N_STATE, SSM_WIDTH)
    cbd = jnp.concatenate([out_bd(c_re.astype(F32)), -out_bd(c_im.astype(F32))], axis=0)
    lam_row = jnp.concatenate([lam_bar.real.reshape(1, N_STATE), lam_bar.imag.reshape(1, N_STATE)], axis=1)
    return bbd.astype(BF16), lam_row, cbd.astype(BF16), d_skip.astype(F32).reshape(1, SSM_WIDTH)


def _s5(u, z, h0, s5w, chunk, out_dtype):
    bbd, lam_row, cbd, d_row = s5w
    s, t, _ = u.shape
    const = lambda shape: pl.BlockSpec(shape, lambda b, c: (0,) * len(shape))
    return pl.pallas_call(
        _s5_kernel,
        grid=(s, t // chunk),
        in_specs=[
            pl.BlockSpec((1, chunk, SSM_WIDTH), lambda b, c: (b, c, 0)),
            pl.BlockSpec((1, chunk, SSM_WIDTH), lambda b, c: (b, c, 0)),
            pl.BlockSpec((1, 1, 2 * N_STATE), lambda b, c: (b, 0, 0)),
            const((SSM_WIDTH, 2 * N_STATE)), const((1, 2 * N_STATE)),
            const((2 * N_STATE, SSM_WIDTH)), const((1, SSM_WIDTH)),
        ],
        out_specs=[
            pl.BlockSpec((1, chunk, SSM_WIDTH), lambda b, c: (b, c, 0)),
            pl.BlockSpec((1, 1, 2 * N_STATE), lambda b, c: (b, 0, 0)),
        ],
        out_shape=[jax.ShapeDtypeStruct((s, t, SSM_WIDTH), out_dtype),
                   jax.ShapeDtypeStruct((s, 1, 2 * N_STATE), F32)],
        scratch_shapes=[pltpu.VMEM((chunk, 2 * N_STATE), F32),
                        pltpu.VMEM((chunk, 2 * N_STATE), F32),
                        pltpu.VMEM((1, 2 * N_STATE), F32)],
        compiler_params=_cparams("parallel", "arbitrary"),
        name="s5",
    )(u, z, h0, bbd, lam_row, cbd, d_row)


def _outproj_kernel(x_ref, a_ref, s_ref, wa_ref, ws_ref, y_ref):
    y = x_ref[...]
    y = y + _dot(a_ref[...].astype(BF16), wa_ref[...])
    y = y + _dot(s_ref[...].astype(BF16), ws_ref[...])
    y_ref[...] = y


def _outproj_weights(w_out):
    wa = w_out[:NSA_WIDTH].reshape(N_KV, HPG, 1, HEAD_DIM, D_MODEL)
    slot = jnp.eye(N_KV, dtype=F32).reshape(N_KV, 1, N_KV, 1, 1)
    return (wa * slot).reshape(QPAD, D_MODEL).astype(BF16), w_out[NSA_WIDTH:].astype(BF16)


def _outproj(x2d, a_out, s_out, wa, ws, tm):
    n = x2d.shape[0]
    row = lambda w: pl.BlockSpec((tm, w), lambda i: (i, 0))
    return pl.pallas_call(
        _outproj_kernel,
        grid=(n // tm,),
        in_specs=[row(D_MODEL), row(QPAD), row(SSM_WIDTH),
                  pl.BlockSpec((QPAD, D_MODEL), lambda i: (0, 0)),
                  pl.BlockSpec((SSM_WIDTH, D_MODEL), lambda i: (0, 0))],
        out_specs=row(D_MODEL),
        out_shape=jax.ShapeDtypeStruct((n, D_MODEL), F32),
        compiler_params=_cparams("parallel"),
        name="outproj",
    )(x2d, a_out, s_out, wa, ws)


PEER_HALF = PEER_HEADS * PEER_DK // 2
_CAND_COUNTS = tuple(PEER_TOPK // (a + 1) for a in range(PEER_TOPK))
N_CAND = sum(_CAND_COUNTS)
N_CAND_PAD = -(-N_CAND // 8) * 8


def _peer_route_kernel(y_ref, g_ref, wq_ref, k1_ref, k2_ref, xn_ref, c1_ref, e1_ref, r2_ref, e2_ref,
                       s1_ref, s2_ref, v1_ref, v2_ref, cand_ref, rank1_ref, rank2_ref, rankc_ref):
    xn = _rmsnorm(y_ref[...], g_ref[...]).astype(BF16)
    xn_ref[...] = xn
    q = _dot(xn, wq_ref[...]).astype(BF16)
    s1_ref[...] = _nt_dot(k1_ref[...], q[:, :PEER_HALF])
    s2_ref[...] = _nt_dot(k2_ref[...], q[:, PEER_HALF:])
    t = y_ref.shape[0]
    krows = lambda h: slice(h * N_KEYS, (h + 1) * N_KEYS)
    crows = lambda h: slice(h * N_CAND_PAD, (h + 1) * N_CAND_PAD)

    def key_ranks(exact):
        worst = jnp.zeros((1, t), F32)
        for h in range(PEER_HEADS):
            for s_ref, rank_ref, v_ref in ((s1_ref, rank1_ref, v1_ref), (s2_ref, rank2_ref, v2_ref)):
                rank = _take_top(s_ref[krows(h), :], PEER_TOPK, v_ref, h * PEER_TOPK, exact)
                rank_ref[krows(h), :] = rank
                worst = jnp.maximum(worst, _n_taken(rank, PEER_TOPK))
        return worst

    worst = key_ranks(False)

    @pl.when(jnp.max(worst) > float(PEER_TOPK))
    def _():
        key_ranks(True)

    for h in range(PEER_HEADS):
        v1 = v1_ref[h * PEER_TOPK:(h + 1) * PEER_TOPK, :]
        v2 = v2_ref[h * PEER_TOPK:(h + 1) * PEER_TOPK, :]
        off = h * N_CAND_PAD
        for a, nb in enumerate(_CAND_COUNTS):
            cand_ref[off:off + nb, :] = v1[a:a + 1, :] + v2[0:nb, :]
            off += nb
        cand_ref[off:(h + 1) * N_CAND_PAD, :] = jnp.full(((h + 1) * N_CAND_PAD - off, t), LOWEST, F32)

    def cand_ranks(exact):
        worst = jnp.zeros((1, t), F32)
        for h in range(PEER_HEADS):
            rank = _take_top(cand_ref[crows(h), :], PEER_TOPK, exact=exact)
            rankc_ref[crows(h), :] = rank
            worst = jnp.maximum(worst, _n_taken(rank, PEER_TOPK))
        return worst

    worst = cand_ranks(False)

    @pl.when(jnp.max(worst) > float(PEER_TOPK))
    def _():
        cand_ranks(True)

    for h in range(PEER_HEADS):
        rows = krows(h)
        top = slice(h * PEER_TOPK, h * PEER_TOPK + 1)
        cand = cand_ref[crows(h), :]
        taken = jnp.where(rankc_ref[crows(h), :] < float(PEER_TOPK), 1.0, 0.0)
        z = jnp.sum(taken * jnp.exp(cand - cand[0:1, :]), axis=0, keepdims=True)
        rank1 = rank1_ref[rows, :]
        count = jnp.zeros((N_KEYS, t), F32)
        off = 0
        for a, nb in enumerate(_CAND_COUNTS):
            n_a = jnp.sum(taken[off:off + nb, :], axis=0, keepdims=True)
            count = jnp.where(rank1 == float(a), n_a, count)
            off += nb
        c1_ref[rows, :] = count
        e1_ref[rows, :] = jnp.exp(s1_ref[rows, :] - v1_ref[top, :]) / z
        r2_ref[rows, :] = rank2_ref[rows, :].astype(BF16)
        e2_ref[rows, :] = jnp.exp(s2_ref[rows, :] - v2_ref[top, :]).astype(BF16)


def _peer_weights(w_q, sub_k1, sub_k2):
    wq = w_q.reshape(D_MODEL, PEER_HEADS, 2, PEER_DK // 2).transpose(0, 2, 1, 3).reshape(D_MODEL, 2 * PEER_HALF)
    eye = jnp.eye(PEER_HEADS, dtype=F32)
    bd = lambda k: jnp.einsum("hkd,hj->hkjd", k, eye).reshape(PEER_HEADS * N_KEYS, PEER_HALF)
    return wq.astype(BF16), bd(sub_k1).astype(BF16), bd(sub_k2).astype(BF16)


def _peer_route(y2d, norm_g, wq, k1bd, k2bd, tm):
    n = y2d.shape[0]
    hk = PEER_HEADS * N_KEYS
    const = lambda a: pl.BlockSpec(a.shape, lambda i: (0, 0))
    col = lambda r: pl.BlockSpec((r, tm), lambda i: (0, i))
    tshape = lambda dt: jax.ShapeDtypeStruct((hk, n), dt)
    return pl.pallas_call(
        _peer_route_kernel,
        grid=(n // tm,),
        in_specs=[pl.BlockSpec((tm, D_MODEL), lambda i: (i, 0)),
                  pl.BlockSpec((1, D_MODEL), lambda i: (0, 0)), const(wq), const(k1bd), const(k2bd)],
        out_specs=[pl.BlockSpec((tm, D_MODEL), lambda i: (i, 0)), col(hk), col(hk), col(hk), col(hk)],
        out_shape=[jax.ShapeDtypeStruct((n, D_MODEL), BF16),
                   tshape(F32),
                   tshape(F32),
                   tshape(BF16),
                   tshape(BF16)],
        scratch_shapes=[pltpu.VMEM((hk, tm), F32), pltpu.VMEM((hk, tm), F32),
                        pltpu.VMEM((PEER_HEADS * PEER_TOPK, tm), F32),
                        pltpu.VMEM((PEER_HEADS * PEER_TOPK, tm), F32),
                        pltpu.VMEM((PEER_HEADS * N_CAND_PAD, tm), F32),
                        pltpu.VMEM((hk, tm), F32), pltpu.VMEM((hk, tm), F32),
                        pltpu.VMEM((PEER_HEADS * N_CAND_PAD, tm), F32)],
        compiler_params=_cparams("parallel"),
        name="peer_route",
    )(y2d, norm_g.reshape(1, D_MODEL), wq, k1bd, k2bd)


EXPERT_BLOCK = 1024
I1_PER_BLOCK = EXPERT_BLOCK // N_KEYS
N_EXPERT_BLOCKS = N_EXPERTS // EXPERT_BLOCK


def _peer_dense_kernel(y_ref, xn_ref, u_ref, vt_ref, c1_ref, e1_ref, r2_ref, e2_ref,
                       gf_ref, out_ref, acc_ref, act_ref, ga_ref):
    j = pl.program_id(1)
    t = xn_ref.shape[0]

    @pl.when(j == 0)
    def _():
        acc_ref[...] = jnp.zeros(acc_ref.shape, F32)
        act_ref[...] = jnp.zeros(act_ref.shape, BF16)
        ga_ref[...] = jnp.zeros(ga_ref.shape, BF16)

    def gate_stage(prev):
        jb = j - 1
        for ii in range(I1_PER_BLOCK):
            erows = slice(ii * N_KEYS, (ii + 1) * N_KEYS)
            for tc in range(t // LANES):
                cols = slice(tc * LANES, (tc + 1) * LANES)
                gate = jnp.zeros((N_KEYS, LANES), BF16)
                for h in range(PEER_HEADS):
                    grp = pl.ds(pl.multiple_of(h * N_KEYS + jb * I1_PER_BLOCK, I1_PER_BLOCK), I1_PER_BLOCK)
                    krows = slice(h * N_KEYS, (h + 1) * N_KEYS)
                    count = c1_ref[grp, cols][ii:ii + 1, :].astype(BF16)
                    e1 = e1_ref[grp, cols][ii:ii + 1, :].astype(BF16)
                    gate = gate + jnp.where(r2_ref[krows, cols] < count, e2_ref[krows, cols] * e1,
                                            jnp.zeros((), BF16))
                ga_ref[prev, erows, cols] = gate * act_ref[prev, erows, cols]

    def stages(cur, prev):
        pl.when((j >= 1) & (j <= N_EXPERT_BLOCKS))(functools.partial(gate_stage, prev))

        def output_matmul(rows):
            acc_ref[rows, :] += _dot(vt_ref[rows, :], ga_ref[cur])

        def activations(rows):
            act_ref[cur, rows, :] = _gelu_tanh(_nt_dot(u_ref[rows, :], xn_ref[...])).astype(BF16)

        everything = slice(None)
        halves = (slice(0, D_MODEL // 2), slice(D_MODEL // 2, D_MODEL))
        assert EXPERT_BLOCK == D_MODEL

        @pl.when((j >= 2) & (j < N_EXPERT_BLOCKS))
        def _():
            output_matmul(everything)
            activations(everything)

        @pl.when(j < 2)
        def _():
            for rows in halves:
                activations(rows)

        @pl.when(j >= N_EXPERT_BLOCKS)
        def _():
            for rows in halves:
                output_matmul(rows)

    for parity in range(2):
        pl.when(j % 2 == parity)(functools.partial(stages, parity, 1 - parity))

    @pl.when(j == pl.num_programs(1) - 1)
    def _():
        y = y_ref[...] + acc_ref[...].T
        out_ref[...] = _rmsnorm(y, gf_ref[...])


def _peer_dense(y2d, xn, u_bf, vt_bf, c1, e1, r2, e2, norm_f, tm):
    n = y2d.shape[0]
    hk = PEER_HEADS * N_KEYS
    tok = lambda w: pl.BlockSpec((tm, w), lambda i, j: (i, 0))
    col = lambda r: pl.BlockSpec((r, tm), lambda i, j: (0, i))
    last = N_EXPERT_BLOCKS - 1
    return pl.pallas_call(
        _peer_dense_kernel,
        grid=(n // tm, N_EXPERT_BLOCKS + 2),
        in_specs=[tok(D_MODEL), tok(D_MODEL),
                  pl.BlockSpec((EXPERT_BLOCK, D_MODEL), lambda i, j: (jnp.minimum(j, last), 0)),
                  pl.BlockSpec((D_MODEL, EXPERT_BLOCK), lambda i, j: (0, jnp.maximum(j - 2, 0))),
                  col(hk), col(hk), col(hk), col(hk),
                  pl.BlockSpec((1, D_MODEL), lambda i, j: (0, 0))],
        out_specs=tok(D_MODEL),
        out_shape=jax.ShapeDtypeStruct((n, D_MODEL), F32),
        scratch_shapes=[pltpu.VMEM((D_MODEL, tm), F32),
                        pltpu.VMEM((2, EXPERT_BLOCK, tm), BF16),
                        pltpu.VMEM((2, EXPERT_BLOCK, tm), BF16)],
        compiler_params=_cparams("parallel", "arbitrary"),
        name="peer_dense",
    )(y2d, xn, u_bf, vt_bf, c1, e1, r2, e2, norm_f.reshape(1, D_MODEL))


PROJ_TILE = 512
PEER_TILE = 256
PEER_DENSE_TILE = 512
S5_CHUNK = 256


def _state_to_rows(st):
    s = st.shape[0]
    return jnp.concatenate([st[..., 0].reshape(s, 1, N_STATE), st[..., 1].reshape(s, 1, N_STATE)], axis=-1)


def _rows_to_state(h):
    s = h.shape[0]
    shape = (s, N_SSM_GROUPS, SSM_STATE)
    return jnp.stack([h[:, 0, :N_STATE].reshape(shape), h[:, 0, N_STATE:].reshape(shape)], axis=-1)


def _feat_to_tokens(a, n_types):
    s, _, t = a.shape
    return a.reshape(s, n_types, N_KV, HEAD_DIM, t).transpose(0, 4, 1, 2, 3)


def _peer_block(y2d, norm_g, peer_w, norm_f, tm):
    wq, k1bd, k2bd, u_bf, vt_bf = peer_w
    xn, c1, e1, r2, e2 = _peer_route(y2d, norm_g, wq, k1bd, k2bd, tm)
    return _peer_dense(y2d, xn, u_bf, vt_bf, c1, e1, r2, e2, norm_f, min(PEER_DENSE_TILE, y2d.shape[0]))


def kernel(x_prompt, x_sample, cache_kv, cache_win, state_ssm, page_table, norm_mix, w_in, w_cmp1, w_cmp2, pe_cmp, lam_re, lam_im, log_dt, b_re, b_im, c_re, c_im, d_skip, w_out, norm_ffn, w_q_peer, sub_k1, sub_k2, u_tab, v_tab, norm_final):
    b, t, d = x_prompt.shape
    db, ts, _ = x_sample.shape
    assert w_in.shape[0] == DEPTH == 1 and d == D_MODEL
    l = 0
    n_pool = cache_kv.shape[1]
    wb = cache_win.shape[2]

    w_tok, w_feat = _proj_weights(w_in[l])
    cmp_w = _compress_weights(w_cmp1[l], w_cmp2[l], pe_cmp[l])
    s5_w = _s5_weights(lam_re[l], lam_im[l], log_dt[l], b_re[l], b_im[l], c_re[l], c_im[l], d_skip[l])
    wa, ws = _outproj_weights(w_out[l])
    peer_w = _peer_weights(w_q_peer[l], sub_k1[l], sub_k2[l]) + (
        u_tab[l].astype(BF16), v_tab[l].T.astype(BF16))

    q, gate, u, z, kswk, q_t, kvsel_t, ksvs_t, kvwin_t, kwvw_t = _project(
        x_prompt, norm_mix[l], w_tok, w_feat, PROJ_TILE, BF16)
    n_pg = t // PAGE_SIZE
    table_p = jnp.broadcast_to(jnp.arange(n_pg, dtype=jnp.int32), (b, n_pg))
    prompt_page = lambda k: pl.BlockSpec((None, 2, LANES, PAGE_SIZE),
                                         lambda bb, j, tb: (bb, 0, 0, tb[bb, j * CMP_PAGES + k]))
    cmp_p = _compress(kvsel_t.reshape(b, 4, LANES, t), prompt_page, table_p, *cmp_w)
    a_p = _nsa_prompt_t(q.reshape(b, t, QPAD), q_t, gate.reshape(b, t, LANES),
                        kswk.reshape(b, t, 2 * KV_WIDTH), ksvs_t, kwvw_t, cmp_p)
    s_p, h_p = _s5(u.reshape(b, t, SSM_WIDTH), z.reshape(b, t, SSM_WIDTH),
                   jnp.zeros((b, 1, 2 * N_STATE), F32), s5_w, S5_CHUNK, BF16)
    y1p = _outproj(x_prompt.reshape(b * t, d), a_p.reshape(b * t, QPAD), s_p.reshape(b * t, SSM_WIDTH),
                   wa, ws, PROJ_TILE)
    y_prompt = _peer_block(y1p, norm_ffn[l], peer_w, norm_final, PEER_TILE).reshape(b, t, d)
    kv_prompt = _feat_to_tokens(kvsel_t, 4)
    win_prompt = _feat_to_tokens(kvwin_t[:, :, t - min(WINDOW, t):], 2)
    ssm_prompt = _rows_to_state(h_p)

    qs, gate_s, u_s, z_s, _, _, kvsel_st, ksvs_st, kvwin_st, kwvw_st = _project(
        x_sample.reshape(1, db * ts, d), norm_mix[l], w_tok, w_feat, db * ts, F32)
    pages_s = cache_kv[l].transpose(0, 2, 3, 4, 1).reshape(n_pool, 4, LANES, PAGE_SIZE)
    sample_page = lambda k: pl.BlockSpec((None, 2, LANES, PAGE_SIZE),
                                         lambda bb, j, tb: (tb[bb, j * CMP_PAGES + k], 0, 0, 0))
    cmp_s = _compress(pages_s, sample_page, page_table, *cmp_w)
    new_page = lambda a: jnp.pad(a[0].reshape(2 * LANES, db, ts).transpose(1, 0, 2),
                                 ((0, 0), (0, 0), (0, PAGE_SIZE - ts)))
    win_t = cache_win[l].transpose(0, 2, 3, 4, 1).reshape(db, 2 * LANES, wb)
    a_s = _nsa_sample(qs.reshape(db, ts, QPAD), gate_s.reshape(db, ts, LANES), pages_s, page_table, cmp_s,
                      win_t, new_page(ksvs_st), new_page(kwvw_st))
    s_s, h_s = _s5(u_s.reshape(db, ts, SSM_WIDTH), z_s.reshape(db, ts, SSM_WIDTH),
                   _state_to_rows(state_ssm[l].astype(F32)), s5_w, ts, F32)
    y1s = _outproj(x_sample.reshape(db * ts, d), a_s.reshape(db * ts, QPAD), s_s.reshape(db * ts, SSM_WIDTH),
                   wa, ws, db * ts)
    y_sample = _peer_block(y1s, norm_ffn[l], peer_w, norm_final, PEER_TILE).reshape(db, ts, d)
    per_tok = lambda a, n_types: a[0].reshape(n_types, N_KV, HEAD_DIM, db, ts).transpose(3, 4, 0, 1, 2)
    kv_sample = per_tok(kvsel_st, 4)
    win_new = per_tok(kvwin_st, 2).astype(cache_win.dtype)
    win_sample = jnp.concatenate([cache_win[l], win_new], axis=1)[:, ts:]
    ssm_sample = _rows_to_state(h_s)

    return (y_prompt, y_sample, kv_prompt[None], kv_sample[None], win_prompt[None], win_sample[None],
            ssm_prompt[None], ssm_sample[None])
```
